```python
import math
import jax, jax.numpy as jnp
from jax import lax
import numpy as np

D_MODEL = 1024
BATCH = 32
SEQ = 256
DEPTH = 1
DEC_BATCH = 8
DEC_SEQ = 2048
PAST_LEN = 512

GRID_W = 64
MIX_WIDTH = D_MODEL
ATT_WIDTH = MIX_WIDTH // 2
N_DIFF_HEADS = 4
DIFF_HEAD_DIM = ATT_WIDTH // (2 * N_DIFF_HEADS)
VAL_DIM = 2 * DIFF_HEAD_DIM
MLP_WIDTH = MIX_WIDTH - ATT_WIDTH
N_SGU_GROUPS = 4
SGU_GROUP_DIM = MLP_WIDTH // N_SGU_GROUPS
CHUNK = 128
Q_BLOCK = 128
D_FF = -(-(8 * D_MODEL) // (3 * 256)) * 256
IN_WIDTH = 3 * ATT_WIDTH + 2 * MLP_WIDTH
ROPE_THETA = 10000.0
ROPE_AXIS_DIM = DIFF_HEAD_DIM // 2
EPS = 1e-6

kernel_name = "hybrid_diffattn_gmlp_prefix_dit_step"


def _rms(x, g):
    xf = x.astype(jnp.float32)
    y = xf * lax.rsqrt(jnp.mean(xf * xf, axis=-1, keepdims=True) + EPS)
    return (y * g.astype(jnp.float32)).astype(x.dtype)


def _rms_nogain(x):
    xf = x.astype(jnp.float32)
    y = xf * lax.rsqrt(jnp.mean(xf * xf, axis=-1, keepdims=True) + EPS)
    return y.astype(x.dtype)


def _axial_rope_tables(n):
    rows = n // GRID_W
    row = jnp.repeat(jnp.arange(rows), GRID_W).astype(jnp.float32)
    col = jnp.tile(jnp.arange(GRID_W), rows).astype(jnp.float32)
    inv = ROPE_THETA ** (-jnp.arange(0, ROPE_AXIS_DIM, 2, dtype=jnp.float32) / ROPE_AXIS_DIM)
    ang_r = row[:, None] * inv[None, :]
    ang_c = col[:, None] * inv[None, :]
    return (jnp.cos(ang_r), jnp.sin(ang_r), jnp.cos(ang_c), jnp.sin(ang_c))


def _rotate(a, cos, sin):
    half = a.shape[-1] // 2
    a1, a2 = a[..., :half], a[..., half:]
    cos = cos.astype(a.dtype)
    sin = sin.astype(a.dtype)
    return jnp.concatenate([a1 * cos - a2 * sin, a2 * cos + a1 * sin], axis=-1)


def _apply_axial_rope(x, tables):
    cos_r, sin_r, cos_c, sin_c = tables
    xr, xc = x[..., :ROPE_AXIS_DIM], x[..., ROPE_AXIS_DIM:]
    return jnp.concatenate([_rotate(xr, cos_r, sin_r), _rotate(xc, cos_c, sin_c)], axis=-1)


def _project(xm, w_in, q_g, k_g):
    B, S, _ = xm.shape
    h = xm @ w_in
    q, k, v, u, gv = jnp.split(
        h, [ATT_WIDTH, 2 * ATT_WIDTH, 3 * ATT_WIDTH, 3 * ATT_WIDTH + MLP_WIDTH], axis=-1)
    q = _rms(q.reshape(B, S, N_DIFF_HEADS, 2, DIFF_HEAD_DIM), q_g).transpose(0, 2, 3, 1, 4)
    k = _rms(k.reshape(B, S, N_DIFF_HEADS, 2, DIFF_HEAD_DIM), k_g).transpose(0, 2, 3, 1, 4)
    v = v.reshape(B, S, N_DIFF_HEADS, VAL_DIM).transpose(0, 2, 1, 3)
    return q, k, v, u, gv


def _diff_attention(q, k, v, lam):
    B, H, _, Nq, DH = q.shape
    nb = Nq // Q_BLOCK
    scale = 1.0 / math.sqrt(DH)
    qb = jnp.moveaxis(q.reshape(B, H, 2, nb, Q_BLOCK, DH), 3, 0)

    def block(qblk):
        s = jnp.einsum('bhiqd,bhikd->bhiqk', qblk, k).astype(jnp.float32) * scale
        p = jax.nn.softmax(s, axis=-1)
        a = p[:, :, 0] - lam * p[:, :, 1]
        return jnp.einsum('bhqk,bhkd->bhqd', a.astype(v.dtype), v)

    out = lax.map(block, qb)
    return jnp.moveaxis(out, 0, 2).reshape(B, H, Nq, VAL_DIM)


def _diff_out(o, lambda_init, att_out_g):
    B, H, S, _ = o.shape
    o = _rms_nogain(o) * (1.0 - lambda_init)
    return o.transpose(0, 2, 1, 3).reshape(B, S, ATT_WIDTH) * att_out_g


def _chunk_mlp(u, gv, sgu_norm_g, sgu_w, sgu_b, mlp_out_g):
    B, S, _ = u.shape
    n = S // CHUNK
    gv = _rms(gv.reshape(B, n, CHUNK, N_SGU_GROUPS, SGU_GROUP_DIM),
              sgu_norm_g.reshape(N_SGU_GROUPS, SGU_GROUP_DIM))
    s = jnp.einsum('gpq,bnqgc->bnpgc', sgu_w, gv) + sgu_b.T[:, :, None]
    out = (u.reshape(B, n, CHUNK, N_SGU_GROUPS, SGU_GROUP_DIM) * s).reshape(B, S, MLP_WIDTH)
    return _rms(out, mlp_out_g)


def _layer(x, cond, k_ctx, v_ctx, lp, lambda_init, rope):
    (norm1_g, norm2_g, w_ada, b_ada, w_in, q_norm_g, k_norm_g, lambda_q1, lambda_k1,
     lambda_q2, lambda_k2, att_out_g, sgu_norm_g, sgu_w, sgu_b, mlp_out_g, w_out,
     w_ffn_in, w_ffn_out) = lp
    mods = jax.nn.silu(cond) @ w_ada + b_ada
    shift1, scale1, gate1, shift2, scale2, gate2 = [m[:, None, :] for m in jnp.split(mods, 6, axis=-1)]

    xm = _rms(x, norm1_g) * (1.0 + scale1) + shift1
    q, k, v, u, gv = _project(xm, w_in, q_norm_g, k_norm_g)
    if rope is None:
        k_ctx, v_ctx = k, v
        k_all, v_all = k, v
    else:
        q = _apply_axial_rope(q, rope)
        k = _apply_axial_rope(k, rope)
        k_all = jnp.concatenate([k, k_ctx.astype(k.dtype)], axis=3)
        v_all = jnp.concatenate([v, v_ctx.astype(v.dtype)], axis=2)
    lam = (jnp.exp(jnp.sum(lambda_q1.astype(jnp.float32) * lambda_k1.astype(jnp.float32)))
           - jnp.exp(jnp.sum(lambda_q2.astype(jnp.float32) * lambda_k2.astype(jnp.float32)))
           + lambda_init)
    att = _diff_out(_diff_attention(q, k_all, v_all, lam), lambda_init, att_out_g)
    mlp = _chunk_mlp(u, gv, sgu_norm_g, sgu_w, sgu_b, mlp_out_g)
    x = x + gate1 * (jnp.concatenate([att, mlp], axis=-1) @ w_out)

    xm2 = _rms(x, norm2_g) * (1.0 + scale2) + shift2
    g, up = jnp.split(xm2 @ w_ffn_in, 2, axis=-1)
    x = x + gate2 * ((jax.nn.silu(g) * up) @ w_ffn_out)
    return x, k_ctx, v_ctx


def setup_inputs(seed: int = 0) -> dict:
    key = jax.random.key(seed)
    ks = jax.random.split(key, 26)
    f32 = jnp.float32

    def nrm(k, shape, scale):
        return jax.random.normal(k, shape, f32) * scale

    return {
        "x_prompt": nrm(ks[0], (BATCH, SEQ, D_MODEL), 1.0),
        "x_sample": nrm(ks[1], (DEC_BATCH, DEC_SEQ, D_MODEL), 1.0),
        "cache_k_ctx": nrm(ks[2], (DEC_BATCH, DEPTH, N_DIFF_HEADS, 2, PAST_LEN, DIFF_HEAD_DIM), 1.0),
        "cache_v_ctx": nrm(ks[3], (DEC_BATCH, DEPTH, N_DIFF_HEADS, PAST_LEN, VAL_DIM), 1.0),
        "c": nrm(ks[4], (DEC_BATCH, D_MODEL), 1.0),
        "c_ctx": nrm(ks[5], (D_MODEL,), 1.0),
        "norm1_g": 1.0 + nrm(ks[6], (DEPTH, D_MODEL), 0.02),
        "norm2_g": 1.0 + nrm(ks[7], (DEPTH, D_MODEL), 0.02),
        "w_ada": nrm(ks[8], (DEPTH, D_MODEL, 6 * D_MODEL), D_MODEL ** -0.5),
        "b_ada": nrm(ks[9], (DEPTH, 6 * D_MODEL), 0.02),
        "w_in": nrm(ks[10], (DEPTH, D_MODEL, IN_WIDTH), D_MODEL ** -0.5),
        "q_norm_g": 1.0 + nrm(ks[11], (DEPTH, DIFF_HEAD_DIM), 0.02),
        "k_norm_g": 1.0 + nrm(ks[12], (DEPTH, DIFF_HEAD_DIM), 0.02),
        "lambda_q1": nrm(ks[13], (DEPTH, DIFF_HEAD_DIM), 0.1),
        "lambda_k1": nrm(ks[14], (DEPTH, DIFF_HEAD_DIM), 0.1),
        "lambda_q2": nrm(ks[15], (DEPTH, DIFF_HEAD_DIM), 0.1),
        "lambda_k2": nrm(ks[16], (DEPTH, DIFF_HEAD_DIM), 0.1),
        "att_out_g": 1.0 + nrm(ks[17], (DEPTH, ATT_WIDTH), 0.02),
        "sgu_norm_g": 1.0 + nrm(ks[18], (DEPTH, MLP_WIDTH), 0.02),
        "sgu_w": nrm(ks[19], (DEPTH, N_SGU_GROUPS, CHUNK, CHUNK), CHUNK ** -0.5),
        "sgu_b": 1.0 + nrm(ks[20], (DEPTH, N_SGU_GROUPS, CHUNK), 0.02),
        "mlp_out_g": 1.0 + nrm(ks[21], (DEPTH, MLP_WIDTH), 0.02),
        "w_out": nrm(ks[22], (DEPTH, MIX_WIDTH, D_MODEL), MIX_WIDTH ** -0.5),
        "w_ffn_in": nrm(ks[23], (DEPTH, D_MODEL, 2 * D_FF), D_MODEL ** -0.5),
        "w_ffn_out": nrm(ks[24], (DEPTH, D_FF, D_MODEL), D_FF ** -0.5),
    }


def reference(x_prompt, x_sample, cache_k_ctx, cache_v_ctx, c, c_ctx, norm1_g, norm2_g,
              w_ada, b_ada, w_in, q_norm_g, k_norm_g, lambda_q1, lambda_k1, lambda_q2,
              lambda_k2, att_out_g, sgu_norm_g, sgu_w, sgu_b, mlp_out_g, w_out,
              w_ffn_in, w_ffn_out):
    rope = _axial_rope_tables(x_sample.shape[1])
    xp, xs = x_prompt, x_sample
    k_states, v_states = [], []
    for l in range(DEPTH):
        lp = (norm1_g[l], norm2_g[l], w_ada[l], b_ada[l], w_in[l], q_norm_g[l], k_norm_g[l],
              lambda_q1[l], lambda_k1[l], lambda_q2[l], lambda_k2[l], att_out_g[l],
              sgu_norm_g[l], sgu_w[l], sgu_b[l], mlp_out_g[l], w_out[l], w_ffn_in[l],
              w_ffn_out[l])
        lambda_init = 0.8 - 0.6 * math.exp(-0.3 * l)
        xp, k_c, v_c = _layer(xp, c_ctx[None, :], None, None, lp, lambda_init, None)
        k_states.append(k_c)
        v_states.append(v_c)
        xs, _, _ = _layer(xs, c, cache_k_ctx[:, l], cache_v_ctx[:, l], lp, lambda_init, rope)
    state_k_ctx = jnp.stack(k_states, axis=1)
    state_v_ctx = jnp.stack(v_states, axis=1)
    return (xp, xs, state_k_ctx, state_v_ctx)
```

```python
import functools
import math

import jax
import jax.numpy as jnp
from jax import lax
from jax.experimental import pallas as pl
from jax.experimental.pallas import tpu as pltpu

D_MODEL = 1024
ATT_WIDTH = 512
N_HEADS = 4
HEAD_DIM = 64
VAL_DIM = 128
MLP_WIDTH = 512
N_GROUPS = 4
GROUP_DIM = 128
CHUNK = 128
D_FF = 2816
IN_WIDTH = 2560
GRID_W = 64
ROPE_THETA = 10000.0
ROPE_AXIS_DIM = 32
EPS = 1e-6
LOG2E = 1.4426950408889634
LANES = 128

F32 = jnp.float32
BF16 = jnp.bfloat16

VMEM_LIMIT_BYTES = 56 * 1024 * 1024
MODS_ROWS = 16
CTX_ROW = 8

TM_PROJ = 512
TM_FFN = 512
TQ = 256
FF_CHUNKS = ((0, 1024), (1024, 1024), (2048, 768))


def _const_spec(shape):
    zeros = (0,) * len(shape)
    return pl.BlockSpec(shape, lambda *_: zeros, pipeline_mode=pl.Buffered(1))


def _params(n_grid):
    return pltpu.CompilerParams(
        dimension_semantics=("arbitrary",) * n_grid,
        vmem_limit_bytes=VMEM_LIMIT_BYTES,
    )


def _rms_scale(x):
    return lax.rsqrt(jnp.mean(x * x, axis=-1, keepdims=True) + EPS)


def _mods_kernel(cond_ref, w_ref, b_ref, o_ref):
    cnd = cond_ref[...]
    act = (cnd * jax.nn.sigmoid(cnd)).astype(BF16)
    o_ref[...] = jnp.dot(act, w_ref[...].astype(BF16), preferred_element_type=F32) + b_ref[...]


def _mods(cond, w_ada, b_ada):
    tn = 1536
    n_out = w_ada.shape[1]
    return pl.pallas_call(
        _mods_kernel,
        grid=(n_out // tn,),
        in_specs=[
            _const_spec((MODS_ROWS, D_MODEL)),
            pl.BlockSpec((D_MODEL, tn), lambda j: (0, j)),
            pl.BlockSpec((1, tn), lambda j: (0, j)),
        ],
        out_specs=pl.BlockSpec((MODS_ROWS, tn), lambda j: (0, j)),
        out_shape=jax.ShapeDtypeStruct((MODS_ROWS, n_out), F32),
        compiler_params=_params(1),
        name="mods",
    )(cond, w_ada, b_ada)


def _proj_kernel(*refs, rope, seq_len, tm):
    it = iter(refs)
    x_ref, mods_ref, g1_ref, w_in_ref, pool_ref, qg_ref, kg_ref = (next(it) for _ in range(7))
    if rope:
        cq_ref, sq_ref, ck_ref, sk_ref = (next(it) for _ in range(4))
    sgun_ref, sguw_ref, sgub_ref, mlpg_ref = (next(it) for _ in range(4))
    q_ref, k_ref, v_ref, mlp_ref = (next(it) for _ in range(4))
    if not rope:
        kst_ref, vst_ref = (next(it) for _ in range(2))
    gate_ref = next(it)

    x = x_ref[...]
    xn = x * _rms_scale(x) * g1_ref[...]
    xm = xn * (1.0 + mods_ref[0, 1:2, :]) + mods_ref[0, 0:1, :]
    xb = xm.astype(BF16)

    def section(lo, hi):
        return jnp.dot(xb, w_in_ref[:, lo:hi], preferred_element_type=F32)

    def head_norm(h, g_ref):
        sq = (h * h).astype(BF16)
        msq = jnp.dot(sq, pool_ref[...], preferred_element_type=F32)
        return h * lax.rsqrt(msq + EPS) * g_ref[...]

    if rope:
        lane = lax.broadcasted_iota(jnp.int32, (tm, LANES), 1)
        first_half = (lane % 32) < 16

    def emit_rotary(hn, c_ref, s_ref, out_ref):
        for c in range(ATT_WIDTH // LANES):
            hc = hn[:, c * LANES:(c + 1) * LANES]
            swapped = jnp.where(first_half,
                                pltpu.roll(hc, LANES - 16, 1),
                                pltpu.roll(hc, 16, 1))
            out_ref[:, c * LANES:(c + 1) * LANES] = (
                hc * c_ref[...] + swapped * s_ref[...]).astype(BF16)

    qn = head_norm(section(0, ATT_WIDTH), qg_ref)
    if rope:
        emit_rotary(qn, cq_ref, sq_ref, q_ref)
    else:
        q_ref[...] = (qn * (LOG2E / math.sqrt(HEAD_DIM))).astype(BF16)

    kn = head_norm(section(ATT_WIDTH, 2 * ATT_WIDTH), kg_ref)
    if rope:
        emit_rotary(kn, ck_ref, sk_ref, k_ref)
    else:
        k_ref[...] = kn.astype(BF16)

    hv = section(2 * ATT_WIDTH, 3 * ATT_WIDTH)
    v_ref[...] = hv.astype(BF16)

    if not rope:
        for s in range(tm // seq_len):
            rows = slice(s * seq_len, (s + 1) * seq_len)
            for h in range(N_HEADS):
                for i in range(2):
                    j = 2 * h + i
                    kst_ref[s, 0, h, i, :, :] = kn[rows, j * HEAD_DIM:(j + 1) * HEAD_DIM]
                vst_ref[s, 0, h, :, :] = hv[rows, h * VAL_DIM:(h + 1) * VAL_DIM]

    hu = section(3 * ATT_WIDTH, 3 * ATT_WIDTH + MLP_WIDTH)
    hg = section(3 * ATT_WIDTH + MLP_WIDTH, IN_WIDTH)
    for g in range(N_GROUPS):
        cols = slice(g * GROUP_DIM, (g + 1) * GROUP_DIM)
        gg = hg[:, cols]
        gn = (gg * _rms_scale(gg) * sgun_ref[:, cols]).astype(BF16)
        ug = hu[:, cols]
        wg = sguw_ref[g]
        bg = sgub_ref[g]
        for n in range(tm // CHUNK):
            rows = slice(n * CHUNK, (n + 1) * CHUNK)
            sp = jnp.dot(wg, gn[rows, :], preferred_element_type=F32) + bg
            gate_ref[rows, cols] = ug[rows, :] * sp
    o = gate_ref[...]
    mlp_ref[...] = (o * _rms_scale(o) * mlpg_ref[...]).astype(BF16)


def _proj(x2d, mods3, g1, w_in, pool, qg, kg, rope_tabs, sgun, sguw, sgub, mlpg,
          *, seq_len, mods_row_fn):
    n_tok = x2d.shape[0]
    tm = TM_PROJ
    rope = rope_tabs is not None
    blocks_per_seq = seq_len // tm if rope else None

    in_specs = [
        pl.BlockSpec((tm, D_MODEL), lambda i: (i, 0)),
        pl.BlockSpec((1, 6, D_MODEL), lambda i: (mods_row_fn(i), 0, 0)),
        _const_spec((1, D_MODEL)),
        _const_spec((D_MODEL, IN_WIDTH)),
        _const_spec((ATT_WIDTH, ATT_WIDTH)),
        _const_spec((1, ATT_WIDTH)),
        _const_spec((1, ATT_WIDTH)),
    ]
    args = [x2d, mods3, g1, w_in, pool, qg, kg]
    if rope:
        tab_spec = pl.BlockSpec((tm, LANES), lambda i: (i % blocks_per_seq, 0))
        in_specs += [tab_spec] * 4
        args += list(rope_tabs)
    in_specs += [
        _const_spec((1, MLP_WIDTH)),
        _const_spec((N_GROUPS, CHUNK, CHUNK)),
        _const_spec((N_GROUPS, CHUNK, GROUP_DIM)),
        _const_spec((1, MLP_WIDTH)),
    ]
    args += [sgun, sguw, sgub, mlpg]

    tok_spec = pl.BlockSpec((tm, ATT_WIDTH), lambda i: (i, 0))
    out_specs = [tok_spec] * 4
    out_shape = [jax.ShapeDtypeStruct((n_tok, ATT_WIDTH), BF16)] * 4
    if not rope:
        n_seq = n_tok // seq_len
        spb = tm // seq_len
        out_specs += [
            pl.BlockSpec((spb, 1, N_HEADS, 2, seq_len, HEAD_DIM), lambda i: (i, 0, 0, 0, 0, 0)),
            pl.BlockSpec((spb, 1, N_HEADS, seq_len, VAL_DIM), lambda i: (i, 0, 0, 0, 0)),
        ]
        out_shape += [
            jax.ShapeDtypeStruct((n_seq, 1, N_HEADS, 2, seq_len, HEAD_DIM), F32),
            jax.ShapeDtypeStruct((n_seq, 1, N_HEADS, seq_len, VAL_DIM), F32),
        ]

    return pl.pallas_call(
        functools.partial(_proj_kernel, rope=rope, seq_len=seq_len, tm=tm),
        grid=(n_tok // tm,),
        in_specs=in_specs,
        out_specs=out_specs,
        out_shape=out_shape,
        scratch_shapes=[pltpu.VMEM((tm, MLP_WIDTH), F32)],
        compiler_params=_params(1),
        name="proj_rope" if rope else "proj_ctx",
    )(*args)


def _attn_kernel(*refs, lambda_init, has_cache):
    it = iter(refs)
    lq1, lk1, lq2, lk2, q_ref, k_ref, v_ref = (next(it) for _ in range(7))
    if has_cache:
        kc_ref, vc_ref = (next(it) for _ in range(2))
    ag_ref, o_ref = (next(it) for _ in range(2))

    lam = (jnp.exp(jnp.sum(lq1[...] * lk1[...], keepdims=True))
           - jnp.exp(jnp.sum(lq2[...] * lk2[...], keepdims=True))
           + lambda_init)

    q = q_ref[...]
    lane = lax.broadcasted_iota(jnp.int32, q.shape, 1)
    zero = jnp.zeros_like(q)
    q_maps = (jnp.where(lane < HEAD_DIM, q, zero), jnp.where(lane >= HEAD_DIM, q, zero))

    keys = [k_ref[...]] + ([kc_ref[0, 0]] if has_cache else [])
    vals = [v_ref[...]] + ([vc_ref[0, 0]] if has_cache else [])
    nt = (((1,), (1,)), ((), ()))

    def softmax_parts(qm):
        parts = [lax.dot_general(qm, kk, nt, preferred_element_type=F32) for kk in keys]
        m = functools.reduce(jnp.maximum, [jnp.max(p, axis=-1, keepdims=True) for p in parts])
        es = [jnp.exp2(p - m) for p in parts]
        denom = functools.reduce(jnp.add, [jnp.sum(e, axis=-1, keepdims=True) for e in es])
        return es, denom

    e1, d1 = softmax_parts(q_maps[0])
    e2, d2 = softmax_parts(q_maps[1])
    r1 = 1.0 / d1
    r2 = lam / d2
    acc = None
    for ea, eb, vv in zip(e1, e2, vals):
        a = (ea * r1 - eb * r2).astype(BF16)
        part = jnp.dot(a, vv, preferred_element_type=F32)
        acc = part if acc is None else acc + part
    o = acc * _rms_scale(acc) * (1.0 - lambda_init) * ag_ref[...]
    o_ref[...] = o.astype(BF16)


def _attn(lams, q, k, v, cache, att_g, *, n_batch, seq_len, tq, lambda_init):
    nq = seq_len // tq
    has_cache = cache is not None
    lam_spec = _const_spec((1, HEAD_DIM))
    in_specs = [lam_spec] * 4 + [
        pl.BlockSpec((tq, LANES), lambda b, h, i: (b * nq + i, h)),
        pl.BlockSpec((seq_len, LANES), lambda b, h, i: (b, h)),
        pl.BlockSpec((seq_len, LANES), lambda b, h, i: (b, h)),
    ]
    args = list(lams) + [q, k, v]
    if has_cache:
        kc, vc = cache
        past = kc.shape[2]
        cspec = pl.BlockSpec((1, 1, past, LANES), lambda b, h, i: (b, h, 0, 0))
        in_specs += [cspec, cspec]
        args += [kc, vc]
    in_specs.append(pl.BlockSpec((1, LANES), lambda b, h, i: (0, h)))
    args.append(att_g)
    return pl.pallas_call(
        functools.partial(_attn_kernel, lambda_init=lambda_init, has_cache=has_cache),
        grid=(n_batch, N_HEADS, nq),
        in_specs=in_specs,
        out_specs=pl.BlockSpec((tq, LANES), lambda b, h, i: (b * nq + i, h)),
        out_shape=jax.ShapeDtypeStruct((n_batch * seq_len, ATT_WIDTH), BF16),
        compiler_params=_params(3),
        name="attn_cache" if has_cache else "attn_ctx",
    )(*args)


def _ffn_kernel(x_ref, att_ref, mlp_ref, mods_ref, g2_ref, wo_ref, wfi_ref, wfo_ref, o_ref):
    y = (jnp.dot(att_ref[...], wo_ref[0:ATT_WIDTH, :], preferred_element_type=F32)
         + jnp.dot(mlp_ref[...], wo_ref[ATT_WIDTH:, :], preferred_element_type=F32))
    x1 = x_ref[...] + mods_ref[0, 2:3, :] * y
    xn = x1 * _rms_scale(x1) * g2_ref[...]
    xb = (xn * (1.0 + mods_ref[0, 4:5, :]) + mods_ref[0, 3:4, :]).astype(BF16)
    acc = None
    for c0, cw in FF_CHUNKS:
        gte = jnp.dot(xb, wfi_ref[:, c0:c0 + cw], preferred_element_type=F32)
        up = jnp.dot(xb, wfi_ref[:, D_FF + c0:D_FF + c0 + cw], preferred_element_type=F32)
        act = (gte * jax.nn.sigmoid(gte) * up).astype(BF16)
        part = jnp.dot(act, wfo_ref[c0:c0 + cw, :], preferred_element_type=F32)
        acc = part if acc is None else acc + part
    o_ref[...] = x1 + mods_ref[0, 5:6, :] * acc


def _ffn(x2d, att, mlp, mods3, g2, w_out, w_ffn_in, w_ffn_out, *, mods_row_fn):
    n_tok = x2d.shape[0]
    tm = TM_FFN
    return pl.pallas_call(
        _ffn_kernel,
        grid=(n_tok // tm,),
        in_specs=[
            pl.BlockSpec((tm, D_MODEL), lambda i: (i, 0)),
            pl.BlockSpec((tm, ATT_WIDTH), lambda i: (i, 0)),
            pl.BlockSpec((tm, MLP_WIDTH), lambda i: (i, 0)),
            pl.BlockSpec((1, 6, D_MODEL), lambda i: (mods_row_fn(i), 0, 0)),
            _const_spec((1, D_MODEL)),
            _const_spec((D_MODEL, D_MODEL)),
            _const_spec((D_MODEL, 2 * D_FF)),
            _const_spec((D_FF, D_MODEL)),
        ],
        out_specs=pl.BlockSpec((tm, D_MODEL), lambda i: (i, 0)),
        out_shape=jax.ShapeDtypeStruct((n_tok, D_MODEL), F32),
        compiler_params=_params(1),
        name="ffn",
    )(x2d, att, mlp, mods3, g2, w_out, w_ffn_in, w_ffn_out)


def _rope_tables(n):
    pos = jnp.arange(n)
    row = (pos // GRID_W).astype(F32)
    col = (pos % GRID_W).astype(F32)
    inv = ROPE_THETA ** (-jnp.arange(0, ROPE_AXIS_DIM, 2, dtype=F32) / ROPE_AXIS_DIM)
    ang_r = row[:, None] * inv[None, :]
    ang_c = col[:, None] * inv[None, :]
    cos64 = jnp.concatenate([jnp.cos(ang_r)] * 2 + [jnp.cos(ang_c)] * 2, axis=1)
    sin64 = jnp.concatenate([-jnp.sin(ang_r), jnp.sin(ang_r), -jnp.sin(ang_c), jnp.sin(ang_c)], axis=1)
    return jnp.tile(cos64, (1, 2)), jnp.tile(sin64, (1, 2))


def kernel(x_prompt, x_sample, cache_k_ctx, cache_v_ctx, c, c_ctx, norm1_g, norm2_g, w_ada, b_ada, w_in, q_norm_g, k_norm_g, lambda_q1, lambda_k1, lambda_q2, lambda_k2, att_out_g, sgu_norm_g, sgu_w, sgu_b, mlp_out_g, w_out, w_ffn_in, w_ffn_out):
    n_ctx, ctx_len, _ = x_prompt.shape
    n_dec, dec_len, _ = x_sample.shape
    depth = norm1_g.shape[0]

    cos_t, sin_t = _rope_tables(dec_len)
    q_scale = LOG2E / math.sqrt(HEAD_DIM)
    rope_tabs = (cos_t * q_scale, sin_t * q_scale, cos_t, sin_t)
    group = jnp.arange(ATT_WIDTH) // HEAD_DIM
    pool = jnp.where(group[:, None] == group[None, :], 1.0 / HEAD_DIM, 0.0).astype(BF16)

    cond = jnp.concatenate(
        [c, c_ctx[None, :], jnp.zeros((MODS_ROWS - n_dec - 1, D_MODEL), F32)], axis=0)

    xp = x_prompt.reshape(n_ctx * ctx_len, D_MODEL)
    xs = x_sample.reshape(n_dec * dec_len, D_MODEL)
    k_states, v_states = [], []
    ctx_row = lambda i: CTX_ROW
    dec_row_proj = lambda i: i // (dec_len // TM_PROJ)
    dec_row_ffn = lambda i: i // (dec_len // TM_FFN)

    for l in range(depth):
        lambda_init = 0.8 - 0.6 * math.exp(-0.3 * l)
        mods3 = _mods(cond, w_ada[l], b_ada[l][None, :]).reshape(MODS_ROWS, 6, D_MODEL)
        g1 = norm1_g[l][None, :]
        g2 = norm2_g[l][None, :]
        w_in_b = w_in[l].astype(BF16)
        w_out_b = w_out[l].astype(BF16)
        w_fi_b = w_ffn_in[l].astype(BF16)
        w_fo_b = w_ffn_out[l].astype(BF16)
        qg = jnp.tile(q_norm_g[l], ATT_WIDTH // HEAD_DIM)[None, :]
        kg = jnp.tile(k_norm_g[l], ATT_WIDTH // HEAD_DIM)[None, :]
        sgun = sgu_norm_g[l][None, :]
        sguw = sgu_w[l].astype(BF16)
        sgub = jnp.broadcast_to(sgu_b[l][:, :, None], (N_GROUPS, CHUNK, GROUP_DIM))
        mlpg = mlp_out_g[l][None, :]
        att_g = att_out_g[l][None, :]
        lams = (lambda_q1[l][None, :], lambda_k1[l][None, :],
                lambda_q2[l][None, :], lambda_k2[l][None, :])

        q, k, v, mlp, k_c, v_c = _proj(
            xp, mods3, g1, w_in_b, pool, qg, kg, None, sgun, sguw, sgub, mlpg,
            seq_len=ctx_len, mods_row_fn=ctx_row)
        att = _attn(lams, q, k, v, None, att_g, n_batch=n_ctx, seq_len=ctx_len,
                    tq=ctx_len, lambda_init=lambda_init)
        xp = _ffn(xp, att, mlp, mods3, g2, w_out_b, w_fi_b, w_fo_b, mods_row_fn=ctx_row)
        k_states.append(k_c)
        v_states.append(v_c)

        kc = cache_k_ctx[:, l]
        kc = jnp.swapaxes(kc, 2, 3).reshape(n_dec, N_HEADS, -1, 2 * HEAD_DIM).astype(BF16)
        vc = cache_v_ctx[:, l].astype(BF16)
        q, k, v, mlp = _proj(
            xs, mods3, g1, w_in_b, pool, qg, kg, rope_tabs, sgun, sguw, sgub, mlpg,
            seq_len=dec_len, mods_row_fn=dec_row_proj)
        att = _attn(lams, q, k, v, (kc, vc), att_g, n_batch=n_dec, seq_len=dec_len,
                    tq=TQ, lambda_init=lambda_init)
        xs = _ffn(xs, att, mlp, mods3, g2, w_out_b, w_fi_b, w_fo_b, mods_row_fn=dec_row_ffn)

    state_k = jnp.concatenate(k_states, axis=1)
    state_v = jnp.concatenate(v_states, axis=1)
    return (xp.reshape(n_ctx, ctx_len, D_MODEL), xs.reshape(n_dec, dec_len, D_MODEL),
            state_k, state_v)
```

```python
import functools
import math

import jax
import jax.numpy as jnp
from jax import lax
from jax.experimental import pallas as pl
from jax.experimental.pallas import tpu as pltpu

D_MODEL = 1024
ATT_WIDTH = 512
N_HEADS = 4
HEAD_DIM = 64
VAL_DIM = 128
MLP_WIDTH = 512
N_GROUPS = 4
GROUP_DIM = 128
CHUNK = 128
D_FF = 2816
IN_WIDTH = 2560
GRID_W = 64
ROPE_THETA = 10000.0
ROPE_AXIS_DIM = 32
EPS = 1e-6
LOG2E = 1.4426950408889634
LANES = 128

F32 = jnp.float32
BF16 = jnp.bfloat16

VMEM_LIMIT_BYTES = 56 * 1024 * 1024
MODS_ROWS = 16
CTX_ROW = 8

TM_PROJ = 512
TM_FFN = 512
TQ = 256
FF_CHUNKS = ((0, 1024), (1024, 1024), (2048, 768))


def _const_spec(shape):
    zeros = (0,) * len(shape)
    return pl.BlockSpec(shape, lambda *_: zeros, pipeline_mode=pl.Buffered(1))


def _params(n_grid, flags=None):
    return pltpu.CompilerParams(
        dimension_semantics=("arbitrary",) * n_grid,
        vmem_limit_bytes=VMEM_LIMIT_BYTES,
        flags=flags,
    )


def _rms_scale(x):
    return lax.rsqrt(jnp.mean(x * x, axis=-1, keepdims=True) + EPS)


def _mods_kernel(cond_ref, w_ref, b_ref, o_ref):
    cnd = cond_ref[...]
    act = (cnd * jax.nn.sigmoid(cnd)).astype(BF16)
    o_ref[...] = jnp.dot(act, w_ref[...].astype(BF16), preferred_element_type=F32) + b_ref[...]


def _mods(cond, w_ada, b_ada):
    tn = 1536
    n_out = w_ada.shape[1]
    return pl.pallas_call(
        _mods_kernel,
        grid=(n_out // tn,),
        in_specs=[
            _const_spec((MODS_ROWS, D_MODEL)),
            pl.BlockSpec((D_MODEL, tn), lambda j: (0, j)),
            pl.BlockSpec((1, tn), lambda j: (0, j)),
        ],
        out_specs=pl.BlockSpec((MODS_ROWS, tn), lambda j: (0, j)),
        out_shape=jax.ShapeDtypeStruct((MODS_ROWS, n_out), F32),
        compiler_params=_params(1),
        name="mods",
    )(cond, w_ada, b_ada)


def _proj_kernel(*refs, rope, seq_len, tm):
    it = iter(refs)
    x_ref, mods_ref, g1_ref, w_in_ref, pool_ref, qg_ref, kg_ref = (next(it) for _ in range(7))
    if rope:
        wvt_ref, cq_ref, sq_ref, ck_ref, sk_ref = (next(it) for _ in range(5))
    sgun_ref, sguw_ref, sgub_ref, mlpg_ref = (next(it) for _ in range(4))
    q_ref, k_ref, v_ref, mlp_ref = (next(it) for _ in range(4))
    if not rope:
        kst_ref, vst_ref = (next(it) for _ in range(2))
    gate_ref = next(it)

    x = x_ref[...]
    xn = x * _rms_scale(x) * g1_ref[...]
    xm = xn * (1.0 + mods_ref[0, 1:2, :]) + mods_ref[0, 0:1, :]
    xb = xm.astype(BF16)

    def section(lo, hi):
        return jnp.dot(xb, w_in_ref[:, lo:hi], preferred_element_type=F32)

    def head_norm(h, g_ref):
        sq = (h * h).astype(BF16)
        msq = jnp.dot(sq, pool_ref[...], preferred_element_type=F32)
        return h * lax.rsqrt(msq + EPS) * g_ref[...]

    if rope:
        lane = lax.broadcasted_iota(jnp.int32, (tm, LANES), 1)
        first_half = (lane % 32) < 16

    def emit_rotary(hn, c_ref, s_ref, out_ref):
        for c in range(ATT_WIDTH // LANES):
            hc = hn[:, c * LANES:(c + 1) * LANES]
            swapped = jnp.where(first_half,
                                pltpu.roll(hc, LANES - 16, 1),
                                pltpu.roll(hc, 16, 1))
            out_ref[:, c * LANES:(c + 1) * LANES] = (
                hc * c_ref[...] + swapped * s_ref[...]).astype(BF16)

    qn = head_norm(section(0, ATT_WIDTH), qg_ref)
    if rope:
        emit_rotary(qn, cq_ref, sq_ref, q_ref)
    else:
        q_ref[...] = (qn * (LOG2E / math.sqrt(HEAD_DIM))).astype(BF16)

    kn = head_norm(section(ATT_WIDTH, 2 * ATT_WIDTH), kg_ref)
    if rope:
        emit_rotary(kn, ck_ref, sk_ref, k_ref)
    else:
        k_ref[...] = kn.astype(BF16)

    if rope:
        v_ref[...] = lax.dot_general(wvt_ref[...], xb, (((1,), (1,)), ((), ())),
                                     preferred_element_type=F32).astype(BF16)
    else:
        hv = section(2 * ATT_WIDTH, 3 * ATT_WIDTH)
        v_ref[...] = hv.astype(BF16)

    if not rope:
        for s in range(tm // seq_len):
            rows = slice(s * seq_len, (s + 1) * seq_len)
            for h in range(N_HEADS):
                for i in range(2):
                    j = 2 * h + i
                    kst_ref[s, 0, h, i, :, :] = kn[rows, j * HEAD_DIM:(j + 1) * HEAD_DIM]
                vst_ref[s, 0, h, :, :] = hv[rows, h * VAL_DIM:(h + 1) * VAL_DIM]

    hu = section(3 * ATT_WIDTH, 3 * ATT_WIDTH + MLP_WIDTH)
    hg = section(3 * ATT_WIDTH + MLP_WIDTH, IN_WIDTH)
    for g in range(N_GROUPS):
        cols = slice(g * GROUP_DIM, (g + 1) * GROUP_DIM)
        gg = hg[:, cols]
        gn = (gg * _rms_scale(gg) * sgun_ref[:, cols]).astype(BF16)
        ug = hu[:, cols]
        wg = sguw_ref[g]
        bg = sgub_ref[g]
        for n in range(tm // CHUNK):
            rows = slice(n * CHUNK, (n + 1) * CHUNK)
            sp = jnp.dot(wg, gn[rows, :], preferred_element_type=F32) + bg
            gate_ref[rows, cols] = ug[rows, :] * sp
    o = gate_ref[...]
    mlp_ref[...] = (o * _rms_scale(o) * mlpg_ref[...]).astype(BF16)


def _proj(x2d, mods3, g1, w_in, pool, qg, kg, rope_tabs, sgun, sguw, sgub, mlpg,
          *, seq_len, mods_row_fn, w_vt=None):
    n_tok = x2d.shape[0]
    tm = TM_PROJ
    rope = rope_tabs is not None
    blocks_per_seq = seq_len // tm if rope else None

    in_specs = [
        pl.BlockSpec((tm, D_MODEL), lambda i: (i, 0)),
        pl.BlockSpec((1, 6, D_MODEL), lambda i: (mods_row_fn(i), 0, 0)),
        _const_spec((1, D_MODEL)),
        _const_spec((D_MODEL, IN_WIDTH)),
        _const_spec((ATT_WIDTH, ATT_WIDTH)),
        _const_spec((1, ATT_WIDTH)),
        _const_spec((1, ATT_WIDTH)),
    ]
    args = [x2d, mods3, g1, w_in, pool, qg, kg]
    if rope:
        tab_spec = pl.BlockSpec((tm, LANES), lambda i: (i % blocks_per_seq, 0))
        in_specs += [_const_spec((ATT_WIDTH, D_MODEL))] + [tab_spec] * 4
        args += [w_vt] + list(rope_tabs)
    in_specs += [
        _const_spec((1, MLP_WIDTH)),
        _const_spec((N_GROUPS, CHUNK, CHUNK)),
        _const_spec((N_GROUPS, CHUNK, GROUP_DIM)),
        _const_spec((1, MLP_WIDTH)),
    ]
    args += [sgun, sguw, sgub, mlpg]

    tok_spec = pl.BlockSpec((tm, ATT_WIDTH), lambda i: (i, 0))
    tok_shape = jax.ShapeDtypeStruct((n_tok, ATT_WIDTH), BF16)
    out_specs = [tok_spec] * 4
    out_shape = [tok_shape] * 4
    if rope:
        out_specs[2] = pl.BlockSpec((ATT_WIDTH, tm), lambda i: (0, i))
        out_shape[2] = jax.ShapeDtypeStruct((ATT_WIDTH, n_tok), BF16)
    else:
        n_seq = n_tok // seq_len
        spb = tm // seq_len
        out_specs += [
            pl.BlockSpec((spb, 1, N_HEADS, 2, seq_len, HEAD_DIM), lambda i: (i, 0, 0, 0, 0, 0)),
            pl.BlockSpec((spb, 1, N_HEADS, seq_len, VAL_DIM), lambda i: (i, 0, 0, 0, 0)),
        ]
        out_shape += [
            jax.ShapeDtypeStruct((n_seq, 1, N_HEADS, 2, seq_len, HEAD_DIM), F32),
            jax.ShapeDtypeStruct((n_seq, 1, N_HEADS, seq_len, VAL_DIM), F32),
        ]

    return pl.pallas_call(
        functools.partial(_proj_kernel, rope=rope, seq_len=seq_len, tm=tm),
        grid=(n_tok // tm,),
        in_specs=in_specs,
        out_specs=out_specs,
        out_shape=out_shape,
        scratch_shapes=[pltpu.VMEM((tm, MLP_WIDTH), F32)],
        compiler_params=_params(1),
        name="proj_rope" if rope else "proj_ctx",
    )(*args)


def _attn_kernel(*refs, lambda_init, has_cache):
    it = iter(refs)
    lq1, lk1, lq2, lk2, q_ref, k_ref, v_ref = (next(it) for _ in range(7))
    if has_cache:
        kc_ref, vc_ref = (next(it) for _ in range(2))
    ag_ref, o_ref = (next(it) for _ in range(2))

    lam = (jnp.exp(jnp.sum(lq1[...] * lk1[...], keepdims=True))
           - jnp.exp(jnp.sum(lq2[...] * lk2[...], keepdims=True))
           + lambda_init)

    q = q_ref[...]
    lane = lax.broadcasted_iota(jnp.int32, q.shape, 1)
    zero = jnp.zeros_like(q)
    q_maps = (jnp.where(lane < HEAD_DIM, q, zero), jnp.where(lane >= HEAD_DIM, q, zero))

    keys = [k_ref[...]] + ([kc_ref[0, 0]] if has_cache else [])
    vals = [v_ref[...]] + ([vc_ref[0, 0]] if has_cache else [])
    nt = (((1,), (1,)), ((), ()))

    def softmax_parts(qm):
        parts = [lax.dot_general(qm, kk, nt, preferred_element_type=F32) for kk in keys]
        m = functools.reduce(jnp.maximum, [jnp.max(p, axis=-1, keepdims=True) for p in parts])
        es = [jnp.exp2(p - m) for p in parts]
        denom = functools.reduce(jnp.add, [jnp.sum(e, axis=-1, keepdims=True) for e in es])
        return es, denom

    e1, d1 = softmax_parts(q_maps[0])
    e2, d2 = softmax_parts(q_maps[1])
    r1 = 1.0 / d1
    r2 = lam / d2
    acc = None
    for ea, eb, vv in zip(e1, e2, vals):
        a = (ea * r1 - eb * r2).astype(BF16)
        part = jnp.dot(a, vv, preferred_element_type=F32)
        acc = part if acc is None else acc + part
    o = acc * _rms_scale(acc) * (1.0 - lambda_init) * ag_ref[...]
    o_ref[...] = o.astype(BF16)


def _attn(lams, q, k, v, cache, att_g, *, n_batch, seq_len, tq, lambda_init):
    nq = seq_len // tq
    has_cache = cache is not None
    lam_spec = _const_spec((1, HEAD_DIM))
    in_specs = [lam_spec] * 4 + [
        pl.BlockSpec((tq, LANES), lambda b, h, i: (b * nq + i, h)),
        pl.BlockSpec((seq_len, LANES), lambda b, h, i: (b, h)),
        pl.BlockSpec((seq_len, LANES), lambda b, h, i: (b, h)),
    ]
    args = list(lams) + [q, k, v]
    if has_cache:
        kc, vc = cache
        past = kc.shape[2]
        cspec = pl.BlockSpec((1, 1, past, LANES), lambda b, h, i: (b, h, 0, 0))
        in_specs += [cspec, cspec]
        args += [kc, vc]
    in_specs.append(pl.BlockSpec((1, LANES), lambda b, h, i: (0, h)))
    args.append(att_g)
    return pl.pallas_call(
        functools.partial(_attn_kernel, lambda_init=lambda_init, has_cache=has_cache),
        grid=(n_batch, N_HEADS, nq),
        in_specs=in_specs,
        out_specs=pl.BlockSpec((tq, LANES), lambda b, h, i: (b * nq + i, h)),
        out_shape=jax.ShapeDtypeStruct((n_batch * seq_len, ATT_WIDTH), BF16),
        compiler_params=_params(3),
        name="attn_cache" if has_cache else "attn_ctx",
    )(*args)


ONES_ROWS = 16
TQ_UNIT = 256
KEY_BLOCK = 256


def _attn_t_kernel(lq1, lk1, lq2, lk2, q_ref, k_ref, vt_ref, kc_ref, vtc_ref, ag_ref, o_ref,
                   k_all, vt_all, m_buf, *bufs, lambda_init, n_new):
    st = (bufs[0:2], bufs[2:4])
    n_chunks = n_new // TQ_UNIT
    n_keys = k_all.shape[0]

    k_all[0:n_new, :] = k_ref[...]
    k_all[n_new:, :] = kc_ref[0, 0]
    vt_all[0:VAL_DIM, 0:n_new] = vt_ref[...]
    vt_all[0:VAL_DIM, n_new:] = vtc_ref[0, 0]
    vt_all[VAL_DIM:, :] = jnp.ones((ONES_ROWS, vt_all.shape[1]), BF16)

    lam = (jnp.exp(jnp.sum(lq1[...] * lk1[...], keepdims=True))
           - jnp.exp(jnp.sum(lq2[...] * lk2[...], keepdims=True))
           + lambda_init)
    out_gain = (1.0 - lambda_init) * ag_ref[...]

    lane = lax.broadcasted_iota(jnp.int32, (TQ_UNIT, LANES), 1)
    keep = (lane < HEAD_DIM, lane >= HEAD_DIM)
    nt = (((1,), (1,)), ((), ()))

    def rows(c):
        return pl.ds(pl.multiple_of(c * TQ_UNIT, TQ_UNIT), TQ_UNIT)

    def stage(fin, sc):
        if sc is not None:
            qc = q_ref[rows(sc[0]), :]
            qms = [jnp.where(keep[mp], qc, jnp.zeros_like(qc)) for mp in range(2)]
            mrun = [None, None]
        if fin is not None:
            ms = [m_buf[fin[1], mp] for mp in range(2)]
            accs = [None, None]
        for kb in range(n_keys // KEY_BLOCK):
            kr = slice(kb * KEY_BLOCK, (kb + 1) * KEY_BLOCK)
            if sc is not None:
                kk = k_all[kr, :]
                for mp in range(2):
                    s = lax.dot_general(kk, qms[mp], nt, preferred_element_type=F32)
                    st[sc[1]][mp][kr, :] = s
                    smax = jnp.max(s.reshape(KEY_BLOCK // 8, 8, TQ_UNIT), axis=0)
                    mrun[mp] = smax if mrun[mp] is None else jnp.maximum(mrun[mp], smax)
            if fin is not None:
                vt = vt_all[:, kr]
                for mp in range(2):
                    p = jnp.exp2(st[fin[1]][mp][kr, :] - ms[mp]).astype(BF16)
                    d = jnp.dot(vt, p, preferred_element_type=F32)
                    accs[mp] = d if accs[mp] is None else accs[mp] + d
        if sc is not None:
            for mp in range(2):
                m_buf[sc[1], mp] = jnp.max(mrun[mp], axis=0, keepdims=True)
        if fin is not None:
            o1, o2 = accs
            r1 = 1.0 / o1[VAL_DIM:VAL_DIM + 1, :]
            r2 = lam / o2[VAL_DIM:VAL_DIM + 1, :]
            ot = o1[0:VAL_DIM, :] * r1 - o2[0:VAL_DIM, :] * r2
            ot = ot * lax.rsqrt(jnp.mean(ot * ot, axis=0, keepdims=True) + EPS)
            o_ref[rows(fin[0]), :] = (ot.T * out_gain).astype(BF16)

    stage(None, (0, 0))

    def pair(i, carry):
        c = 2 * i
        stage((c, 0), (c + 1, 1))
        stage((c + 1, 1), (c + 2, 0))
        return carry

    lax.fori_loop(0, n_chunks // 2 - 1, pair, 0)
    stage((n_chunks - 2, 0), (n_chunks - 1, 1))
    stage((n_chunks - 1, 1), None)


def _attn_t(lams, q, k, vt, kc, vtc, att_g, *, n_batch, seq_len, lambda_init):
    past = kc.shape[2]
    n_keys = seq_len + past
    assert (seq_len // TQ_UNIT) % 2 == 0 and seq_len // TQ_UNIT >= 4
    lam_spec = _const_spec((1, HEAD_DIM))
    in_specs = [lam_spec] * 4 + [
        pl.BlockSpec((seq_len, LANES), lambda b, h: (b, h)),
        pl.BlockSpec((seq_len, LANES), lambda b, h: (b, h)),
        pl.BlockSpec((VAL_DIM, seq_len), lambda b, h: (h, b)),
        pl.BlockSpec((1, 1, past, LANES), lambda b, h: (b, h, 0, 0)),
        pl.BlockSpec((1, 1, VAL_DIM, past), lambda b, h: (b, h, 0, 0)),
        pl.BlockSpec((1, LANES), lambda b, h: (0, h)),
    ]
    return pl.pallas_call(
        functools.partial(_attn_t_kernel, lambda_init=lambda_init, n_new=seq_len),
        grid=(n_batch, N_HEADS),
        in_specs=in_specs,
        out_specs=pl.BlockSpec((seq_len, LANES), lambda b, h: (b, h)),
        out_shape=jax.ShapeDtypeStruct((n_batch * seq_len, ATT_WIDTH), BF16),
        scratch_shapes=([pltpu.VMEM((n_keys, LANES), BF16),
                         pltpu.VMEM((VAL_DIM + ONES_ROWS, n_keys), BF16),
                         pltpu.VMEM((2, 2, 1, TQ_UNIT), F32)]
                        + [pltpu.VMEM((n_keys, TQ_UNIT), F32)] * 4),
        compiler_params=_params(2),
        name="attn_cache",
    )(*lams, q, k, vt, kc, vtc, att_g)


def _ffn_kernel(x_ref, att_ref, mlp_ref, mods_ref, g2_ref, wo_ref, wfi_ref, wfo_ref, o_ref):
    y = (jnp.dot(att_ref[...], wo_ref[0:ATT_WIDTH, :], preferred_element_type=F32)
         + jnp.dot(mlp_ref[...], wo_ref[ATT_WIDTH:, :], preferred_element_type=F32))
    x1 = x_ref[...] + mods_ref[0, 2:3, :] * y
    xn = x1 * _rms_scale(x1) * g2_ref[...]
    xb = (xn * (1.0 + mods_ref[0, 4:5, :]) + mods_ref[0, 3:4, :]).astype(BF16)
    acc = None
    for c0, cw in FF_CHUNKS:
        gte = jnp.dot(xb, wfi_ref[:, c0:c0 + cw], preferred_element_type=F32)
        up = jnp.dot(xb, wfi_ref[:, D_FF + c0:D_FF + c0 + cw], preferred_element_type=F32)
        act = (gte * jax.nn.sigmoid(gte) * up).astype(BF16)
        part = jnp.dot(act, wfo_ref[c0:c0 + cw, :], preferred_element_type=F32)
        acc = part if acc is None else acc + part
    o_ref[...] = x1 + mods_ref[0, 5:6, :] * acc


def _ffn(x2d, att, mlp, mods3, g2, w_out, w_ffn_in, w_ffn_out, *, mods_row_fn):
    n_tok = x2d.shape[0]
    tm = TM_FFN
    return pl.pallas_call(
        _ffn_kernel,
        grid=(n_tok // tm,),
        in_specs=[
            pl.BlockSpec((tm, D_MODEL), lambda i: (i, 0)),
            pl.BlockSpec((tm, ATT_WIDTH), lambda i: (i, 0)),
            pl.BlockSpec((tm, MLP_WIDTH), lambda i: (i, 0)),
            pl.BlockSpec((1, 6, D_MODEL), lambda i: (mods_row_fn(i), 0, 0)),
            _const_spec((1, D_MODEL)),
            _const_spec((D_MODEL, D_MODEL)),
            _const_spec((D_MODEL, 2 * D_FF)),
            _const_spec((D_FF, D_MODEL)),
        ],
        out_specs=pl.BlockSpec((tm, D_MODEL), lambda i: (i, 0)),
        out_shape=jax.ShapeDtypeStruct((n_tok, D_MODEL), F32),
        compiler_params=_params(1),
        name="ffn",
    )(x2d, att, mlp, mods3, g2, w_out, w_ffn_in, w_ffn_out)


def _rope_tables(n):
    pos = jnp.arange(n)
    row = (pos // GRID_W).astype(F32)
    col = (pos % GRID_W).astype(F32)
    inv = ROPE_THETA ** (-jnp.arange(0, ROPE_AXIS_DIM, 2, dtype=F32) / ROPE_AXIS_DIM)
    ang_r = row[:, None] * inv[None, :]
    ang_c = col[:, None] * inv[None, :]
    cos64 = jnp.concatenate([jnp.cos(ang_r)] * 2 + [jnp.cos(ang_c)] * 2, axis=1)
    sin64 = jnp.concatenate([-jnp.sin(ang_r), jnp.sin(ang_r), -jnp.sin(ang_c), jnp.sin(ang_c)], axis=1)
    return jnp.tile(cos64, (1, 2)), jnp.tile(sin64, (1, 2))


def kernel(x_prompt, x_sample, cache_k_ctx, cache_v_ctx, c, c_ctx, norm1_g, norm2_g, w_ada, b_ada, w_in, q_norm_g, k_norm_g, lambda_q1, lambda_k1, lambda_q2, lambda_k2, att_out_g, sgu_norm_g, sgu_w, sgu_b, mlp_out_g, w_out, w_ffn_in, w_ffn_out):
    n_ctx, ctx_len, _ = x_prompt.shape
    n_dec, dec_len, _ = x_sample.shape
    depth = norm1_g.shape[0]

    cos_t, sin_t = _rope_tables(dec_len)
    q_scale = LOG2E / math.sqrt(HEAD_DIM)
    rope_tabs = (cos_t * q_scale, sin_t * q_scale, cos_t, sin_t)
    group = jnp.arange(ATT_WIDTH) // HEAD_DIM
    pool = jnp.where(group[:, None] == group[None, :], 1.0 / HEAD_DIM, 0.0).astype(BF16)

    cond = jnp.concatenate(
        [c, c_ctx[None, :], jnp.zeros((MODS_ROWS - n_dec - 1, D_MODEL), F32)], axis=0)

    xp = x_prompt.reshape(n_ctx * ctx_len, D_MODEL)
    xs = x_sample.reshape(n_dec * dec_len, D_MODEL)
    k_states, v_states = [], []
    ctx_row = lambda i: CTX_ROW
    dec_row_proj = lambda i: i // (dec_len // TM_PROJ)
    dec_row_ffn = lambda i: i // (dec_len // TM_FFN)

    for l in range(depth):
        lambda_init = 0.8 - 0.6 * math.exp(-0.3 * l)
        mods3 = _mods(cond, w_ada[l], b_ada[l][None, :]).reshape(MODS_ROWS, 6, D_MODEL)
        g1 = norm1_g[l][None, :]
        g2 = norm2_g[l][None, :]
        w_in_b = w_in[l].astype(BF16)
        w_out_b = w_out[l].astype(BF16)
        w_fi_b = w_ffn_in[l].astype(BF16)
        w_fo_b = w_ffn_out[l].astype(BF16)
        qg = jnp.tile(q_norm_g[l], ATT_WIDTH // HEAD_DIM)[None, :]
        kg = jnp.tile(k_norm_g[l], ATT_WIDTH // HEAD_DIM)[None, :]
        sgun = sgu_norm_g[l][None, :]
        sguw = sgu_w[l].astype(BF16)
        sgub = jnp.broadcast_to(sgu_b[l][:, :, None], (N_GROUPS, CHUNK, GROUP_DIM))
        mlpg = mlp_out_g[l][None, :]
        att_g = att_out_g[l][None, :]
        lams = (lambda_q1[l][None, :], lambda_k1[l][None, :],
                lambda_q2[l][None, :], lambda_k2[l][None, :])

        q, k, v, mlp, k_c, v_c = _proj(
            xp, mods3, g1, w_in_b, pool, qg, kg, None, sgun, sguw, sgub, mlpg,
            seq_len=ctx_len, mods_row_fn=ctx_row)
        att = _attn(lams, q, k, v, None, att_g, n_batch=n_ctx, seq_len=ctx_len,
                    tq=ctx_len, lambda_init=lambda_init)
        xp = _ffn(xp, att, mlp, mods3, g2, w_out_b, w_fi_b, w_fo_b, mods_row_fn=ctx_row)
        k_states.append(k_c)
        v_states.append(v_c)

        kc = cache_k_ctx[:, l]
        kc = jnp.swapaxes(kc, 2, 3).reshape(n_dec, N_HEADS, -1, 2 * HEAD_DIM).astype(BF16)
        vtc = jnp.swapaxes(cache_v_ctx[:, l], 2, 3).astype(BF16)
        w_vt = w_in[l][:, 2 * ATT_WIDTH:3 * ATT_WIDTH].T.astype(BF16)
        q, k, vt, mlp = _proj(
            xs, mods3, g1, w_in_b, pool, qg, kg, rope_tabs, sgun, sguw, sgub, mlpg,
            seq_len=dec_len, mods_row_fn=dec_row_proj, w_vt=w_vt)
        att = _attn_t(lams, q, k, vt, kc, vtc, att_g, n_batch=n_dec, seq_len=dec_len,
                      lambda_init=lambda_init)
        xs = _ffn(xs, att, mlp, mods3, g2, w_out_b, w_fi_b, w_fo_b, mods_row_fn=dec_row_ffn)

    state_k = jnp.concatenate(k_states, axis=1)
    state_v = jnp.concatenate(v_states, axis=1)
    return (xp.reshape(n_ctx, ctx_len, D_MODEL), xs.reshape(n_dec, dec_len, D_MODEL),
            state_k, state_v)
```

```python
import functools
import math

import jax
import jax.numpy as jnp
from jax import lax
from jax.experimental import pallas as pl
from jax.experimental.pallas import tpu as pltpu

D_MODEL = 1024
ATT_WIDTH = 512
N_HEADS = 4
HEAD_DIM = 64
VAL_DIM = 128
MLP_WIDTH = 512
N_GROUPS = 4
GROUP_DIM = 128
CHUNK = 128
D_FF = 2816
IN_WIDTH = 2560
GRID_W = 64
ROPE_THETA = 10000.0
ROPE_AXIS_DIM = 32
EPS = 1e-6
LOG2E = 1.4426950408889634
LANES = 128

F32 = jnp.float32
BF16 = jnp.bfloat16

VMEM_LIMIT_BYTES = 56 * 1024 * 1024
MODS_ROWS = 16
CTX_ROW = 8

TM_PROJ = 512
TM_FFN = 512
TM_ATTN_CTX = 512
CTX_AHEAD = 3
FF_CHUNKS = ((0, 1024), (1024, 1024), (2048, 768))

ONES_ROWS = 16
TQ_UNIT = 256
KEY_BLOCK = 256

NT_DIMS = (((1,), (1,)), ((), ()))


def _const_spec(shape):
    zeros = (0,) * len(shape)
    return pl.BlockSpec(shape, lambda *_: zeros, pipeline_mode=pl.Buffered(1))


def _params(n_grid):
    return pltpu.CompilerParams(
        dimension_semantics=("arbitrary",) * n_grid,
        vmem_limit_bytes=VMEM_LIMIT_BYTES,
    )


def _rms_scale(x):
    return lax.rsqrt(jnp.mean(x * x, axis=-1, keepdims=True) + EPS)


def _mods_kernel(cond_ref, w_ref, b_ref, o_ref):
    cnd = cond_ref[...]
    act = (cnd * jax.nn.sigmoid(cnd)).astype(BF16)
    o_ref[...] = jnp.dot(act, w_ref[...].astype(BF16), preferred_element_type=F32) + b_ref[...]


def _mods(cond, w_ada, b_ada):
    tn = 1536
    n_out = w_ada.shape[1]
    return pl.pallas_call(
        _mods_kernel,
        grid=(n_out // tn,),
        in_specs=[
            _const_spec((MODS_ROWS, D_MODEL)),
            pl.BlockSpec((D_MODEL, tn), lambda j: (0, j)),
            pl.BlockSpec((1, tn), lambda j: (0, j)),
        ],
        out_specs=pl.BlockSpec((MODS_ROWS, tn), lambda j: (0, j)),
        out_shape=jax.ShapeDtypeStruct((MODS_ROWS, n_out), F32),
        compiler_params=_params(1),
        name="mods",
    )(cond, w_ada, b_ada)


def _proj_kernel(*refs, rope, seq_len, tm):
    it = iter(refs)
    x_ref, mods_ref, g1_ref, w_in_ref, pool_ref, qg_ref, kg_ref = (next(it) for _ in range(7))
    if rope:
        wvt_ref, cq_ref, sq_ref, ck_ref, sk_ref = (next(it) for _ in range(5))
    sgun_ref, sguw_ref, sgub_ref, mlpg_ref = (next(it) for _ in range(4))
    q_ref, k_ref, vt_ref, mlp_ref = (next(it) for _ in range(4))
    if not rope:
        kst_ref, vst_ref = (next(it) for _ in range(2))
    gate_ref = next(it)

    x = x_ref[...]
    xn = x * _rms_scale(x) * g1_ref[...]
    xm = xn * (1.0 + mods_ref[0, 1:2, :]) + mods_ref[0, 0:1, :]
    xb = xm.astype(BF16)

    def section(lo, hi):
        return jnp.dot(xb, w_in_ref[:, lo:hi], preferred_element_type=F32)

    def head_norm(h, g_ref):
        sq = (h * h).astype(BF16)
        msq = jnp.dot(sq, pool_ref[...], preferred_element_type=F32)
        return h * lax.rsqrt(msq + EPS) * g_ref[...]

    if rope:
        lane = lax.broadcasted_iota(jnp.int32, (tm, LANES), 1)
        first_half = (lane % 32) < 16

    def emit_rotary(hn, c_ref, s_ref, out_ref):
        for c in range(ATT_WIDTH // LANES):
            hc = hn[:, c * LANES:(c + 1) * LANES]
            swapped = jnp.where(first_half,
                                pltpu.roll(hc, LANES - 16, 1),
                                pltpu.roll(hc, 16, 1))
            out_ref[:, c * LANES:(c + 1) * LANES] = (
                hc * c_ref[...] + swapped * s_ref[...]).astype(BF16)

    qn = head_norm(section(0, ATT_WIDTH), qg_ref)
    if rope:
        emit_rotary(qn, cq_ref, sq_ref, q_ref)
    else:
        q_ref[...] = (qn * (LOG2E / math.sqrt(HEAD_DIM))).astype(BF16)

    kn = head_norm(section(ATT_WIDTH, 2 * ATT_WIDTH), kg_ref)
    if rope:
        emit_rotary(kn, ck_ref, sk_ref, k_ref)
    else:
        k_ref[...] = kn.astype(BF16)

    if rope:
        vt_ref[...] = lax.dot_general(wvt_ref[...], xb, NT_DIMS,
                                      preferred_element_type=F32).astype(BF16)
    else:
        hv = section(2 * ATT_WIDTH, 3 * ATT_WIDTH)
        vt_ref[...] = hv.T.astype(BF16)
        for s in range(tm // seq_len):
            rows = slice(s * seq_len, (s + 1) * seq_len)
            for h in range(N_HEADS):
                for i in range(2):
                    j = 2 * h + i
                    kst_ref[s, 0, h, i, :, :] = kn[rows, j * HEAD_DIM:(j + 1) * HEAD_DIM]
                vst_ref[s, 0, h, :, :] = hv[rows, h * VAL_DIM:(h + 1) * VAL_DIM]

    hu = section(3 * ATT_WIDTH, 3 * ATT_WIDTH + MLP_WIDTH)
    hg = section(3 * ATT_WIDTH + MLP_WIDTH, IN_WIDTH)
    for g in range(N_GROUPS):
        cols = slice(g * GROUP_DIM, (g + 1) * GROUP_DIM)
        gg = hg[:, cols]
        gn = (gg * _rms_scale(gg) * sgun_ref[:, cols]).astype(BF16)
        ug = hu[:, cols]
        wg = sguw_ref[g]
        bg = sgub_ref[g]
        for n in range(tm // CHUNK):
            rows = slice(n * CHUNK, (n + 1) * CHUNK)
            sp = jnp.dot(wg, gn[rows, :], preferred_element_type=F32) + bg
            gate_ref[rows, cols] = ug[rows, :] * sp
    o = gate_ref[...]
    mlp_ref[...] = (o * _rms_scale(o) * mlpg_ref[...]).astype(BF16)


def _proj(x2d, mods3, g1, w_in, pool, qg, kg, rope_tabs, sgun, sguw, sgub, mlpg,
          *, seq_len, mods_row_fn, w_vt=None):
    n_tok = x2d.shape[0]
    tm = TM_PROJ
    rope = rope_tabs is not None
    blocks_per_seq = seq_len // tm if rope else None

    in_specs = [
        pl.BlockSpec((tm, D_MODEL), lambda i: (i, 0)),
        pl.BlockSpec((1, 6, D_MODEL), lambda i: (mods_row_fn(i), 0, 0)),
        _const_spec((1, D_MODEL)),
        _const_spec((D_MODEL, IN_WIDTH)),
        _const_spec((ATT_WIDTH, ATT_WIDTH)),
        _const_spec((1, ATT_WIDTH)),
        _const_spec((1, ATT_WIDTH)),
    ]
    args = [x2d, mods3, g1, w_in, pool, qg, kg]
    if rope:
        tab_spec = pl.BlockSpec((tm, LANES), lambda i: (i % blocks_per_seq, 0))
        in_specs += [_const_spec((ATT_WIDTH, D_MODEL))] + [tab_spec] * 4
        args += [w_vt] + list(rope_tabs)
    in_specs += [
        _const_spec((1, MLP_WIDTH)),
        _const_spec((N_GROUPS, CHUNK, CHUNK)),
        _const_spec((N_GROUPS, CHUNK, GROUP_DIM)),
        _const_spec((1, MLP_WIDTH)),
    ]
    args += [sgun, sguw, sgub, mlpg]

    tok_spec = pl.BlockSpec((tm, ATT_WIDTH), lambda i: (i, 0))
    tok_shape = jax.ShapeDtypeStruct((n_tok, ATT_WIDTH), BF16)
    out_specs = [tok_spec] * 4
    out_shape = [tok_shape] * 4
    out_specs[2] = pl.BlockSpec((ATT_WIDTH, tm), lambda i: (0, i))
    out_shape[2] = jax.ShapeDtypeStruct((ATT_WIDTH, n_tok), BF16)
    if not rope:
        n_seq = n_tok // seq_len
        spb = tm // seq_len
        out_specs += [
            pl.BlockSpec((spb, 1, N_HEADS, 2, seq_len, HEAD_DIM), lambda i: (i, 0, 0, 0, 0, 0)),
            pl.BlockSpec((spb, 1, N_HEADS, seq_len, VAL_DIM), lambda i: (i, 0, 0, 0, 0)),
        ]
        out_shape += [
            jax.ShapeDtypeStruct((n_seq, 1, N_HEADS, 2, seq_len, HEAD_DIM), F32),
            jax.ShapeDtypeStruct((n_seq, 1, N_HEADS, seq_len, VAL_DIM), F32),
        ]

    return pl.pallas_call(
        functools.partial(_proj_kernel, rope=rope, seq_len=seq_len, tm=tm),
        grid=(n_tok // tm,),
        in_specs=in_specs,
        out_specs=out_specs,
        out_shape=out_shape,
        scratch_shapes=[pltpu.VMEM((tm, MLP_WIDTH), F32)],
        compiler_params=_params(1),
        name="proj_rope" if rope else "proj_ctx",
    )(*args)


def _lambda_full(lq1, lk1, lq2, lk2, lambda_init):
    return (jnp.exp(jnp.sum(lq1[...] * lk1[...], keepdims=True))
            - jnp.exp(jnp.sum(lq2[...] * lk2[...], keepdims=True))
            + lambda_init)


def _map_queries(qc):
    lane = lax.broadcasted_iota(jnp.int32, qc.shape, 1)
    zero = jnp.zeros_like(qc)
    return (jnp.where(lane < HEAD_DIM, qc, zero), jnp.where(lane >= HEAD_DIM, qc, zero))


def _combine_maps(o1, d1, o2, d2, lam, out_gain):
    ot = o1 * (1.0 / d1) - o2 * (lam / d2)
    ot = ot * lax.rsqrt(jnp.mean(ot * ot, axis=0, keepdims=True) + EPS)
    return (ot.T * out_gain).astype(BF16)


def _attn_ctx_kernel(lq1, lk1, lq2, lk2, q_ref, k_ref, vt_ref, ag_ref, o_ref, st_buf,
                     *, lambda_init, seq_len):
    lam = _lambda_full(lq1, lk1, lq2, lk2, lambda_init)
    units = [(slice(s * seq_len, (s + 1) * seq_len), slice(h * LANES, (h + 1) * LANES))
             for s in range(q_ref.shape[0] // seq_len) for h in range(N_HEADS)]

    def scores(u):
        rows, cols = units[u]
        kk = k_ref[rows, cols]
        maxes = []
        for mp, qm in enumerate(_map_queries(q_ref[rows, cols])):
            st = lax.dot_general(kk, qm, NT_DIMS, preferred_element_type=F32)
            st_buf[u % CTX_AHEAD, mp] = st
            maxes.append(jnp.max(st, axis=0, keepdims=True))
        return maxes

    def finish(u, maxes):
        rows, cols = units[u]
        vt = vt_ref[cols, rows]
        outs, dens = [], []
        for mp in range(2):
            e = jnp.exp2(st_buf[u % CTX_AHEAD, mp] - maxes[mp])
            dens.append(jnp.sum(e, axis=0, keepdims=True))
            outs.append(jnp.dot(vt, e.astype(BF16), preferred_element_type=F32))
        out_gain = (1.0 - lambda_init) * ag_ref[:, cols]
        o_ref[rows, cols] = _combine_maps(outs[0], dens[0], outs[1], dens[1], lam, out_gain)

    pending = [scores(u) for u in range(CTX_AHEAD - 1)]
    for u in range(len(units)):
        if u + CTX_AHEAD - 1 < len(units):
            pending.append(scores(u + CTX_AHEAD - 1))
        finish(u, pending.pop(0))


def _attn_ctx(lams, q, k, vt, att_g, *, seq_len, lambda_init):
    n_tok = q.shape[0]
    tm = TM_ATTN_CTX
    return pl.pallas_call(
        functools.partial(_attn_ctx_kernel, lambda_init=lambda_init, seq_len=seq_len),
        grid=(n_tok // tm,),
        in_specs=[_const_spec((1, HEAD_DIM))] * 4 + [
            pl.BlockSpec((tm, ATT_WIDTH), lambda i: (i, 0)),
            pl.BlockSpec((tm, ATT_WIDTH), lambda i: (i, 0)),
            pl.BlockSpec((ATT_WIDTH, tm), lambda i: (0, i)),
            _const_spec((1, ATT_WIDTH)),
        ],
        out_specs=pl.BlockSpec((tm, ATT_WIDTH), lambda i: (i, 0)),
        out_shape=jax.ShapeDtypeStruct((n_tok, ATT_WIDTH), BF16),
        scratch_shapes=[pltpu.VMEM((CTX_AHEAD, 2, seq_len, seq_len), F32)],
        compiler_params=_params(1),
        name="attn_ctx",
    )(*lams, q, k, vt, att_g)


def _attn_cache_kernel(lq1, lk1, lq2, lk2, q_ref, k_ref, vt_ref, kc_ref, vtc_ref, ag_ref, o_ref,
                       k_all, vt_all, m_buf, *bufs, lambda_init, n_new):
    st = (bufs[0:2], bufs[2:4])
    n_chunks = n_new // TQ_UNIT
    n_keys = k_all.shape[0]

    k_all[0:n_new, :] = k_ref[...]
    k_all[n_new:, :] = kc_ref[0, 0]
    vt_all[0:VAL_DIM, 0:n_new] = vt_ref[...]
    vt_all[0:VAL_DIM, n_new:] = vtc_ref[0, 0]
    vt_all[VAL_DIM:, :] = jnp.ones((ONES_ROWS, n_keys), BF16)

    lam = _lambda_full(lq1, lk1, lq2, lk2, lambda_init)
    out_gain = (1.0 - lambda_init) * ag_ref[...]

    def rows(c):
        return pl.ds(pl.multiple_of(c * TQ_UNIT, TQ_UNIT), TQ_UNIT)

    def stage(fin, sc):
        if sc is not None:
            qms = _map_queries(q_ref[rows(sc[0]), :])
            mrun = [None, None]
        if fin is not None:
            ms = [m_buf[fin[1], mp] for mp in range(2)]
            accs = [None, None]
        for kb in range(n_keys // KEY_BLOCK):
            kr = slice(kb * KEY_BLOCK, (kb + 1) * KEY_BLOCK)
            if sc is not None:
                kk = k_all[kr, :]
                for mp in range(2):
                    s = lax.dot_general(kk, qms[mp], NT_DIMS, preferred_element_type=F32)
                    st[sc[1]][mp][kr, :] = s
                    smax = jnp.max(s.reshape(KEY_BLOCK // 8, 8, TQ_UNIT), axis=0)
                    mrun[mp] = smax if mrun[mp] is None else jnp.maximum(mrun[mp], smax)
            if fin is not None:
                vt = vt_all[:, kr]
                for mp in range(2):
                    p = jnp.exp2(st[fin[1]][mp][kr, :] - ms[mp]).astype(BF16)
                    d = jnp.dot(vt, p, preferred_element_type=F32)
                    accs[mp] = d if accs[mp] is None else accs[mp] + d
        if sc is not None:
            for mp in range(2):
                m_buf[sc[1], mp] = jnp.max(mrun[mp], axis=0, keepdims=True)
        if fin is not None:
            o1, o2 = accs
            o_ref[rows(fin[0]), :] = _combine_maps(
                o1[0:VAL_DIM, :], o1[VAL_DIM:VAL_DIM + 1, :],
                o2[0:VAL_DIM, :], o2[VAL_DIM:VAL_DIM + 1, :], lam, out_gain)

    stage(None, (0, 0))

    def pair(i, carry):
        c = 2 * i
        stage((c, 0), (c + 1, 1))
        stage((c + 1, 1), (c + 2, 0))
        return carry

    lax.fori_loop(0, n_chunks // 2 - 1, pair, 0)
    stage((n_chunks - 2, 0), (n_chunks - 1, 1))
    stage((n_chunks - 1, 1), None)


def _attn_cache(lams, q, k, vt, kc, vtc, att_g, *, n_batch, seq_len, lambda_init):
    past = kc.shape[2]
    n_keys = seq_len + past
    assert (seq_len // TQ_UNIT) % 2 == 0 and seq_len // TQ_UNIT >= 4
    assert n_keys % KEY_BLOCK == 0
    in_specs = [_const_spec((1, HEAD_DIM))] * 4 + [
        pl.BlockSpec((seq_len, LANES), lambda b, h: (b, h)),
        pl.BlockSpec((seq_len, LANES), lambda b, h: (b, h)),
        pl.BlockSpec((VAL_DIM, seq_len), lambda b, h: (h, b)),
        pl.BlockSpec((1, 1, past, LANES), lambda b, h: (b, h, 0, 0)),
        pl.BlockSpec((1, 1, VAL_DIM, past), lambda b, h: (b, h, 0, 0)),
        pl.BlockSpec((1, LANES), lambda b, h: (0, h)),
    ]
    return pl.pallas_call(
        functools.partial(_attn_cache_kernel, lambda_init=lambda_init, n_new=seq_len),
        grid=(n_batch, N_HEADS),
        in_specs=in_specs,
        out_specs=pl.BlockSpec((seq_len, LANES), lambda b, h: (b, h)),
        out_shape=jax.ShapeDtypeStruct((n_batch * seq_len, ATT_WIDTH), BF16),
        scratch_shapes=([pltpu.VMEM((n_keys, LANES), BF16),
                         pltpu.VMEM((VAL_DIM + ONES_ROWS, n_keys), BF16),
                         pltpu.VMEM((2, 2, 1, TQ_UNIT), F32)]
                        + [pltpu.VMEM((n_keys, TQ_UNIT), F32)] * 4),
        compiler_params=_params(2),
        name="attn_cache",
    )(*lams, q, k, vt, kc, vtc, att_g)


def _ffn_kernel(x_ref, att_ref, mlp_ref, mods_ref, g2_ref, wo_ref, wfi_ref, wfo_ref, o_ref):
    y = (jnp.dot(att_ref[...], wo_ref[0:ATT_WIDTH, :], preferred_element_type=F32)
         + jnp.dot(mlp_ref[...], wo_ref[ATT_WIDTH:, :], preferred_element_type=F32))
    x1 = x_ref[...] + mods_ref[0, 2:3, :] * y
    xn = x1 * _rms_scale(x1) * g2_ref[...]
    xb = (xn * (1.0 + mods_ref[0, 4:5, :]) + mods_ref[0, 3:4, :]).astype(BF16)
    acc = None
    for c0, cw in FF_CHUNKS:
        gte = jnp.dot(xb, wfi_ref[:, c0:c0 + cw], preferred_element_type=F32)
        up = jnp.dot(xb, wfi_ref[:, D_FF + c0:D_FF + c0 + cw], preferred_element_type=F32)
        act = (gte * jax.nn.sigmoid(gte) * up).astype(BF16)
        part = jnp.dot(act, wfo_ref[c0:c0 + cw, :], preferred_element_type=F32)
        acc = part if acc is None else acc + part
    o_ref[...] = x1 + mods_ref[0, 5:6, :] * acc


def _ffn(x2d, att, mlp, mods3, g2, w_out, w_ffn_in, w_ffn_out, *, mods_row_fn):
    n_tok = x2d.shape[0]
    tm = TM_FFN
    return pl.pallas_call(
        _ffn_kernel,
        grid=(n_tok // tm,),
        in_specs=[
            pl.BlockSpec((tm, D_MODEL), lambda i: (i, 0)),
            pl.BlockSpec((tm, ATT_WIDTH), lambda i: (i, 0)),
            pl.BlockSpec((tm, MLP_WIDTH), lambda i: (i, 0)),
            pl.BlockSpec((1, 6, D_MODEL), lambda i: (mods_row_fn(i), 0, 0)),
            _const_spec((1, D_MODEL)),
            _const_spec((D_MODEL, D_MODEL)),
            _const_spec((D_MODEL, 2 * D_FF)),
            _const_spec((D_FF, D_MODEL)),
        ],
        out_specs=pl.BlockSpec((tm, D_MODEL), lambda i: (i, 0)),
        out_shape=jax.ShapeDtypeStruct((n_tok, D_MODEL), F32),
        compiler_params=_params(1),
        name="ffn",
    )(x2d, att, mlp, mods3, g2, w_out, w_ffn_in, w_ffn_out)


def _rope_tables(n):
    pos = jnp.arange(n)
    row = (pos // GRID_W).astype(F32)
    col = (pos % GRID_W).astype(F32)
    inv = ROPE_THETA ** (-jnp.arange(0, ROPE_AXIS_DIM, 2, dtype=F32) / ROPE_AXIS_DIM)
    ang_r = row[:, None] * inv[None, :]
    ang_c = col[:, None] * inv[None, :]
    cos64 = jnp.concatenate([jnp.cos(ang_r)] * 2 + [jnp.cos(ang_c)] * 2, axis=1)
    sin64 = jnp.concatenate([-jnp.sin(ang_r), jnp.sin(ang_r), -jnp.sin(ang_c), jnp.sin(ang_c)], axis=1)
    return jnp.tile(cos64, (1, 2)), jnp.tile(sin64, (1, 2))


def kernel(x_prompt, x_sample, cache_k_ctx, cache_v_ctx, c, c_ctx, norm1_g, norm2_g, w_ada, b_ada, w_in, q_norm_g, k_norm_g, lambda_q1, lambda_k1, lambda_q2, lambda_k2, att_out_g, sgu_norm_g, sgu_w, sgu_b, mlp_out_g, w_out, w_ffn_in, w_ffn_out):
    n_ctx, ctx_len, _ = x_prompt.shape
    n_dec, dec_len, _ = x_sample.shape
    depth = norm1_g.shape[0]

    cos_t, sin_t = _rope_tables(dec_len)
    q_scale = LOG2E / math.sqrt(HEAD_DIM)
    rope_tabs = (cos_t * q_scale, sin_t * q_scale, cos_t, sin_t)
    group = jnp.arange(ATT_WIDTH) // HEAD_DIM
    pool = jnp.where(group[:, None] == group[None, :], 1.0 / HEAD_DIM, 0.0).astype(BF16)

    cond = jnp.concatenate(
        [c, c_ctx[None, :], jnp.zeros((MODS_ROWS - n_dec - 1, D_MODEL), F32)], axis=0)

    xp = x_prompt.reshape(n_ctx * ctx_len, D_MODEL)
    xs = x_sample.reshape(n_dec * dec_len, D_MODEL)
    k_states, v_states = [], []
    ctx_row = lambda i: CTX_ROW
    dec_row_proj = lambda i: i // (dec_len // TM_PROJ)
    dec_row_ffn = lambda i: i // (dec_len // TM_FFN)

    for l in range(depth):
        lambda_init = 0.8 - 0.6 * math.exp(-0.3 * l)
        mods3 = _mods(cond, w_ada[l], b_ada[l][None, :]).reshape(MODS_ROWS, 6, D_MODEL)
        g1 = norm1_g[l][None, :]
        g2 = norm2_g[l][None, :]
        w_in_b = w_in[l].astype(BF16)
        w_vt = w_in[l][:, 2 * ATT_WIDTH:3 * ATT_WIDTH].T.astype(BF16)
        w_out_b = w_out[l].astype(BF16)
        w_fi_b = w_ffn_in[l].astype(BF16)
        w_fo_b = w_ffn_out[l].astype(BF16)
        qg = jnp.tile(q_norm_g[l], ATT_WIDTH // HEAD_DIM)[None, :]
        kg = jnp.tile(k_norm_g[l], ATT_WIDTH // HEAD_DIM)[None, :]
        sgun = sgu_norm_g[l][None, :]
        sguw = sgu_w[l].astype(BF16)
        sgub = jnp.broadcast_to(sgu_b[l][:, :, None], (N_GROUPS, CHUNK, GROUP_DIM))
        mlpg = mlp_out_g[l][None, :]
        att_g = att_out_g[l][None, :]
        lams = (lambda_q1[l][None, :], lambda_k1[l][None, :],
                lambda_q2[l][None, :], lambda_k2[l][None, :])

        q, k, vt, mlp, k_c, v_c = _proj(
            xp, mods3, g1, w_in_b, pool, qg, kg, None, sgun, sguw, sgub, mlpg,
            seq_len=ctx_len, mods_row_fn=ctx_row)
        att = _attn_ctx(lams, q, k, vt, att_g, seq_len=ctx_len, lambda_init=lambda_init)
        xp = _ffn(xp, att, mlp, mods3, g2, w_out_b, w_fi_b, w_fo_b, mods_row_fn=ctx_row)
        k_states.append(k_c)
        v_states.append(v_c)

        kc = cache_k_ctx[:, l]
        kc = jnp.swapaxes(kc, 2, 3).reshape(n_dec, N_HEADS, -1, 2 * HEAD_DIM).astype(BF16)
        vtc = jnp.swapaxes(cache_v_ctx[:, l], 2, 3).astype(BF16)
        q, k, vt, mlp = _proj(
            xs, mods3, g1, w_in_b, pool, qg, kg, rope_tabs, sgun, sguw, sgub, mlpg,
            seq_len=dec_len, mods_row_fn=dec_row_proj, w_vt=w_vt)
        att = _attn_cache(lams, q, k, vt, kc, vtc, att_g, n_batch=n_dec, seq_len=dec_len,
                          lambda_init=lambda_init)
        xs = _ffn(xs, att, mlp, mods3, g2, w_out_b, w_fi_b, w_fo_b, mods_row_fn=dec_row_ffn)

    state_k = jnp.concatenate(k_states, axis=1)
    state_v = jnp.concatenate(v_states, axis=1)
    return (xp.reshape(n_ctx, ctx_len, D_MODEL), xs.reshape(n_dec, dec_len, D_MODEL),
            state_k, state_v)
```

```python
import functools
import math

import jax
import jax.numpy as jnp
import numpy as np
from jax import lax
from jax.experimental import pallas as pl
from jax.experimental.pallas import tpu as pltpu

D_MODEL = 1024
ATT_WIDTH = 512
N_HEADS = 4
HEAD_DIM = 64
VAL_DIM = 128
MLP_WIDTH = 512
N_GROUPS = 4
GROUP_DIM = 128
CHUNK = 128
D_FF = 2816
IN_WIDTH = 2560
GRID_W = 64
ROPE_THETA = 10000.0
ROPE_AXIS_DIM = 32
EPS = 1e-6
LOG2E = 1.4426950408889634
LANES = 128

F32 = jnp.float32
BF16 = jnp.bfloat16

VMEM_LIMIT_BYTES = 56 * 1024 * 1024
MODS_ROWS = 16
CTX_ROW = 8

TM_PROJ = 512
TM_FFN = 512
TM_ATTN_CTX = 512
CTX_AHEAD = 3
FF_CHUNKS = ((0, 1024), (1024, 1024), (2048, 768))

ONES_ROWS = 16
TQ_UNIT = 256
KEY_BLOCK = 256

NT_DIMS = (((1,), (1,)), ((), ()))


def _const_spec(shape):
    zeros = (0,) * len(shape)
    return pl.BlockSpec(shape, lambda *_: zeros, pipeline_mode=pl.Buffered(1))


def _params(n_grid):
    return pltpu.CompilerParams(
        dimension_semantics=("arbitrary",) * n_grid,
        vmem_limit_bytes=VMEM_LIMIT_BYTES,
    )


def _rms_scale(x):
    return lax.rsqrt(jnp.mean(x * x, axis=-1, keepdims=True) + EPS)


def _mods_kernel(cond_ref, w_ref, b_ref, o_ref):
    cnd = cond_ref[...]
    act = (cnd * jax.nn.sigmoid(cnd)).astype(BF16)
    o_ref[...] = jnp.dot(act, w_ref[...].astype(BF16), preferred_element_type=F32) + b_ref[...]


def _mods(cond, w_ada, b_ada):
    tn = 1536
    n_out = w_ada.shape[1]
    return pl.pallas_call(
        _mods_kernel,
        grid=(n_out // tn,),
        in_specs=[
            _const_spec((MODS_ROWS, D_MODEL)),
            pl.BlockSpec((D_MODEL, tn), lambda j: (0, j)),
            pl.BlockSpec((1, tn), lambda j: (0, j)),
        ],
        out_specs=pl.BlockSpec((MODS_ROWS, tn), lambda j: (0, j)),
        out_shape=jax.ShapeDtypeStruct((MODS_ROWS, n_out), F32),
        compiler_params=_params(1),
        name="mods",
    )(cond, w_ada, b_ada)


def _proj_kernel(*refs, rope, seq_len, tm):
    it = iter(refs)
    x_ref, mods_ref, g1_ref, w_in_ref, pool_ref, qg_ref, kg_ref = (next(it) for _ in range(7))
    if rope:
        cos_ref, sin_ref = (next(it) for _ in range(2))
    sgun_ref, sguw_ref, sgub_ref, mlpg_ref = (next(it) for _ in range(4))
    q_ref, k_ref, vt_ref, mlp_ref = (next(it) for _ in range(4))
    if not rope:
        kst_ref, vst_ref = (next(it) for _ in range(2))
    gate_ref = next(it)

    x = x_ref[...]
    xn = x * _rms_scale(x) * g1_ref[...]
    xm = xn * (1.0 + mods_ref[0, 1:2, :]) + mods_ref[0, 0:1, :]
    xb = xm.astype(BF16)

    def section(lo, hi):
        return jnp.dot(xb, w_in_ref[:, lo:hi], preferred_element_type=F32)

    def head_norm(h, g_ref):
        sq = (h * h).astype(BF16)
        msq = jnp.dot(sq, pool_ref[...], preferred_element_type=F32)
        return h * lax.rsqrt(msq + EPS) * g_ref[...]

    if rope:
        lane = lax.broadcasted_iota(jnp.int32, (tm, LANES), 1)
        first_half = (lane % 32) < 16

    def emit_rotary(hn, out_ref):
        for c in range(ATT_WIDTH // LANES):
            hc = hn[:, c * LANES:(c + 1) * LANES]
            swapped = jnp.where(first_half,
                                pltpu.roll(hc, LANES - 16, 1),
                                pltpu.roll(hc, 16, 1))
            out_ref[:, c * LANES:(c + 1) * LANES] = (
                hc * cos_ref[...] + swapped * sin_ref[...]).astype(BF16)

    qn = head_norm(section(0, ATT_WIDTH), qg_ref)
    kn = head_norm(section(ATT_WIDTH, 2 * ATT_WIDTH), kg_ref)
    if rope:
        emit_rotary(qn, q_ref)
        emit_rotary(kn, k_ref)
    else:
        q_ref[...] = qn.astype(BF16)
        k_ref[...] = kn.astype(BF16)

    hv = section(2 * ATT_WIDTH, 3 * ATT_WIDTH)
    vt_ref[...] = hv.T.astype(BF16)
    if not rope:
        knt = kn.T
        for s in range(tm // seq_len):
            rows = slice(s * seq_len, (s + 1) * seq_len)
            for h in range(N_HEADS):
                for i in range(2):
                    j = 2 * h + i
                    kst_ref[s, 0, h, i, :, :] = knt[j * HEAD_DIM:(j + 1) * HEAD_DIM, rows]
                vst_ref[s, 0, h, :, :] = hv[rows, h * VAL_DIM:(h + 1) * VAL_DIM]

    hu = section(3 * ATT_WIDTH, 3 * ATT_WIDTH + MLP_WIDTH)
    hg = section(3 * ATT_WIDTH + MLP_WIDTH, IN_WIDTH)
    for g in range(N_GROUPS):
        cols = slice(g * GROUP_DIM, (g + 1) * GROUP_DIM)
        gg = hg[:, cols]
        gn = (gg * _rms_scale(gg) * sgun_ref[:, cols]).astype(BF16)
        ug = hu[:, cols]
        wg = sguw_ref[g]
        bg = sgub_ref[g]
        for n in range(tm // CHUNK):
            rows = slice(n * CHUNK, (n + 1) * CHUNK)
            sp = jnp.dot(wg, gn[rows, :], preferred_element_type=F32) + bg
            gate_ref[rows, cols] = ug[rows, :] * sp
    o = gate_ref[...]
    mlp_ref[...] = (o * _rms_scale(o) * mlpg_ref[...]).astype(BF16)


def _proj(x2d, mods3, g1, w_in, pool, qg, kg, rope_tabs, sgun, sguw, sgub, mlpg,
          *, seq_len, mods_row_fn):
    n_tok = x2d.shape[0]
    tm = TM_PROJ
    rope = rope_tabs is not None
    blocks_per_seq = seq_len // tm if rope else None

    in_specs = [
        pl.BlockSpec((tm, D_MODEL), lambda i: (i, 0)),
        pl.BlockSpec((1, 6, D_MODEL), lambda i: (mods_row_fn(i), 0, 0)),
        _const_spec((1, D_MODEL)),
        _const_spec((D_MODEL, IN_WIDTH)),
        _const_spec((ATT_WIDTH, ATT_WIDTH)),
        _const_spec((1, ATT_WIDTH)),
        _const_spec((1, ATT_WIDTH)),
    ]
    args = [x2d, mods3, g1, w_in, pool, qg, kg]
    if rope:
        tab_spec = pl.BlockSpec((tm, LANES), lambda i: (i % blocks_per_seq, 0))
        in_specs += [tab_spec] * 2
        args += list(rope_tabs)
    in_specs += [
        _const_spec((1, MLP_WIDTH)),
        _const_spec((N_GROUPS, CHUNK, CHUNK)),
        _const_spec((N_GROUPS, CHUNK, GROUP_DIM)),
        _const_spec((1, MLP_WIDTH)),
    ]
    args += [sgun, sguw, sgub, mlpg]

    tok_spec = pl.BlockSpec((tm, ATT_WIDTH), lambda i: (i, 0))
    tok_shape = jax.ShapeDtypeStruct((n_tok, ATT_WIDTH), BF16)
    out_specs = [tok_spec] * 4
    out_shape = [tok_shape] * 4
    out_specs[2] = pl.BlockSpec((ATT_WIDTH, tm), lambda i: (0, i))
    out_shape[2] = jax.ShapeDtypeStruct((ATT_WIDTH, n_tok), BF16)
    if not rope:
        n_seq = n_tok // seq_len
        spb = tm // seq_len
        out_specs += [
            pl.BlockSpec((spb, 1, N_HEADS, 2, HEAD_DIM, seq_len), lambda i: (i, 0, 0, 0, 0, 0)),
            pl.BlockSpec((spb, 1, N_HEADS, seq_len, VAL_DIM), lambda i: (i, 0, 0, 0, 0)),
        ]
        out_shape += [
            jax.ShapeDtypeStruct((n_seq, 1, N_HEADS, 2, HEAD_DIM, seq_len), F32),
            jax.ShapeDtypeStruct((n_seq, 1, N_HEADS, seq_len, VAL_DIM), F32),
        ]

    return pl.pallas_call(
        functools.partial(_proj_kernel, rope=rope, seq_len=seq_len, tm=tm),
        grid=(n_tok // tm,),
        in_specs=in_specs,
        out_specs=out_specs,
        out_shape=out_shape,
        scratch_shapes=[pltpu.VMEM((tm, MLP_WIDTH), F32)],
        compiler_params=_params(1),
        name="proj_rope" if rope else "proj_ctx",
    )(*args)


def _lambda_full(lq1, lk1, lq2, lk2, lambda_init):
    return (jnp.exp(jnp.sum(lq1[...] * lk1[...], keepdims=True))
            - jnp.exp(jnp.sum(lq2[...] * lk2[...], keepdims=True))
            + lambda_init)


def _map_queries(qc):
    lane = lax.broadcasted_iota(jnp.int32, qc.shape, 1)
    zero = jnp.zeros_like(qc)
    return (jnp.where(lane < HEAD_DIM, qc, zero), jnp.where(lane >= HEAD_DIM, qc, zero))


def _combine_maps(o1, d1, o2, d2, lam, out_gain):
    ot = o1 * (1.0 / d1) - o2 * (lam / d2)
    ot = ot * lax.rsqrt(jnp.mean(ot * ot, axis=0, keepdims=True) + EPS)
    return (ot.T * out_gain).astype(BF16)


def _attn_ctx_kernel(lq1, lk1, lq2, lk2, q_ref, k_ref, vt_ref, ag_ref, o_ref, st_buf,
                     *, lambda_init, seq_len):
    lam = _lambda_full(lq1, lk1, lq2, lk2, lambda_init)
    units = [(slice(s * seq_len, (s + 1) * seq_len), slice(h * LANES, (h + 1) * LANES))
             for s in range(q_ref.shape[0] // seq_len) for h in range(N_HEADS)]

    def scores(u):
        rows, cols = units[u]
        kk = k_ref[rows, cols]
        maxes = []
        for mp, qm in enumerate(_map_queries(q_ref[rows, cols])):
            st = lax.dot_general(kk, qm, NT_DIMS, preferred_element_type=F32)
            st_buf[u % CTX_AHEAD, mp] = st
            maxes.append(jnp.max(st, axis=0, keepdims=True))
        return maxes

    def finish(u, maxes):
        rows, cols = units[u]
        vt = vt_ref[cols, rows]
        outs, dens = [], []
        for mp in range(2):
            e = jnp.exp2(st_buf[u % CTX_AHEAD, mp] - maxes[mp])
            dens.append(jnp.sum(e, axis=0, keepdims=True))
            outs.append(jnp.dot(vt, e.astype(BF16), preferred_element_type=F32))
        out_gain = (1.0 - lambda_init) * ag_ref[:, cols]
        o_ref[rows, cols] = _combine_maps(outs[0], dens[0], outs[1], dens[1], lam, out_gain)

    pending = [scores(u) for u in range(CTX_AHEAD - 1)]
    for u in range(len(units)):
        if u + CTX_AHEAD - 1 < len(units):
            pending.append(scores(u + CTX_AHEAD - 1))
        finish(u, pending.pop(0))


def _attn_ctx(lams, q, k, vt, att_g, *, seq_len, lambda_init):
    n_tok = q.shape[0]
    tm = TM_ATTN_CTX
    return pl.pallas_call(
        functools.partial(_attn_ctx_kernel, lambda_init=lambda_init, seq_len=seq_len),
        grid=(n_tok // tm,),
        in_specs=[_const_spec((1, HEAD_DIM))] * 4 + [
            pl.BlockSpec((tm, ATT_WIDTH), lambda i: (i, 0)),
            pl.BlockSpec((tm, ATT_WIDTH), lambda i: (i, 0)),
            pl.BlockSpec((ATT_WIDTH, tm), lambda i: (0, i)),
            _const_spec((1, ATT_WIDTH)),
        ],
        out_specs=pl.BlockSpec((tm, ATT_WIDTH), lambda i: (i, 0)),
        out_shape=jax.ShapeDtypeStruct((n_tok, ATT_WIDTH), BF16),
        scratch_shapes=[pltpu.VMEM((CTX_AHEAD, 2, seq_len, seq_len), F32)],
        compiler_params=_params(1),
        name="attn_ctx",
    )(*lams, q, k, vt, att_g)


def _attn_cache_kernel(lq1, lk1, lq2, lk2, q_ref, k_ref, vt_ref, kct_ref, vc_ref, ag_ref, o_ref,
                       k_all, vt_all, m_buf, *bufs, lambda_init, n_new):
    st = (bufs[0:2], bufs[2:4])
    n_chunks = n_new // TQ_UNIT
    n_keys = k_all.shape[0]

    past = n_keys - n_new
    k_all[0:n_new, :] = k_ref[...]
    k_all[n_new:, :] = kct_ref[0, 0, 0].reshape(2 * HEAD_DIM, past).T.astype(BF16)
    vt_all[0:VAL_DIM, 0:n_new] = vt_ref[...]
    vt_all[0:VAL_DIM, n_new:] = vc_ref[0, 0, 0].T.astype(BF16)
    vt_all[VAL_DIM:, :] = jnp.ones((ONES_ROWS, n_keys), BF16)

    lam = _lambda_full(lq1, lk1, lq2, lk2, lambda_init)
    out_gain = (1.0 - lambda_init) * ag_ref[...]

    def rows(c):
        return pl.ds(pl.multiple_of(c * TQ_UNIT, TQ_UNIT), TQ_UNIT)

    def stage(fin, sc):
        if sc is not None:
            qms = _map_queries(q_ref[rows(sc[0]), :])
            mrun = [None, None]
        if fin is not None:
            ms = [m_buf[fin[1], mp] for mp in range(2)]
            accs = [None, None]
        for kb in range(n_keys // KEY_BLOCK):
            kr = slice(kb * KEY_BLOCK, (kb + 1) * KEY_BLOCK)
            if sc is not None:
                kk = k_all[kr, :]
                for mp in range(2):
                    s = lax.dot_general(kk, qms[mp], NT_DIMS, preferred_element_type=F32)
                    st[sc[1]][mp][kr, :] = s
                    smax = jnp.max(s.reshape(KEY_BLOCK // 8, 8, TQ_UNIT), axis=0)
                    mrun[mp] = smax if mrun[mp] is None else jnp.maximum(mrun[mp], smax)
            if fin is not None:
                vt = vt_all[:, kr]
                for mp in range(2):
                    p = jnp.exp2(st[fin[1]][mp][kr, :] - ms[mp]).astype(BF16)
                    d = jnp.dot(vt, p, preferred_element_type=F32)
                    accs[mp] = d if accs[mp] is None else accs[mp] + d
        if sc is not None:
            for mp in range(2):
                m_buf[sc[1], mp] = jnp.max(mrun[mp], axis=0, keepdims=True)
        if fin is not None:
            o1, o2 = accs
            o_ref[rows(fin[0]), :] = _combine_maps(
                o1[0:VAL_DIM, :], o1[VAL_DIM:VAL_DIM + 1, :],
                o2[0:VAL_DIM, :], o2[VAL_DIM:VAL_DIM + 1, :], lam, out_gain)

    stage(None, (0, 0))

    def pair(i, carry):
        c = 2 * i
        stage((c, 0), (c + 1, 1))
        stage((c + 1, 1), (c + 2, 0))
        return carry

    lax.fori_loop(0, n_chunks // 2 - 1, pair, 0)
    stage((n_chunks - 2, 0), (n_chunks - 1, 1))
    stage((n_chunks - 1, 1), None)


def _attn_cache(lams, q, k, vt, kct, vc, att_g, *, layer, n_batch, seq_len, lambda_init):
    past = vc.shape[3]
    n_keys = seq_len + past
    assert (seq_len // TQ_UNIT) % 2 == 0 and seq_len // TQ_UNIT >= 4
    assert n_keys % KEY_BLOCK == 0
    in_specs = [_const_spec((1, HEAD_DIM))] * 4 + [
        pl.BlockSpec((seq_len, LANES), lambda b, h: (b, h)),
        pl.BlockSpec((seq_len, LANES), lambda b, h: (b, h)),
        pl.BlockSpec((VAL_DIM, seq_len), lambda b, h: (h, b)),
        pl.BlockSpec((1, 1, 1, 2, HEAD_DIM, past), lambda b, h: (b, layer, h, 0, 0, 0)),
        pl.BlockSpec((1, 1, 1, past, VAL_DIM), lambda b, h: (b, layer, h, 0, 0)),
        pl.BlockSpec((1, LANES), lambda b, h: (0, h)),
    ]
    return pl.pallas_call(
        functools.partial(_attn_cache_kernel, lambda_init=lambda_init, n_new=seq_len),
        grid=(n_batch, N_HEADS),
        in_specs=in_specs,
        out_specs=pl.BlockSpec((seq_len, LANES), lambda b, h: (b, h)),
        out_shape=jax.ShapeDtypeStruct((n_batch * seq_len, ATT_WIDTH), BF16),
        scratch_shapes=([pltpu.VMEM((n_keys, LANES), BF16),
                         pltpu.VMEM((VAL_DIM + ONES_ROWS, n_keys), BF16),
                         pltpu.VMEM((2, 2, 1, TQ_UNIT), F32)]
                        + [pltpu.VMEM((n_keys, TQ_UNIT), F32)] * 4),
        compiler_params=_params(2),
        name="attn_cache",
    )(*lams, q, k, vt, kct, vc, att_g)


def _ffn_kernel(x_ref, att_ref, mlp_ref, mods_ref, g2_ref, wo_ref, wfi_ref, wfo_ref, o_ref):
    y = (jnp.dot(att_ref[...], wo_ref[0:ATT_WIDTH, :], preferred_element_type=F32)
         + jnp.dot(mlp_ref[...], wo_ref[ATT_WIDTH:, :], preferred_element_type=F32))
    x1 = x_ref[...] + mods_ref[0, 2:3, :] * y
    xn = x1 * _rms_scale(x1) * g2_ref[...]
    xb = (xn * (1.0 + mods_ref[0, 4:5, :]) + mods_ref[0, 3:4, :]).astype(BF16)
    acc = None
    for c0, cw in FF_CHUNKS:
        gte = jnp.dot(xb, wfi_ref[:, c0:c0 + cw], preferred_element_type=F32)
        up = jnp.dot(xb, wfi_ref[:, D_FF + c0:D_FF + c0 + cw], preferred_element_type=F32)
        act = (gte * jax.nn.sigmoid(gte) * up).astype(BF16)
        part = jnp.dot(act, wfo_ref[c0:c0 + cw, :], preferred_element_type=F32)
        acc = part if acc is None else acc + part
    o_ref[...] = x1 + mods_ref[0, 5:6, :] * acc


def _ffn(x2d, att, mlp, mods3, g2, w_out, w_ffn_in, w_ffn_out, *, mods_row_fn):
    n_tok = x2d.shape[0]
    tm = TM_FFN
    return pl.pallas_call(
        _ffn_kernel,
        grid=(n_tok // tm,),
        in_specs=[
            pl.BlockSpec((tm, D_MODEL), lambda i: (i, 0)),
            pl.BlockSpec((tm, ATT_WIDTH), lambda i: (i, 0)),
            pl.BlockSpec((tm, MLP_WIDTH), lambda i: (i, 0)),
            pl.BlockSpec((1, 6, D_MODEL), lambda i: (mods_row_fn(i), 0, 0)),
            _const_spec((1, D_MODEL)),
            _const_spec((D_MODEL, D_MODEL)),
            _const_spec((D_MODEL, 2 * D_FF)),
            _const_spec((D_FF, D_MODEL)),
        ],
        out_specs=pl.BlockSpec((tm, D_MODEL), lambda i: (i, 0)),
        out_shape=jax.ShapeDtypeStruct((n_tok, D_MODEL), F32),
        compiler_params=_params(1),
        name="ffn",
    )(x2d, att, mlp, mods3, g2, w_out, w_ffn_in, w_ffn_out)


def _rope_tables(n):
    pos = np.arange(n)
    row = (pos // GRID_W).astype(np.float32)
    col = (pos % GRID_W).astype(np.float32)
    inv = (ROPE_THETA ** (-np.arange(0, ROPE_AXIS_DIM, 2, dtype=np.float32) / ROPE_AXIS_DIM)
           ).astype(np.float32)
    ang_r = row[:, None] * inv[None, :]
    ang_c = col[:, None] * inv[None, :]
    cos64 = np.concatenate([np.cos(ang_r)] * 2 + [np.cos(ang_c)] * 2, axis=1)
    sin64 = np.concatenate([-np.sin(ang_r), np.sin(ang_r), -np.sin(ang_c), np.sin(ang_c)], axis=1)
    return (np.tile(cos64, (1, 2)).astype(np.float32), np.tile(sin64, (1, 2)).astype(np.float32))


def kernel(x_prompt, x_sample, cache_k_ctx, cache_v_ctx, c, c_ctx, norm1_g, norm2_g, w_ada, b_ada, w_in, q_norm_g, k_norm_g, lambda_q1, lambda_k1, lambda_q2, lambda_k2, att_out_g, sgu_norm_g, sgu_w, sgu_b, mlp_out_g, w_out, w_ffn_in, w_ffn_out):
    n_ctx, ctx_len, _ = x_prompt.shape
    n_dec, dec_len, _ = x_sample.shape
    depth = norm1_g.shape[0]

    rope_tabs = _rope_tables(dec_len)
    q_scale = LOG2E / math.sqrt(HEAD_DIM)
    group = np.arange(ATT_WIDTH) // HEAD_DIM
    pool = jnp.asarray(np.where(group[:, None] == group[None, :], 1.0 / HEAD_DIM, 0.0), BF16)

    cond = jnp.concatenate(
        [c, c_ctx[None, :], jnp.zeros((MODS_ROWS - n_dec - 1, D_MODEL), F32)], axis=0)

    xp = x_prompt.reshape(n_ctx * ctx_len, D_MODEL)
    xs = x_sample.reshape(n_dec * dec_len, D_MODEL)
    cache_kt = jnp.swapaxes(cache_k_ctx, -1, -2)
    k_states, v_states = [], []
    ctx_row = lambda i: CTX_ROW
    dec_row_proj = lambda i: i // (dec_len // TM_PROJ)
    dec_row_ffn = lambda i: i // (dec_len // TM_FFN)

    for l in range(depth):
        lambda_init = 0.8 - 0.6 * math.exp(-0.3 * l)
        mods3 = _mods(cond, w_ada[l], b_ada[l][None, :]).reshape(MODS_ROWS, 6, D_MODEL)
        g1 = norm1_g[l][None, :]
        g2 = norm2_g[l][None, :]
        w_in_b = w_in[l].astype(BF16)
        w_out_b = w_out[l].astype(BF16)
        w_fi_b = w_ffn_in[l].astype(BF16)
        w_fo_b = w_ffn_out[l].astype(BF16)
        qg = jnp.tile(q_norm_g[l] * q_scale, ATT_WIDTH // HEAD_DIM)[None, :]
        kg = jnp.tile(k_norm_g[l], ATT_WIDTH // HEAD_DIM)[None, :]
        sgun = sgu_norm_g[l][None, :]
        sguw = sgu_w[l].astype(BF16)
        sgub = jnp.broadcast_to(sgu_b[l][:, :, None], (N_GROUPS, CHUNK, GROUP_DIM))
        mlpg = mlp_out_g[l][None, :]
        att_g = att_out_g[l][None, :]
        lams = (lambda_q1[l][None, :], lambda_k1[l][None, :],
                lambda_q2[l][None, :], lambda_k2[l][None, :])

        q, k, vt, mlp, k_c, v_c = _proj(
            xp, mods3, g1, w_in_b, pool, qg, kg, None, sgun, sguw, sgub, mlpg,
            seq_len=ctx_len, mods_row_fn=ctx_row)
        att = _attn_ctx(lams, q, k, vt, att_g, seq_len=ctx_len, lambda_init=lambda_init)
        xp = _ffn(xp, att, mlp, mods3, g2, w_out_b, w_fi_b, w_fo_b, mods_row_fn=ctx_row)
        k_states.append(k_c)
        v_states.append(v_c)

        q, k, vt, mlp = _proj(
            xs, mods3, g1, w_in_b, pool, qg, kg, rope_tabs, sgun, sguw, sgub, mlpg,
            seq_len=dec_len, mods_row_fn=dec_row_proj)
        att = _attn_cache(lams, q, k, vt, cache_kt, cache_v_ctx, att_g, layer=l,
                          n_batch=n_dec, seq_len=dec_len, lambda_init=lambda_init)
        xs = _ffn(xs, att, mlp, mods3, g2, w_out_b, w_fi_b, w_fo_b, mods_row_fn=dec_row_ffn)

    state_k = jnp.swapaxes(jnp.concatenate(k_states, axis=1), -1, -2)
    state_v = jnp.concatenate(v_states, axis=1)
    return (xp.reshape(n_ctx, ctx_len, D_MODEL), xs.reshape(n_dec, dec_len, D_MODEL),
            state_k, state_v)
```

```python
import functools
import math

import jax
import jax.numpy as jnp
import numpy as np
from jax import lax
from jax.experimental import pallas as pl
from jax.experimental.pallas import tpu as pltpu

D_MODEL = 1024
ATT_WIDTH = 512
N_HEADS = 4
HEAD_DIM = 64
VAL_DIM = 128
MLP_WIDTH = 512
N_GROUPS = 4
GROUP_DIM = 128
CHUNK = 128
D_FF = 2816
IN_WIDTH = 2560
GRID_W = 64
ROPE_THETA = 10000.0
ROPE_AXIS_DIM = 32
EPS = 1e-6
LOG2E = 1.4426950408889634
LANES = 128

F32 = jnp.float32
BF16 = jnp.bfloat16

VMEM_LIMIT_BYTES = 56 * 1024 * 1024
MODS_ROWS = 16
CTX_ROW = 8

TM_PROJ = 512
TM_FFN = 512
TM_ATTN_CTX = 512
CTX_AHEAD = 3
FF_CHUNKS = ((0, 1024), (1024, 1024), (2048, 768))

ONES_ROWS = 16
TQ_UNIT = 256
KEY_BLOCK = 256
SCORE_LEAD = 3

NT_DIMS = (((1,), (1,)), ((), ()))


def _const_spec(shape):
    zeros = (0,) * len(shape)
    return pl.BlockSpec(shape, lambda *_: zeros, pipeline_mode=pl.Buffered(1))


def _params(n_grid):
    return pltpu.CompilerParams(
        dimension_semantics=("arbitrary",) * n_grid,
        vmem_limit_bytes=VMEM_LIMIT_BYTES,
    )


def _rms_scale(x):
    return lax.rsqrt(jnp.mean(x * x, axis=-1, keepdims=True) + EPS)


def _mods_kernel(cond_ref, w_ref, b_ref, o_ref):
    cnd = cond_ref[...]
    act = (cnd * jax.nn.sigmoid(cnd)).astype(BF16)
    o_ref[...] = jnp.dot(act, w_ref[...].astype(BF16), preferred_element_type=F32) + b_ref[...]


def _mods(cond, w_ada, b_ada):
    tn = 1536
    n_out = w_ada.shape[1]
    return pl.pallas_call(
        _mods_kernel,
        grid=(n_out // tn,),
        in_specs=[
            _const_spec((MODS_ROWS, D_MODEL)),
            pl.BlockSpec((D_MODEL, tn), lambda j: (0, j)),
            pl.BlockSpec((1, tn), lambda j: (0, j)),
        ],
        out_specs=pl.BlockSpec((MODS_ROWS, tn), lambda j: (0, j)),
        out_shape=jax.ShapeDtypeStruct((MODS_ROWS, n_out), F32),
        compiler_params=_params(1),
        name="mods",
    )(cond, w_ada, b_ada)


def _proj_kernel(*refs, rope, seq_len, tm):
    it = iter(refs)
    x_ref, mods_ref, g1_ref, w_in_ref, pool_ref, qg_ref, kg_ref = (next(it) for _ in range(7))
    if rope:
        cos_ref, sin_ref = (next(it) for _ in range(2))
    sgun_ref, sguw_ref, sgub_ref, mlpg_ref = (next(it) for _ in range(4))
    q_ref, k_ref, vt_ref, mlp_ref = (next(it) for _ in range(4))
    if not rope:
        kst_ref, vst_ref = (next(it) for _ in range(2))
    gate_ref = next(it)

    x = x_ref[...]
    xn = x * _rms_scale(x) * g1_ref[...]
    xm = xn * (1.0 + mods_ref[0, 1:2, :]) + mods_ref[0, 0:1, :]
    xb = xm.astype(BF16)

    def section(lo, hi):
        return jnp.dot(xb, w_in_ref[:, lo:hi], preferred_element_type=F32)

    def head_norm(h, g_ref):
        sq = (h * h).astype(BF16)
        msq = jnp.dot(sq, pool_ref[...], preferred_element_type=F32)
        return h * lax.rsqrt(msq + EPS) * g_ref[...]

    if rope:
        lane = lax.broadcasted_iota(jnp.int32, (tm, LANES), 1)
        first_half = (lane % 32) < 16

    def emit_heads(hn, out_ref):
        for h in range(N_HEADS):
            hc = hn[:, h * LANES:(h + 1) * LANES]
            if rope:
                swapped = jnp.where(first_half,
                                    pltpu.roll(hc, LANES - 16, 1),
                                    pltpu.roll(hc, 16, 1))
                hc = hc * cos_ref[...] + swapped * sin_ref[...]
            out_ref[h] = hc.astype(BF16)

    qn = head_norm(section(0, ATT_WIDTH), qg_ref)
    kn = head_norm(section(ATT_WIDTH, 2 * ATT_WIDTH), kg_ref)
    emit_heads(qn, q_ref)
    emit_heads(kn, k_ref)

    hv = section(2 * ATT_WIDTH, 3 * ATT_WIDTH)
    hvt = hv.T.astype(BF16)
    for h in range(N_HEADS):
        vt_ref[h] = hvt[h * VAL_DIM:(h + 1) * VAL_DIM, :]
    if not rope:
        knt = kn.T
        for s in range(tm // seq_len):
            rows = slice(s * seq_len, (s + 1) * seq_len)
            for h in range(N_HEADS):
                for i in range(2):
                    j = 2 * h + i
                    kst_ref[s, 0, h, i, :, :] = knt[j * HEAD_DIM:(j + 1) * HEAD_DIM, rows]
                vst_ref[s, 0, h, :, :] = hv[rows, h * VAL_DIM:(h + 1) * VAL_DIM]

    hu = section(3 * ATT_WIDTH, 3 * ATT_WIDTH + MLP_WIDTH)
    hg = section(3 * ATT_WIDTH + MLP_WIDTH, IN_WIDTH)
    for g in range(N_GROUPS):
        cols = slice(g * GROUP_DIM, (g + 1) * GROUP_DIM)
        gg = hg[:, cols]
        gn = (gg * _rms_scale(gg) * sgun_ref[:, cols]).astype(BF16)
        ug = hu[:, cols]
        wg = sguw_ref[g]
        bg = sgub_ref[g]
        for n in range(tm // CHUNK):
            rows = slice(n * CHUNK, (n + 1) * CHUNK)
            sp = jnp.dot(wg, gn[rows, :], preferred_element_type=F32) + bg
            gate_ref[rows, cols] = ug[rows, :] * sp
    o = gate_ref[...]
    mlp_ref[...] = (o * _rms_scale(o) * mlpg_ref[...]).astype(BF16)


def _proj(x2d, mods3, g1, w_in, pool, qg, kg, rope_tabs, sgun, sguw, sgub, mlpg,
          *, seq_len, mods_row_fn):
    n_tok = x2d.shape[0]
    tm = TM_PROJ
    rope = rope_tabs is not None
    blocks_per_seq = seq_len // tm if rope else None

    in_specs = [
        pl.BlockSpec((tm, D_MODEL), lambda i: (i, 0)),
        pl.BlockSpec((1, 6, D_MODEL), lambda i: (mods_row_fn(i), 0, 0)),
        _const_spec((1, D_MODEL)),
        _const_spec((D_MODEL, IN_WIDTH)),
        _const_spec((ATT_WIDTH, ATT_WIDTH)),
        _const_spec((1, ATT_WIDTH)),
        _const_spec((1, ATT_WIDTH)),
    ]
    args = [x2d, mods3, g1, w_in, pool, qg, kg]
    if rope:
        tab_spec = pl.BlockSpec((tm, LANES), lambda i: (i % blocks_per_seq, 0))
        in_specs += [tab_spec] * 2
        args += list(rope_tabs)
    in_specs += [
        _const_spec((1, MLP_WIDTH)),
        _const_spec((N_GROUPS, CHUNK, CHUNK)),
        _const_spec((N_GROUPS, CHUNK, GROUP_DIM)),
        _const_spec((1, MLP_WIDTH)),
    ]
    args += [sgun, sguw, sgub, mlpg]

    head_spec = pl.BlockSpec((N_HEADS, tm, LANES), lambda i: (0, i, 0))
    head_shape = jax.ShapeDtypeStruct((N_HEADS, n_tok, LANES), BF16)
    out_specs = [head_spec, head_spec,
                 pl.BlockSpec((N_HEADS, VAL_DIM, tm), lambda i: (0, 0, i)),
                 pl.BlockSpec((tm, MLP_WIDTH), lambda i: (i, 0))]
    out_shape = [head_shape, head_shape,
                 jax.ShapeDtypeStruct((N_HEADS, VAL_DIM, n_tok), BF16),
                 jax.ShapeDtypeStruct((n_tok, MLP_WIDTH), BF16)]
    if not rope:
        n_seq = n_tok // seq_len
        spb = tm // seq_len
        out_specs += [
            pl.BlockSpec((spb, 1, N_HEADS, 2, HEAD_DIM, seq_len), lambda i: (i, 0, 0, 0, 0, 0)),
            pl.BlockSpec((spb, 1, N_HEADS, seq_len, VAL_DIM), lambda i: (i, 0, 0, 0, 0)),
        ]
        out_shape += [
            jax.ShapeDtypeStruct((n_seq, 1, N_HEADS, 2, HEAD_DIM, seq_len), F32),
            jax.ShapeDtypeStruct((n_seq, 1, N_HEADS, seq_len, VAL_DIM), F32),
        ]

    return pl.pallas_call(
        functools.partial(_proj_kernel, rope=rope, seq_len=seq_len, tm=tm),
        grid=(n_tok // tm,),
        in_specs=in_specs,
        out_specs=out_specs,
        out_shape=out_shape,
        scratch_shapes=[pltpu.VMEM((tm, MLP_WIDTH), F32)],
        compiler_params=_params(1),
        name="proj_rope" if rope else "proj_ctx",
    )(*args)


def _lambda_full(lq1, lk1, lq2, lk2, lambda_init):
    return (jnp.exp(jnp.sum(lq1[...] * lk1[...], keepdims=True))
            - jnp.exp(jnp.sum(lq2[...] * lk2[...], keepdims=True))
            + lambda_init)


def _map_queries(qc):
    lane = lax.broadcasted_iota(jnp.int32, qc.shape, 1)
    zero = jnp.zeros_like(qc)
    return (jnp.where(lane < HEAD_DIM, qc, zero), jnp.where(lane >= HEAD_DIM, qc, zero))


def _combine_maps(o1, d1, o2, d2, lam, out_gain):
    ot = o1 * (1.0 / d1) - o2 * (lam / d2)
    ot = ot * lax.rsqrt(jnp.mean(ot * ot, axis=0, keepdims=True) + EPS)
    return (ot.T * out_gain).astype(BF16)


def _attn_ctx_kernel(lq1, lk1, lq2, lk2, q_ref, k_ref, vt_ref, ag_ref, o_ref, st_buf,
                     *, lambda_init, seq_len):
    lam = _lambda_full(lq1, lk1, lq2, lk2, lambda_init)
    units = [(slice(s * seq_len, (s + 1) * seq_len), h)
             for s in range(q_ref.shape[1] // seq_len) for h in range(N_HEADS)]

    def scores(u):
        rows, h = units[u]
        kk = k_ref[h, rows, :]
        maxes = []
        for mp, qm in enumerate(_map_queries(q_ref[h, rows, :])):
            st = lax.dot_general(kk, qm, NT_DIMS, preferred_element_type=F32)
            st_buf[u % CTX_AHEAD, mp] = st
            maxes.append(jnp.max(st, axis=0, keepdims=True))
        return maxes

    def finish(u, maxes):
        rows, h = units[u]
        vt = vt_ref[h, :, rows]
        outs, dens = [], []
        for mp in range(2):
            e = jnp.exp2(st_buf[u % CTX_AHEAD, mp] - maxes[mp])
            dens.append(jnp.sum(e, axis=0, keepdims=True))
            outs.append(jnp.dot(vt, e.astype(BF16), preferred_element_type=F32))
        out_gain = (1.0 - lambda_init) * ag_ref[h]
        o_ref[h, rows, :] = _combine_maps(outs[0], dens[0], outs[1], dens[1], lam, out_gain)

    pending = [scores(u) for u in range(CTX_AHEAD - 1)]
    for u in range(len(units)):
        if u + CTX_AHEAD - 1 < len(units):
            pending.append(scores(u + CTX_AHEAD - 1))
        finish(u, pending.pop(0))


def _attn_ctx(lams, q, k, vt, att_g, *, seq_len, lambda_init):
    n_tok = q.shape[1]
    tm = TM_ATTN_CTX
    head_spec = pl.BlockSpec((N_HEADS, tm, LANES), lambda i: (0, i, 0))
    return pl.pallas_call(
        functools.partial(_attn_ctx_kernel, lambda_init=lambda_init, seq_len=seq_len),
        grid=(n_tok // tm,),
        in_specs=[_const_spec((1, HEAD_DIM))] * 4 + [
            head_spec,
            head_spec,
            pl.BlockSpec((N_HEADS, VAL_DIM, tm), lambda i: (0, 0, i)),
            _const_spec((N_HEADS, 1, LANES)),
        ],
        out_specs=head_spec,
        out_shape=jax.ShapeDtypeStruct((N_HEADS, n_tok, LANES), BF16),
        scratch_shapes=[pltpu.VMEM((CTX_AHEAD, 2, seq_len, seq_len), F32)],
        compiler_params=_params(1),
        name="attn_ctx",
    )(*lams, q, k, vt, att_g)


def _attn_cache_kernel(lq1, lk1, lq2, lk2, q_ref, k_ref, vt_ref, kct_ref, vc_ref, ag_ref, o_ref,
                       k_all, vt_all, m_buf, acc_buf, *bufs, lambda_init, n_new):
    st = (bufs[0:2], bufs[2:4])
    n_chunks = n_new // TQ_UNIT
    n_units = N_HEADS * n_chunks
    n_keys = k_all.shape[1]

    past = n_keys - n_new
    for h in range(N_HEADS):
        k_all[h, 0:n_new, :] = k_ref[h]
        k_all[h, n_new:, :] = kct_ref[0, 0, h].reshape(2 * HEAD_DIM, past).T.astype(BF16)
        vt_all[h, 0:VAL_DIM, 0:n_new] = vt_ref[h]
        vt_all[h, 0:VAL_DIM, n_new:] = vc_ref[0, 0, h].T.astype(BF16)
        vt_all[h, VAL_DIM:, :] = jnp.ones((ONES_ROWS, n_keys), BF16)

    lam = _lambda_full(lq1, lk1, lq2, lk2, lambda_init)

    def head_rows(u):
        c = u % n_chunks
        return u // n_chunks, pl.ds(pl.multiple_of(c * TQ_UNIT, TQ_UNIT), TQ_UNIT)

    def stage(fin, sc, defer_out=False):
        if sc is not None:
            sc_head, sc_rows = head_rows(sc[0])
            qms = _map_queries(q_ref[sc_head, sc_rows, :])
            mrun = [None, None]
        if fin is not None:
            fin_head = fin[0] // n_chunks
            ms = [m_buf[fin[1], mp] for mp in range(2)]
            accs = [None, None]
        n_kb = n_keys // KEY_BLOCK
        lead = SCORE_LEAD if (sc is not None and fin is not None) else 0
        for step in range(n_kb + lead):
            if sc is not None and step < n_kb:
                kr = slice(step * KEY_BLOCK, (step + 1) * KEY_BLOCK)
                kk = k_all[sc_head, kr, :]
                for mp in range(2):
                    s = lax.dot_general(kk, qms[mp], NT_DIMS, preferred_element_type=F32)
                    st[sc[1]][mp][kr, :] = s
                    smax = jnp.max(s.reshape(KEY_BLOCK // 8, 8, TQ_UNIT), axis=0)
                    mrun[mp] = smax if mrun[mp] is None else jnp.maximum(mrun[mp], smax)
            if fin is not None and step >= lead:
                kr = slice((step - lead) * KEY_BLOCK, (step - lead + 1) * KEY_BLOCK)
                vt = vt_all[fin_head, :, kr]
                for mp in range(2):
                    p = jnp.exp2(st[fin[1]][mp][kr, :] - ms[mp]).astype(BF16)
                    d = jnp.dot(vt, p, preferred_element_type=F32)
                    accs[mp] = d if accs[mp] is None else accs[mp] + d
        if sc is not None:
            for mp in range(2):
                m_buf[sc[1], mp] = jnp.max(mrun[mp], axis=0, keepdims=True)
        if fin is not None:
            if defer_out:
                for mp in range(2):
                    acc_buf[mp] = accs[mp]
            else:
                write_out(fin[0], accs[0], accs[1])

    def write_out(u, o1, o2):
        head, rows = head_rows(u)
        out_gain = (1.0 - lambda_init) * ag_ref[head]
        o_ref[head, rows, :] = _combine_maps(
            o1[0:VAL_DIM, :], o1[VAL_DIM:VAL_DIM + 1, :],
            o2[0:VAL_DIM, :], o2[VAL_DIM:VAL_DIM + 1, :], lam, out_gain)

    stage(None, (0, 0))
    stage((0, 0), (1, 1), defer_out=True)

    def pair(i, carry):
        u = 2 * i
        write_out(u - 2, acc_buf[0], acc_buf[1])
        stage((u - 1, 1), (u, 0))
        stage((u, 0), (u + 1, 1), defer_out=True)
        return carry

    lax.fori_loop(1, n_units // 2, pair, 0)
    write_out(n_units - 2, acc_buf[0], acc_buf[1])
    stage((n_units - 1, 1), None)


def _attn_cache(lams, q, k, vt, kct, vc, att_g, *, layer, n_batch, seq_len, lambda_init):
    past = vc.shape[3]
    n_keys = seq_len + past
    assert (seq_len // TQ_UNIT) % 2 == 0 and seq_len // TQ_UNIT >= 4
    assert n_keys % KEY_BLOCK == 0
    head_spec = pl.BlockSpec((N_HEADS, seq_len, LANES), lambda b: (0, b, 0))
    in_specs = [_const_spec((1, HEAD_DIM))] * 4 + [
        head_spec,
        head_spec,
        pl.BlockSpec((N_HEADS, VAL_DIM, seq_len), lambda b: (0, 0, b)),
        pl.BlockSpec((1, 1, N_HEADS, 2, HEAD_DIM, past), lambda b: (b, layer, 0, 0, 0, 0)),
        pl.BlockSpec((1, 1, N_HEADS, past, VAL_DIM), lambda b: (b, layer, 0, 0, 0)),
        _const_spec((N_HEADS, 1, LANES)),
    ]
    return pl.pallas_call(
        functools.partial(_attn_cache_kernel, lambda_init=lambda_init, n_new=seq_len),
        grid=(n_batch,),
        in_specs=in_specs,
        out_specs=head_spec,
        out_shape=jax.ShapeDtypeStruct((N_HEADS, n_batch * seq_len, LANES), BF16),
        scratch_shapes=([pltpu.VMEM((N_HEADS, n_keys, LANES), BF16),
                         pltpu.VMEM((N_HEADS, VAL_DIM + ONES_ROWS, n_keys), BF16),
                         pltpu.VMEM((2, 2, 1, TQ_UNIT), F32),
                         pltpu.VMEM((2, VAL_DIM + ONES_ROWS, TQ_UNIT), F32)]
                        + [pltpu.VMEM((n_keys, TQ_UNIT), F32)] * 4),
        compiler_params=_params(1),
        name="attn_cache",
    )(*lams, q, k, vt, kct, vc, att_g)


def _ffn_kernel(x_ref, att_ref, mlp_ref, mods_ref, g2_ref, wo_ref, wfi_ref, wfo_ref, o_ref):
    att = jnp.concatenate([att_ref[h] for h in range(N_HEADS)], axis=1)
    y = (jnp.dot(att, wo_ref[0:ATT_WIDTH, :], preferred_element_type=F32)
         + jnp.dot(mlp_ref[...], wo_ref[ATT_WIDTH:, :], preferred_element_type=F32))
    x1 = x_ref[...] + mods_ref[0, 2:3, :] * y
    xn = x1 * _rms_scale(x1) * g2_ref[...]
    xb = (xn * (1.0 + mods_ref[0, 4:5, :]) + mods_ref[0, 3:4, :]).astype(BF16)
    acc = None
    for c0, cw in FF_CHUNKS:
        gte = jnp.dot(xb, wfi_ref[:, c0:c0 + cw], preferred_element_type=F32)
        up = jnp.dot(xb, wfi_ref[:, D_FF + c0:D_FF + c0 + cw], preferred_element_type=F32)
        act = (gte * jax.nn.sigmoid(gte) * up).astype(BF16)
        part = jnp.dot(act, wfo_ref[c0:c0 + cw, :], preferred_element_type=F32)
        acc = part if acc is None else acc + part
    o_ref[...] = x1 + mods_ref[0, 5:6, :] * acc


def _ffn(x2d, att, mlp, mods3, g2, w_out, w_ffn_in, w_ffn_out, *, mods_row_fn):
    n_tok = x2d.shape[0]
    tm = TM_FFN
    return pl.pallas_call(
        _ffn_kernel,
        grid=(n_tok // tm,),
        in_specs=[
            pl.BlockSpec((tm, D_MODEL), lambda i: (i, 0)),
            pl.BlockSpec((N_HEADS, tm, LANES), lambda i: (0, i, 0)),
            pl.BlockSpec((tm, MLP_WIDTH), lambda i: (i, 0)),
            pl.BlockSpec((1, 6, D_MODEL), lambda i: (mods_row_fn(i), 0, 0)),
            _const_spec((1, D_MODEL)),
            _const_spec((D_MODEL, D_MODEL)),
            _const_spec((D_MODEL, 2 * D_FF)),
            _const_spec((D_FF, D_MODEL)),
        ],
        out_specs=pl.BlockSpec((tm, D_MODEL), lambda i: (i, 0)),
        out_shape=jax.ShapeDtypeStruct((n_tok, D_MODEL), F32),
        compiler_params=_params(1),
        name="ffn",
    )(x2d, att, mlp, mods3, g2, w_out, w_ffn_in, w_ffn_out)


def _rope_tables(n):
    pos = np.arange(n)
    row = (pos // GRID_W).astype(np.float32)
    col = (pos % GRID_W).astype(np.float32)
    inv = (ROPE_THETA ** (-np.arange(0, ROPE_AXIS_DIM, 2, dtype=np.float32) / ROPE_AXIS_DIM)
           ).astype(np.float32)
    ang_r = row[:, None] * inv[None, :]
    ang_c = col[:, None] * inv[None, :]
    cos64 = np.concatenate([np.cos(ang_r)] * 2 + [np.cos(ang_c)] * 2, axis=1)
    sin64 = np.concatenate([-np.sin(ang_r), np.sin(ang_r), -np.sin(ang_c), np.sin(ang_c)], axis=1)
    return (np.tile(cos64, (1, 2)).astype(np.float32), np.tile(sin64, (1, 2)).astype(np.float32))


def kernel(x_prompt, x_sample, cache_k_ctx, cache_v_ctx, c, c_ctx, norm1_g, norm2_g, w_ada, b_ada, w_in, q_norm_g, k_norm_g, lambda_q1, lambda_k1, lambda_q2, lambda_k2, att_out_g, sgu_norm_g, sgu_w, sgu_b, mlp_out_g, w_out, w_ffn_in, w_ffn_out):
    n_ctx, ctx_len, _ = x_prompt.shape
    n_dec, dec_len, _ = x_sample.shape
    depth = norm1_g.shape[0]

    rope_tabs = _rope_tables(dec_len)
    q_scale = LOG2E / math.sqrt(HEAD_DIM)
    group = np.arange(ATT_WIDTH) // HEAD_DIM
    pool = jnp.asarray(np.where(group[:, None] == group[None, :], 1.0 / HEAD_DIM, 0.0), BF16)

    cond = jnp.concatenate(
        [c, c_ctx[None, :], jnp.zeros((MODS_ROWS - n_dec - 1, D_MODEL), F32)], axis=0)

    xp = x_prompt.reshape(n_ctx * ctx_len, D_MODEL)
    xs = x_sample.reshape(n_dec * dec_len, D_MODEL)
    cache_kt = jnp.swapaxes(cache_k_ctx, -1, -2)
    k_states, v_states = [], []
    ctx_row = lambda i: CTX_ROW
    dec_row_proj = lambda i: i // (dec_len // TM_PROJ)
    dec_row_ffn = lambda i: i // (dec_len // TM_FFN)

    for l in range(depth):
        lambda_init = 0.8 - 0.6 * math.exp(-0.3 * l)
        mods3 = _mods(cond, w_ada[l], b_ada[l][None, :]).reshape(MODS_ROWS, 6, D_MODEL)
        g1 = norm1_g[l][None, :]
        g2 = norm2_g[l][None, :]
        w_in_b = w_in[l].astype(BF16)
        w_out_b = w_out[l].astype(BF16)
        w_fi_b = w_ffn_in[l].astype(BF16)
        w_fo_b = w_ffn_out[l].astype(BF16)
        qg = jnp.tile(q_norm_g[l] * q_scale, ATT_WIDTH // HEAD_DIM)[None, :]
        kg = jnp.tile(k_norm_g[l], ATT_WIDTH // HEAD_DIM)[None, :]
        sgun = sgu_norm_g[l][None, :]
        sguw = sgu_w[l].astype(BF16)
        sgub = jnp.broadcast_to(sgu_b[l][:, :, None], (N_GROUPS, CHUNK, GROUP_DIM))
        mlpg = mlp_out_g[l][None, :]
        att_g = att_out_g[l].reshape(N_HEADS, 1, VAL_DIM)
        lams = (lambda_q1[l][None, :], lambda_k1[l][None, :],
                lambda_q2[l][None, :], lambda_k2[l][None, :])

        q, k, vt, mlp, k_c, v_c = _proj(
            xp, mods3, g1, w_in_b, pool, qg, kg, None, sgun, sguw, sgub, mlpg,
            seq_len=ctx_len, mods_row_fn=ctx_row)
        att = _attn_ctx(lams, q, k, vt, att_g, seq_len=ctx_len, lambda_init=lambda_init)
        xp = _ffn(xp, att, mlp, mods3, g2, w_out_b, w_fi_b, w_fo_b, mods_row_fn=ctx_row)
        k_states.append(k_c)
        v_states.append(v_c)

        q, k, vt, mlp = _proj(
            xs, mods3, g1, w_in_b, pool, qg, kg, rope_tabs, sgun, sguw, sgub, mlpg,
            seq_len=dec_len, mods_row_fn=dec_row_proj)
        att = _attn_cache(lams, q, k, vt, cache_kt, cache_v_ctx, att_g, layer=l,
                          n_batch=n_dec, seq_len=dec_len, lambda_init=lambda_init)
        xs = _ffn(xs, att, mlp, mods3, g2, w_out_b, w_fi_b, w_fo_b, mods_row_fn=dec_row_ffn)

    state_k = jnp.swapaxes(jnp.concatenate(k_states, axis=1), -1, -2)
    state_v = jnp.concatenate(v_states, axis=1)
    return (xp.reshape(n_ctx, ctx_len, D_MODEL), xs.reshape(n_dec, dec_len, D_MODEL),
            state_k, state_v)
```

```python
import functools
import math

import jax
import jax.numpy as jnp
import numpy as np
from jax import lax
from jax.experimental import pallas as pl
from jax.experimental.pallas import tpu as pltpu

D_MODEL = 1024
ATT_WIDTH = 512
N_HEADS = 4
HEAD_DIM = 64
VAL_DIM = 128
MLP_WIDTH = 512
N_GROUPS = 4
GROUP_DIM = 128
CHUNK = 128
D_FF = 2816
IN_WIDTH = 2560
GRID_W = 64
ROPE_THETA = 10000.0
ROPE_AXIS_DIM = 32
EPS = 1e-6
LOG2E = 1.4426950408889634
LANES = 128

F32 = jnp.float32
BF16 = jnp.bfloat16

VMEM_LIMIT_BYTES = 56 * 1024 * 1024
MODS_ROWS = 16
CTX_ROW = 8

TM_PROJ = 512
TM_FFN = 512
TM_ATTN_CTX = 512
CTX_AHEAD = 3
FF_CHUNKS = ((0, 1024), (1024, 1024), (2048, 768))

ONES_ROWS = 16
TQ_UNIT = 256
KEY_BLOCK = 256
SCORE_LEAD = 3


def _const_spec(shape):
    zeros = (0,) * len(shape)
    return pl.BlockSpec(shape, lambda *_: zeros, pipeline_mode=pl.Buffered(1))


def _params(n_grid):
    return pltpu.CompilerParams(
        dimension_semantics=("arbitrary",) * n_grid,
        vmem_limit_bytes=VMEM_LIMIT_BYTES,
    )


def _rms_scale(x):
    return lax.rsqrt(jnp.mean(x * x, axis=-1, keepdims=True) + EPS)


def _mods_kernel(cond_ref, w_ref, b_ref, o_ref):
    cnd = cond_ref[...]
    act = (cnd * jax.nn.sigmoid(cnd)).astype(BF16)
    o_ref[...] = jnp.dot(act, w_ref[...].astype(BF16), preferred_element_type=F32) + b_ref[...]


def _mods(cond, w_ada, b_ada):
    tn = 1536
    n_out = w_ada.shape[1]
    return pl.pallas_call(
        _mods_kernel,
        grid=(n_out // tn,),
        in_specs=[
            _const_spec((MODS_ROWS, D_MODEL)),
            pl.BlockSpec((D_MODEL, tn), lambda j: (0, j)),
            pl.BlockSpec((1, tn), lambda j: (0, j)),
        ],
        out_specs=pl.BlockSpec((MODS_ROWS, tn), lambda j: (0, j)),
        out_shape=jax.ShapeDtypeStruct((MODS_ROWS, n_out), F32),
        compiler_params=_params(1),
        name="mods",
    )(cond, w_ada, b_ada)


def _proj_kernel(*refs, rope, seq_len, tm):
    it = iter(refs)
    x_ref, mods_ref, g1_ref, w_in_ref, pool_ref, qg_ref, kg_ref = (next(it) for _ in range(7))
    if rope:
        cos_ref, sin_ref = (next(it) for _ in range(2))
    sgun_ref, sguw_ref, sgub_ref, mlpg_ref = (next(it) for _ in range(4))
    qt_ref, k_ref, vt_ref, mlp_ref = (next(it) for _ in range(4))
    if not rope:
        kst_ref, vst_ref = (next(it) for _ in range(2))
    gate_ref = next(it)

    x = x_ref[...]
    xn = x * _rms_scale(x) * g1_ref[...]
    xm = xn * (1.0 + mods_ref[0, 1:2, :]) + mods_ref[0, 0:1, :]
    xb = xm.astype(BF16)

    def section(lo, hi):
        return jnp.dot(xb, w_in_ref[:, lo:hi], preferred_element_type=F32)

    def head_norm(h, g_ref):
        sq = (h * h).astype(BF16)
        msq = jnp.dot(sq, pool_ref[...], preferred_element_type=F32)
        return h * lax.rsqrt(msq + EPS) * g_ref[...]

    if rope:
        lane = lax.broadcasted_iota(jnp.int32, (tm, LANES), 1)
        first_half = (lane % 32) < 16

    def emit_heads(hn, out_ref, transpose):
        for h in range(N_HEADS):
            hc = hn[:, h * LANES:(h + 1) * LANES]
            if rope:
                swapped = jnp.where(first_half,
                                    pltpu.roll(hc, LANES - 16, 1),
                                    pltpu.roll(hc, 16, 1))
                hc = hc * cos_ref[...] + swapped * sin_ref[...]
            out_ref[h] = (hc.T if transpose else hc).astype(BF16)

    qn = head_norm(section(0, ATT_WIDTH), qg_ref)
    kn = head_norm(section(ATT_WIDTH, 2 * ATT_WIDTH), kg_ref)
    emit_heads(qn, qt_ref, transpose=True)
    emit_heads(kn, k_ref, transpose=False)

    hv = section(2 * ATT_WIDTH, 3 * ATT_WIDTH)
    hvt = hv.T.astype(BF16)
    for h in range(N_HEADS):
        vt_ref[h] = hvt[h * VAL_DIM:(h + 1) * VAL_DIM, :]
    if not rope:
        knt = kn.T
        for s in range(tm // seq_len):
            rows = slice(s * seq_len, (s + 1) * seq_len)
            for h in range(N_HEADS):
                for i in range(2):
                    j = 2 * h + i
                    kst_ref[s, 0, h, i, :, :] = knt[j * HEAD_DIM:(j + 1) * HEAD_DIM, rows]
                vst_ref[s, 0, h, :, :] = hv[rows, h * VAL_DIM:(h + 1) * VAL_DIM]

    hu = section(3 * ATT_WIDTH, 3 * ATT_WIDTH + MLP_WIDTH)
    hg = section(3 * ATT_WIDTH + MLP_WIDTH, IN_WIDTH)
    for g in range(N_GROUPS):
        cols = slice(g * GROUP_DIM, (g + 1) * GROUP_DIM)
        gg = hg[:, cols]
        gn = (gg * _rms_scale(gg) * sgun_ref[:, cols]).astype(BF16)
        ug = hu[:, cols]
        wg = sguw_ref[g]
        bg = sgub_ref[g]
        for n in range(tm // CHUNK):
            rows = slice(n * CHUNK, (n + 1) * CHUNK)
            sp = jnp.dot(wg, gn[rows, :], preferred_element_type=F32) + bg
            gate_ref[rows, cols] = ug[rows, :] * sp
    o = gate_ref[...]
    mlp_ref[...] = (o * _rms_scale(o) * mlpg_ref[...]).astype(BF16)


def _proj(x2d, mods3, g1, w_in, pool, qg, kg, rope_tabs, sgun, sguw, sgub, mlpg,
          *, seq_len, mods_row_fn):
    n_tok = x2d.shape[0]
    tm = TM_PROJ
    rope = rope_tabs is not None
    blocks_per_seq = seq_len // tm if rope else None

    in_specs = [
        pl.BlockSpec((tm, D_MODEL), lambda i: (i, 0)),
        pl.BlockSpec((1, 6, D_MODEL), lambda i: (mods_row_fn(i), 0, 0)),
        _const_spec((1, D_MODEL)),
        _const_spec((D_MODEL, IN_WIDTH)),
        _const_spec((ATT_WIDTH, ATT_WIDTH)),
        _const_spec((1, ATT_WIDTH)),
        _const_spec((1, ATT_WIDTH)),
    ]
    args = [x2d, mods3, g1, w_in, pool, qg, kg]
    if rope:
        tab_spec = pl.BlockSpec((tm, LANES), lambda i: (i % blocks_per_seq, 0))
        in_specs += [tab_spec] * 2
        args += list(rope_tabs)
    in_specs += [
        _const_spec((1, MLP_WIDTH)),
        _const_spec((N_GROUPS, CHUNK, CHUNK)),
        _const_spec((N_GROUPS, CHUNK, GROUP_DIM)),
        _const_spec((1, MLP_WIDTH)),
    ]
    args += [sgun, sguw, sgub, mlpg]

    head_spec = pl.BlockSpec((N_HEADS, tm, LANES), lambda i: (0, i, 0))
    head_t_spec = pl.BlockSpec((N_HEADS, LANES, tm), lambda i: (0, 0, i))
    head_shape = jax.ShapeDtypeStruct((N_HEADS, n_tok, LANES), BF16)
    head_t_shape = jax.ShapeDtypeStruct((N_HEADS, LANES, n_tok), BF16)
    out_specs = [head_t_spec, head_spec, head_t_spec,
                 pl.BlockSpec((tm, MLP_WIDTH), lambda i: (i, 0))]
    out_shape = [head_t_shape, head_shape, head_t_shape,
                 jax.ShapeDtypeStruct((n_tok, MLP_WIDTH), BF16)]
    if not rope:
        n_seq = n_tok // seq_len
        spb = tm // seq_len
        out_specs += [
            pl.BlockSpec((spb, 1, N_HEADS, 2, HEAD_DIM, seq_len), lambda i: (i, 0, 0, 0, 0, 0)),
            pl.BlockSpec((spb, 1, N_HEADS, seq_len, VAL_DIM), lambda i: (i, 0, 0, 0, 0)),
        ]
        out_shape += [
            jax.ShapeDtypeStruct((n_seq, 1, N_HEADS, 2, HEAD_DIM, seq_len), F32),
            jax.ShapeDtypeStruct((n_seq, 1, N_HEADS, seq_len, VAL_DIM), F32),
        ]

    return pl.pallas_call(
        functools.partial(_proj_kernel, rope=rope, seq_len=seq_len, tm=tm),
        grid=(n_tok // tm,),
        in_specs=in_specs,
        out_specs=out_specs,
        out_shape=out_shape,
        scratch_shapes=[pltpu.VMEM((tm, MLP_WIDTH), F32)],
        compiler_params=_params(1),
        name="proj_rope" if rope else "proj_ctx",
    )(*args)


def _lambda_full(lq1, lk1, lq2, lk2, lambda_init):
    return (jnp.exp(jnp.sum(lq1[...] * lk1[...], keepdims=True))
            - jnp.exp(jnp.sum(lq2[...] * lk2[...], keepdims=True))
            + lambda_init)


def _map_queries(qt):
    row = lax.broadcasted_iota(jnp.int32, qt.shape, 0)
    zero = jnp.zeros_like(qt)
    return (jnp.where(row < HEAD_DIM, qt, zero), jnp.where(row >= HEAD_DIM, qt, zero))


def _combine_maps(o1, d1, o2, d2, lam, out_gain):
    ot = o1 * (1.0 / d1) - o2 * (lam / d2)
    ot = ot * lax.rsqrt(jnp.mean(ot * ot, axis=0, keepdims=True) + EPS)
    return (ot.T * out_gain).astype(BF16)


def _attn_ctx_kernel(lq1, lk1, lq2, lk2, qt_ref, k_ref, vt_ref, ag_ref, o_ref, st_buf,
                     *, lambda_init, seq_len):
    lam = _lambda_full(lq1, lk1, lq2, lk2, lambda_init)
    units = [(slice(s * seq_len, (s + 1) * seq_len), h)
             for s in range(k_ref.shape[1] // seq_len) for h in range(N_HEADS)]

    def scores(u):
        rows, h = units[u]
        kk = k_ref[h, rows, :]
        maxes = []
        for mp, qm in enumerate(_map_queries(qt_ref[h, :, rows])):
            st = jnp.dot(kk, qm, preferred_element_type=F32)
            st_buf[u % CTX_AHEAD, mp] = st
            maxes.append(jnp.max(st, axis=0, keepdims=True))
        return maxes

    def finish(u, maxes):
        rows, h = units[u]
        vt = vt_ref[h, :, rows]
        outs, dens = [], []
        for mp in range(2):
            e = jnp.exp2(st_buf[u % CTX_AHEAD, mp] - maxes[mp])
            dens.append(jnp.sum(e, axis=0, keepdims=True))
            outs.append(jnp.dot(vt, e.astype(BF16), preferred_element_type=F32))
        out_gain = (1.0 - lambda_init) * ag_ref[h]
        o_ref[h, rows, :] = _combine_maps(outs[0], dens[0], outs[1], dens[1], lam, out_gain)

    pending = [scores(u) for u in range(CTX_AHEAD - 1)]
    for u in range(len(units)):
        if u + CTX_AHEAD - 1 < len(units):
            pending.append(scores(u + CTX_AHEAD - 1))
        finish(u, pending.pop(0))


def _attn_ctx(lams, q, k, vt, att_g, *, seq_len, lambda_init):
    n_tok = k.shape[1]
    tm = TM_ATTN_CTX
    head_spec = pl.BlockSpec((N_HEADS, tm, LANES), lambda i: (0, i, 0))
    head_t_spec = pl.BlockSpec((N_HEADS, LANES, tm), lambda i: (0, 0, i))
    return pl.pallas_call(
        functools.partial(_attn_ctx_kernel, lambda_init=lambda_init, seq_len=seq_len),
        grid=(n_tok // tm,),
        in_specs=[_const_spec((1, HEAD_DIM))] * 4 + [
            head_t_spec,
            head_spec,
            head_t_spec,
            _const_spec((N_HEADS, 1, LANES)),
        ],
        out_specs=head_spec,
        out_shape=jax.ShapeDtypeStruct((N_HEADS, n_tok, LANES), BF16),
        scratch_shapes=[pltpu.VMEM((CTX_AHEAD, 2, seq_len, seq_len), F32)],
        compiler_params=_params(1),
        name="attn_ctx",
    )(*lams, q, k, vt, att_g)


def _attn_cache_kernel(lq1, lk1, lq2, lk2, qt_ref, k_ref, vt_ref, kct_ref, vc_ref, ag_ref, o_ref,
                       k_all, vt_all, m_buf, acc_buf, *bufs, lambda_init, n_new):
    st = (bufs[0:2], bufs[2:4])
    n_chunks = n_new // TQ_UNIT
    n_units = N_HEADS * n_chunks
    n_keys = k_all.shape[1]

    past = n_keys - n_new
    for h in range(N_HEADS):
        k_all[h, 0:n_new, :] = k_ref[h]
        k_all[h, n_new:, :] = kct_ref[0, 0, h].reshape(2 * HEAD_DIM, past).T.astype(BF16)
        vt_all[h, 0:VAL_DIM, 0:n_new] = vt_ref[h]
        vt_all[h, 0:VAL_DIM, n_new:] = vc_ref[0, 0, h].T.astype(BF16)
        vt_all[h, VAL_DIM:, :] = jnp.ones((ONES_ROWS, n_keys), BF16)

    lam = _lambda_full(lq1, lk1, lq2, lk2, lambda_init)

    def head_rows(u):
        c = u % n_chunks
        return u // n_chunks, pl.ds(pl.multiple_of(c * TQ_UNIT, TQ_UNIT), TQ_UNIT)

    def stage(fin, sc, defer_out=False):
        if sc is not None:
            sc_head, sc_rows = head_rows(sc[0])
            qms = _map_queries(qt_ref[sc_head, :, sc_rows])
            mrun = [None, None]
        if fin is not None:
            fin_head = fin[0] // n_chunks
            ms = [m_buf[fin[1], mp] for mp in range(2)]
            accs = [None, None]
        n_kb = n_keys // KEY_BLOCK
        lead = SCORE_LEAD if (sc is not None and fin is not None) else 0
        for step in range(n_kb + lead):
            if sc is not None and step < n_kb:
                kr = slice(step * KEY_BLOCK, (step + 1) * KEY_BLOCK)
                kk = k_all[sc_head, kr, :]
                for mp in range(2):
                    s = jnp.dot(kk, qms[mp], preferred_element_type=F32)
                    st[sc[1]][mp][kr, :] = s
                    smax = jnp.max(s.reshape(KEY_BLOCK // 8, 8, TQ_UNIT), axis=0)
                    mrun[mp] = smax if mrun[mp] is None else jnp.maximum(mrun[mp], smax)
            if fin is not None and step >= lead:
                kr = slice((step - lead) * KEY_BLOCK, (step - lead + 1) * KEY_BLOCK)
                vt = vt_all[fin_head, :, kr]
                for mp in range(2):
                    p = jnp.exp2(st[fin[1]][mp][kr, :] - ms[mp]).astype(BF16)
                    d = jnp.dot(vt, p, preferred_element_type=F32)
                    accs[mp] = d if accs[mp] is None else accs[mp] + d
        if sc is not None:
            for mp in range(2):
                m_buf[sc[1], mp] = jnp.max(mrun[mp], axis=0, keepdims=True)
        if fin is not None:
            if defer_out:
                for mp in range(2):
                    acc_buf[mp] = accs[mp]
            else:
                write_out(fin[0], accs[0], accs[1])

    def write_out(u, o1, o2):
        head, rows = head_rows(u)
        out_gain = (1.0 - lambda_init) * ag_ref[head]
        o_ref[head, rows, :] = _combine_maps(
            o1[0:VAL_DIM, :], o1[VAL_DIM:VAL_DIM + 1, :],
            o2[0:VAL_DIM, :], o2[VAL_DIM:VAL_DIM + 1, :], lam, out_gain)

    stage(None, (0, 0))
    stage((0, 0), (1, 1), defer_out=True)

    def pair(i, carry):
        u = 2 * i
        write_out(u - 2, acc_buf[0], acc_buf[1])
        stage((u - 1, 1), (u, 0))
        stage((u, 0), (u + 1, 1), defer_out=True)
        return carry

    lax.fori_loop(1, n_units // 2, pair, 0)
    write_out(n_units - 2, acc_buf[0], acc_buf[1])
    stage((n_units - 1, 1), None)


def _attn_cache(lams, q, k, vt, kct, vc, att_g, *, layer, n_batch, seq_len, lambda_init):
    past = vc.shape[3]
    n_keys = seq_len + past
    assert (seq_len // TQ_UNIT) % 2 == 0 and seq_len // TQ_UNIT >= 4
    assert n_keys % KEY_BLOCK == 0
    head_spec = pl.BlockSpec((N_HEADS, seq_len, LANES), lambda b: (0, b, 0))
    head_t_spec = pl.BlockSpec((N_HEADS, LANES, seq_len), lambda b: (0, 0, b))
    in_specs = [_const_spec((1, HEAD_DIM))] * 4 + [
        head_t_spec,
        head_spec,
        head_t_spec,
        pl.BlockSpec((1, 1, N_HEADS, 2, HEAD_DIM, past), lambda b: (b, layer, 0, 0, 0, 0)),
        pl.BlockSpec((1, 1, N_HEADS, past, VAL_DIM), lambda b: (b, layer, 0, 0, 0)),
        _const_spec((N_HEADS, 1, LANES)),
    ]
    return pl.pallas_call(
        functools.partial(_attn_cache_kernel, lambda_init=lambda_init, n_new=seq_len),
        grid=(n_batch,),
        in_specs=in_specs,
        out_specs=head_spec,
        out_shape=jax.ShapeDtypeStruct((N_HEADS, n_batch * seq_len, LANES), BF16),
        scratch_shapes=([pltpu.VMEM((N_HEADS, n_keys, LANES), BF16),
                         pltpu.VMEM((N_HEADS, VAL_DIM + ONES_ROWS, n_keys), BF16),
                         pltpu.VMEM((2, 2, 1, TQ_UNIT), F32),
                         pltpu.VMEM((2, VAL_DIM + ONES_ROWS, TQ_UNIT), F32)]
                        + [pltpu.VMEM((n_keys, TQ_UNIT), F32)] * 4),
        compiler_params=_params(1),
        name="attn_cache",
    )(*lams, q, k, vt, kct, vc, att_g)


def _ffn_kernel(x_ref, att_ref, mlp_ref, mods_ref, g2_ref, wo_ref, wfi_ref, wfo_ref, o_ref):
    att = jnp.concatenate([att_ref[h] for h in range(N_HEADS)], axis=1)
    y = (jnp.dot(att, wo_ref[0:ATT_WIDTH, :], preferred_element_type=F32)
         + jnp.dot(mlp_ref[...], wo_ref[ATT_WIDTH:, :], preferred_element_type=F32))
    x1 = x_ref[...] + mods_ref[0, 2:3, :] * y
    xn = x1 * _rms_scale(x1) * g2_ref[...]
    xb = (xn * (1.0 + mods_ref[0, 4:5, :]) + mods_ref[0, 3:4, :]).astype(BF16)
    acc = None
    for c0, cw in FF_CHUNKS:
        gte = jnp.dot(xb, wfi_ref[:, c0:c0 + cw], preferred_element_type=F32)
        up = jnp.dot(xb, wfi_ref[:, D_FF + c0:D_FF + c0 + cw], preferred_element_type=F32)
        act = (gte * jax.nn.sigmoid(gte) * up).astype(BF16)
        part = jnp.dot(act, wfo_ref[c0:c0 + cw, :], preferred_element_type=F32)
        acc = part if acc is None else acc + part
    o_ref[...] = x1 + mods_ref[0, 5:6, :] * acc


def _ffn(x2d, att, mlp, mods3, g2, w_out, w_ffn_in, w_ffn_out, *, mods_row_fn):
    n_tok = x2d.shape[0]
    tm = TM_FFN
    return pl.pallas_call(
        _ffn_kernel,
        grid=(n_tok // tm,),
        in_specs=[
            pl.BlockSpec((tm, D_MODEL), lambda i: (i, 0)),
            pl.BlockSpec((N_HEADS, tm, LANES), lambda i: (0, i, 0)),
            pl.BlockSpec((tm, MLP_WIDTH), lambda i: (i, 0)),
            pl.BlockSpec((1, 6, D_MODEL), lambda i: (mods_row_fn(i), 0, 0)),
            _const_spec((1, D_MODEL)),
            _const_spec((D_MODEL, D_MODEL)),
            _const_spec((D_MODEL, 2 * D_FF)),
            _const_spec((D_FF, D_MODEL)),
        ],
        out_specs=pl.BlockSpec((tm, D_MODEL), lambda i: (i, 0)),
        out_shape=jax.ShapeDtypeStruct((n_tok, D_MODEL), F32),
        compiler_params=_params(1),
        name="ffn",
    )(x2d, att, mlp, mods3, g2, w_out, w_ffn_in, w_ffn_out)


def _rope_tables(n):
    pos = np.arange(n)
    row = (pos // GRID_W).astype(np.float32)
    col = (pos % GRID_W).astype(np.float32)
    inv = (ROPE_THETA ** (-np.arange(0, ROPE_AXIS_DIM, 2, dtype=np.float32) / ROPE_AXIS_DIM)
           ).astype(np.float32)
    ang_r = row[:, None] * inv[None, :]
    ang_c = col[:, None] * inv[None, :]
    cos64 = np.concatenate([np.cos(ang_r)] * 2 + [np.cos(ang_c)] * 2, axis=1)
    sin64 = np.concatenate([-np.sin(ang_r), np.sin(ang_r), -np.sin(ang_c), np.sin(ang_c)], axis=1)
    return (np.tile(cos64, (1, 2)).astype(np.float32), np.tile(sin64, (1, 2)).astype(np.float32))


def kernel(x_prompt, x_sample, cache_k_ctx, cache_v_ctx, c, c_ctx, norm1_g, norm2_g, w_ada, b_ada, w_in, q_norm_g, k_norm_g, lambda_q1, lambda_k1, lambda_q2, lambda_k2, att_out_g, sgu_norm_g, sgu_w, sgu_b, mlp_out_g, w_out, w_ffn_in, w_ffn_out):
    n_ctx, ctx_len, _ = x_prompt.shape
    n_dec, dec_len, _ = x_sample.shape
    depth = norm1_g.shape[0]

    rope_tabs = _rope_tables(dec_len)
    q_scale = LOG2E / math.sqrt(HEAD_DIM)
    group = np.arange(ATT_WIDTH) // HEAD_DIM
    pool = jnp.asarray(np.where(group[:, None] == group[None, :], 1.0 / HEAD_DIM, 0.0), BF16)

    cond = jnp.concatenate(
        [c, c_ctx[None, :], jnp.zeros((MODS_ROWS - n_dec - 1, D_MODEL), F32)], axis=0)

    xp = x_prompt.reshape(n_ctx * ctx_len, D_MODEL)
    xs = x_sample.reshape(n_dec * dec_len, D_MODEL)
    cache_kt = jnp.swapaxes(cache_k_ctx, -1, -2)
    k_states, v_states = [], []
    ctx_row = lambda i: CTX_ROW
    dec_row_proj = lambda i: i // (dec_len // TM_PROJ)
    dec_row_ffn = lambda i: i // (dec_len // TM_FFN)

    for l in range(depth):
        lambda_init = 0.8 - 0.6 * math.exp(-0.3 * l)
        mods3 = _mods(cond, w_ada[l], b_ada[l][None, :]).reshape(MODS_ROWS, 6, D_MODEL)
        g1 = norm1_g[l][None, :]
        g2 = norm2_g[l][None, :]
        w_in_b = w_in[l].astype(BF16)
        w_out_b = w_out[l].astype(BF16)
        w_fi_b = w_ffn_in[l].astype(BF16)
        w_fo_b = w_ffn_out[l].astype(BF16)
        qg = jnp.tile(q_norm_g[l] * q_scale, ATT_WIDTH // HEAD_DIM)[None, :]
        kg = jnp.tile(k_norm_g[l], ATT_WIDTH // HEAD_DIM)[None, :]
        sgun = sgu_norm_g[l][None, :]
        sguw = sgu_w[l].astype(BF16)
        sgub = jnp.broadcast_to(sgu_b[l][:, :, None], (N_GROUPS, CHUNK, GROUP_DIM))
        mlpg = mlp_out_g[l][None, :]
        att_g = att_out_g[l].reshape(N_HEADS, 1, VAL_DIM)
        lams = (lambda_q1[l][None, :], lambda_k1[l][None, :],
                lambda_q2[l][None, :], lambda_k2[l][None, :])

        q, k, vt, mlp, k_c, v_c = _proj(
            xp, mods3, g1, w_in_b, pool, qg, kg, None, sgun, sguw, sgub, mlpg,
            seq_len=ctx_len, mods_row_fn=ctx_row)
        att = _attn_ctx(lams, q, k, vt, att_g, seq_len=ctx_len, lambda_init=lambda_init)
        xp = _ffn(xp, att, mlp, mods3, g2, w_out_b, w_fi_b, w_fo_b, mods_row_fn=ctx_row)
        k_states.append(k_c)
        v_states.append(v_c)

        q, k, vt, mlp = _proj(
            xs, mods3, g1, w_in_b, pool, qg, kg, rope_tabs, sgun, sguw, sgub, mlpg,
            seq_len=dec_len, mods_row_fn=dec_row_proj)
        att = _attn_cache(lams, q, k, vt, cache_kt, cache_v_ctx, att_g, layer=l,
                          n_batch=n_dec, seq_len=dec_len, lambda_init=lambda_init)
        xs = _ffn(xs, att, mlp, mods3, g2, w_out_b, w_fi_b, w_fo_b, mods_row_fn=dec_row_ffn)

    state_k = jnp.swapaxes(jnp.concatenate(k_states, axis=1), -1, -2)
    state_v = jnp.concatenate(v_states, axis=1)
    return (xp.reshape(n_ctx, ctx_len, D_MODEL), xs.reshape(n_dec, dec_len, D_MODEL),
            state_k, state_v)
```

```python
import functools
import math

import jax
import jax.numpy as jnp
import numpy as np
from jax import lax
from jax.experimental import pallas as pl
from jax.experimental.pallas import tpu as pltpu

D_MODEL = 1024
ATT_WIDTH = 512
N_HEADS = 4
HEAD_DIM = 64
VAL_DIM = 128
MLP_WIDTH = 512
N_GROUPS = 4
GROUP_DIM = 128
CHUNK = 128
D_FF = 2816
IN_WIDTH = 2560
GRID_W = 64
ROPE_THETA = 10000.0
ROPE_AXIS_DIM = 32
EPS = 1e-6
LOG2E = 1.4426950408889634
LANES = 128

F32 = jnp.float32
BF16 = jnp.bfloat16

VMEM_LIMIT_BYTES = 56 * 1024 * 1024
MODS_ROWS = 16
CTX_ROW = 8

TM_PROJ = 512
TM_FFN = 512
TM_ATTN_CTX = 512
CTX_AHEAD = 3
FF_CHUNKS = ((0, 1024), (1024, 1024), (2048, 768))

ONES_ROWS = 16
TQ_UNIT = 256
KEY_BLOCK = 256
SCORE_LEAD = 3


def _const_spec(shape):
    zeros = (0,) * len(shape)
    return pl.BlockSpec(shape, lambda *_: zeros, pipeline_mode=pl.Buffered(1))


def _params(n_grid):
    return pltpu.CompilerParams(
        dimension_semantics=("arbitrary",) * n_grid,
        vmem_limit_bytes=VMEM_LIMIT_BYTES,
    )


def _rms_scale(x):
    return lax.rsqrt(jnp.mean(x * x, axis=-1, keepdims=True) + EPS)


def _mods_kernel(cond_ref, w_ref, b_ref, o_ref):
    cnd = cond_ref[...]
    act = (cnd * jax.nn.sigmoid(cnd)).astype(BF16)
    o_ref[...] = jnp.dot(act, w_ref[...].astype(BF16), preferred_element_type=F32) + b_ref[...]


def _mods(cond, w_ada, b_ada):
    tn = 1536
    n_out = w_ada.shape[1]
    return pl.pallas_call(
        _mods_kernel,
        grid=(n_out // tn,),
        in_specs=[
            _const_spec((MODS_ROWS, D_MODEL)),
            pl.BlockSpec((D_MODEL, tn), lambda j: (0, j)),
            pl.BlockSpec((1, tn), lambda j: (0, j)),
        ],
        out_specs=pl.BlockSpec((MODS_ROWS, tn), lambda j: (0, j)),
        out_shape=jax.ShapeDtypeStruct((MODS_ROWS, n_out), F32),
        compiler_params=_params(1),
        name="mods",
    )(cond, w_ada, b_ada)


def _proj_kernel(*refs, rope, seq_len, tm):
    it = iter(refs)
    x_ref, mods_ref, g1_ref, w_in_ref, pool_ref, qg_ref, kg_ref = (next(it) for _ in range(7))
    if rope:
        cos_ref, sin_ref = (next(it) for _ in range(2))
    sgun_ref, sguw_ref, sgub_ref, mlpg_ref = (next(it) for _ in range(4))
    qt_ref, k_ref, vt_ref, mlp_ref = (next(it) for _ in range(4))
    if not rope:
        kst_ref, vst_ref = (next(it) for _ in range(2))
    gate_ref = next(it)

    x = x_ref[...]
    xn = x * _rms_scale(x) * g1_ref[...]
    xm = xn * (1.0 + mods_ref[0, 1:2, :]) + mods_ref[0, 0:1, :]
    xb = xm.astype(BF16)

    def section(lo, hi):
        return jnp.dot(xb, w_in_ref[:, lo:hi], preferred_element_type=F32)

    def head_norm(h, g_ref):
        sq = (h * h).astype(BF16)
        msq = jnp.dot(sq, pool_ref[...], preferred_element_type=F32)
        return h * lax.rsqrt(msq + EPS) * g_ref[...]

    if rope:
        lane = lax.broadcasted_iota(jnp.int32, (tm, LANES), 1)
        first_half = (lane % 32) < 16

    def emit_heads(hn, out_ref, transpose):
        for h in range(N_HEADS):
            hc = hn[:, h * LANES:(h + 1) * LANES]
            if rope:
                swapped = jnp.where(first_half,
                                    pltpu.roll(hc, LANES - 16, 1),
                                    pltpu.roll(hc, 16, 1))
                hc = hc * cos_ref[...] + swapped * sin_ref[...]
            out_ref[h] = (hc.T if transpose else hc).astype(BF16)

    qn = head_norm(section(0, ATT_WIDTH), qg_ref)
    kn = head_norm(section(ATT_WIDTH, 2 * ATT_WIDTH), kg_ref)
    emit_heads(qn, qt_ref, transpose=True)
    emit_heads(kn, k_ref, transpose=False)

    hv = section(2 * ATT_WIDTH, 3 * ATT_WIDTH)
    hvt = hv.T.astype(BF16)
    for h in range(N_HEADS):
        vt_ref[h] = hvt[h * VAL_DIM:(h + 1) * VAL_DIM, :]
    if not rope:
        knt = kn.T
        for s in range(tm // seq_len):
            rows = slice(s * seq_len, (s + 1) * seq_len)
            for h in range(N_HEADS):
                for i in range(2):
                    j = 2 * h + i
                    kst_ref[s, 0, h, i, :, :] = knt[j * HEAD_DIM:(j + 1) * HEAD_DIM, rows]
                vst_ref[s, 0, h, :, :] = hv[rows, h * VAL_DIM:(h + 1) * VAL_DIM]

    hu = section(3 * ATT_WIDTH, 3 * ATT_WIDTH + MLP_WIDTH)
    hg = section(3 * ATT_WIDTH + MLP_WIDTH, IN_WIDTH)
    for g in range(N_GROUPS):
        cols = slice(g * GROUP_DIM, (g + 1) * GROUP_DIM)
        gg = hg[:, cols]
        gn = (gg * _rms_scale(gg) * sgun_ref[:, cols]).astype(BF16)
        ug = hu[:, cols]
        wg = sguw_ref[g]
        bg = sgub_ref[g]
        for n in range(tm // CHUNK):
            rows = slice(n * CHUNK, (n + 1) * CHUNK)
            sp = jnp.dot(wg, gn[rows, :], preferred_element_type=F32) + bg
            gate_ref[rows, cols] = ug[rows, :] * sp
    o = gate_ref[...]
    mlp_ref[...] = (o * _rms_scale(o) * mlpg_ref[...]).astype(BF16)


def _proj(x2d, mods3, g1, w_in, pool, qg, kg, rope_tabs, sgun, sguw, sgub, mlpg,
          *, seq_len, mods_row_fn):
    n_tok = x2d.shape[0]
    tm = TM_PROJ
    rope = rope_tabs is not None
    blocks_per_seq = seq_len // tm if rope else None

    in_specs = [
        pl.BlockSpec((tm, D_MODEL), lambda i: (i, 0)),
        pl.BlockSpec((1, 6, D_MODEL), lambda i: (mods_row_fn(i), 0, 0)),
        _const_spec((1, D_MODEL)),
        _const_spec((D_MODEL, IN_WIDTH)),
        _const_spec((ATT_WIDTH, ATT_WIDTH)),
        _const_spec((1, ATT_WIDTH)),
        _const_spec((1, ATT_WIDTH)),
    ]
    args = [x2d, mods3, g1, w_in, pool, qg, kg]
    if rope:
        tab_spec = pl.BlockSpec((tm, LANES), lambda i: (i % blocks_per_seq, 0))
        in_specs += [tab_spec] * 2
        args += list(rope_tabs)
    in_specs += [
        _const_spec((1, MLP_WIDTH)),
        _const_spec((N_GROUPS, CHUNK, CHUNK)),
        _const_spec((N_GROUPS, CHUNK, GROUP_DIM)),
        _const_spec((1, MLP_WIDTH)),
    ]
    args += [sgun, sguw, sgub, mlpg]

    head_spec = pl.BlockSpec((N_HEADS, tm, LANES), lambda i: (0, i, 0))
    head_t_spec = pl.BlockSpec((N_HEADS, LANES, tm), lambda i: (0, 0, i))
    head_shape = jax.ShapeDtypeStruct((N_HEADS, n_tok, LANES), BF16)
    head_t_shape = jax.ShapeDtypeStruct((N_HEADS, LANES, n_tok), BF16)
    out_specs = [head_t_spec, head_spec, head_t_spec,
                 pl.BlockSpec((tm, MLP_WIDTH), lambda i: (i, 0))]
    out_shape = [head_t_shape, head_shape, head_t_shape,
                 jax.ShapeDtypeStruct((n_tok, MLP_WIDTH), BF16)]
    if not rope:
        n_seq = n_tok // seq_len
        spb = tm // seq_len
        out_specs += [
            pl.BlockSpec((spb, 1, N_HEADS, 2, HEAD_DIM, seq_len), lambda i: (i, 0, 0, 0, 0, 0)),
            pl.BlockSpec((spb, 1, N_HEADS, seq_len, VAL_DIM), lambda i: (i, 0, 0, 0, 0)),
        ]
        out_shape += [
            jax.ShapeDtypeStruct((n_seq, 1, N_HEADS, 2, HEAD_DIM, seq_len), F32),
            jax.ShapeDtypeStruct((n_seq, 1, N_HEADS, seq_len, VAL_DIM), F32),
        ]

    return pl.pallas_call(
        functools.partial(_proj_kernel, rope=rope, seq_len=seq_len, tm=tm),
        grid=(n_tok // tm,),
        in_specs=in_specs,
        out_specs=out_specs,
        out_shape=out_shape,
        scratch_shapes=[pltpu.VMEM((tm, MLP_WIDTH), F32)],
        compiler_params=_params(1),
        name="proj_rope" if rope else "proj_ctx",
    )(*args)


def _lambda_full(lq1, lk1, lq2, lk2, lambda_init):
    return (jnp.exp(jnp.sum(lq1[...] * lk1[...], keepdims=True))
            - jnp.exp(jnp.sum(lq2[...] * lk2[...], keepdims=True))
            + lambda_init)


def _map_queries(qt):
    row = lax.broadcasted_iota(jnp.int32, qt.shape, 0)
    zero = jnp.zeros_like(qt)
    return (jnp.where(row < HEAD_DIM, qt, zero), jnp.where(row >= HEAD_DIM, qt, zero))


def _combine_maps(o1, d1, o2, d2, lam, out_gain):
    ot = o1 * (1.0 / d1) - o2 * (lam / d2)
    ot = ot * lax.rsqrt(jnp.mean(ot * ot, axis=0, keepdims=True) + EPS)
    return (ot.T * out_gain).astype(BF16)


def _attn_ctx_kernel(lq1, lk1, lq2, lk2, qt_ref, k_ref, vt_ref, ag_ref, o_ref, st_buf,
                     *, lambda_init, seq_len):
    lam = _lambda_full(lq1, lk1, lq2, lk2, lambda_init)
    units = [(slice(s * seq_len, (s + 1) * seq_len), h)
             for s in range(k_ref.shape[1] // seq_len) for h in range(N_HEADS)]

    def scores(u):
        rows, h = units[u]
        kk = k_ref[h, rows, :]
        maxes = []
        for mp, qm in enumerate(_map_queries(qt_ref[h, :, rows])):
            st = jnp.dot(kk, qm, preferred_element_type=F32)
            st_buf[u % CTX_AHEAD, mp] = st
            maxes.append(jnp.max(st, axis=0, keepdims=True))
        return maxes

    def finish(u, maxes):
        rows, h = units[u]
        vt = vt_ref[h, :, rows]
        outs, dens = [], []
        for mp in range(2):
            e = jnp.exp2(st_buf[u % CTX_AHEAD, mp] - maxes[mp])
            dens.append(jnp.sum(e, axis=0, keepdims=True))
            outs.append(jnp.dot(vt, e.astype(BF16), preferred_element_type=F32))
        out_gain = (1.0 - lambda_init) * ag_ref[h]
        o_ref[h, rows, :] = _combine_maps(outs[0], dens[0], outs[1], dens[1], lam, out_gain)

    pending = [scores(u) for u in range(CTX_AHEAD - 1)]
    for u in range(len(units)):
        if u + CTX_AHEAD - 1 < len(units):
            pending.append(scores(u + CTX_AHEAD - 1))
        finish(u, pending.pop(0))


def _attn_ctx(lams, q, k, vt, att_g, *, seq_len, lambda_init):
    n_tok = k.shape[1]
    tm = TM_ATTN_CTX
    head_spec = pl.BlockSpec((N_HEADS, tm, LANES), lambda i: (0, i, 0))
    head_t_spec = pl.BlockSpec((N_HEADS, LANES, tm), lambda i: (0, 0, i))
    return pl.pallas_call(
        functools.partial(_attn_ctx_kernel, lambda_init=lambda_init, seq_len=seq_len),
        grid=(n_tok // tm,),
        in_specs=[_const_spec((1, HEAD_DIM))] * 4 + [
            head_t_spec,
            head_spec,
            head_t_spec,
            _const_spec((N_HEADS, 1, LANES)),
        ],
        out_specs=head_spec,
        out_shape=jax.ShapeDtypeStruct((N_HEADS, n_tok, LANES), BF16),
        scratch_shapes=[pltpu.VMEM((CTX_AHEAD, 2, seq_len, seq_len), F32)],
        compiler_params=_params(1),
        name="attn_ctx",
    )(*lams, q, k, vt, att_g)


def _attn_cache_kernel(lq1, lk1, lq2, lk2, qt_ref, k_ref, vt_ref, kct_ref, vc_ref, ag_ref, o_ref,
                       k_all, vt_all, m_buf, acc_buf, *bufs, lambda_init, n_new):
    st = (bufs[0:2], bufs[2:4])
    n_chunks = n_new // TQ_UNIT
    n_units = N_HEADS * n_chunks
    n_keys = k_all.shape[1]

    past = n_keys - n_new
    for h in range(N_HEADS):
        k_all[h, 0:n_new, :] = k_ref[h]
        k_all[h, n_new:, :] = kct_ref[0, 0, h].reshape(2 * HEAD_DIM, past).T.astype(BF16)
        vt_all[h, 0:VAL_DIM, 0:n_new] = vt_ref[h]
        vt_all[h, 0:VAL_DIM, n_new:] = vc_ref[0, 0, h].T.astype(BF16)
        vt_all[h, VAL_DIM:, :] = jnp.ones((ONES_ROWS, n_keys), BF16)

    lam = _lambda_full(lq1, lk1, lq2, lk2, lambda_init)

    def head_rows(u):
        c = u % n_chunks
        return u // n_chunks, pl.ds(pl.multiple_of(c * TQ_UNIT, TQ_UNIT), TQ_UNIT)

    def stage(fin, sc, defer_out=False):
        if sc is not None:
            sc_head, sc_rows = head_rows(sc[0])
            qms = _map_queries(qt_ref[sc_head, :, sc_rows])
            mrun = [None, None]
        if fin is not None:
            fin_head = fin[0] // n_chunks
            ms = [m_buf[fin[1], mp] for mp in range(2)]
            accs = [None, None]
        n_kb = n_keys // KEY_BLOCK
        lead = SCORE_LEAD if (sc is not None and fin is not None) else 0
        for step in range(n_kb + lead):
            if sc is not None and step < n_kb:
                kr = slice(step * KEY_BLOCK, (step + 1) * KEY_BLOCK)
                kk = k_all[sc_head, kr, :]
                for mp in range(2):
                    s = jnp.dot(kk, qms[mp], preferred_element_type=F32)
                    st[sc[1]][mp][kr, :] = s
                    smax = jnp.max(s.reshape(KEY_BLOCK // 8, 8, TQ_UNIT), axis=0)
                    mrun[mp] = smax if mrun[mp] is None else jnp.maximum(mrun[mp], smax)
            if fin is not None and step >= lead:
                kr = slice((step - lead) * KEY_BLOCK, (step - lead + 1) * KEY_BLOCK)
                vt = vt_all[fin_head, :, kr]
                for mp in range(2):
                    p = jnp.exp2(st[fin[1]][mp][kr, :] - ms[mp]).astype(BF16)
                    d = jnp.dot(vt, p, preferred_element_type=F32)
                    accs[mp] = d if accs[mp] is None else accs[mp] + d
        if sc is not None:
            for mp in range(2):
                m_buf[sc[1], mp] = jnp.max(mrun[mp], axis=0, keepdims=True)
        if fin is not None:
            if defer_out:
                for mp in range(2):
                    acc_buf[mp] = accs[mp]
            else:
                write_out(fin[0], accs[0], accs[1])

    def write_out(u, o1, o2):
        head, rows = head_rows(u)
        out_gain = (1.0 - lambda_init) * ag_ref[head]
        o_ref[head, rows, :] = _combine_maps(
            o1[0:VAL_DIM, :], o1[VAL_DIM:VAL_DIM + 1, :],
            o2[0:VAL_DIM, :], o2[VAL_DIM:VAL_DIM + 1, :], lam, out_gain)

    stage(None, (0, 0))
    stage((0, 0), (1, 1), defer_out=True)

    def pair(i, carry):
        u = 2 * i
        write_out(u - 2, acc_buf[0], acc_buf[1])
        stage((u - 1, 1), (u, 0))
        stage((u, 0), (u + 1, 1), defer_out=True)
        return carry

    lax.fori_loop(1, n_units // 2, pair, 0)
    write_out(n_units - 2, acc_buf[0], acc_buf[1])
    stage((n_units - 1, 1), None)


def _attn_cache(lams, q, k, vt, kct, vc, att_g, *, layer, n_batch, seq_len, lambda_init):
    past = vc.shape[3]
    n_keys = seq_len + past
    assert (seq_len // TQ_UNIT) % 2 == 0 and seq_len // TQ_UNIT >= 4
    assert n_keys % KEY_BLOCK == 0
    head_spec = pl.BlockSpec((N_HEADS, seq_len, LANES), lambda b: (0, b, 0))
    head_t_spec = pl.BlockSpec((N_HEADS, LANES, seq_len), lambda b: (0, 0, b))
    in_specs = [_const_spec((1, HEAD_DIM))] * 4 + [
        head_t_spec,
        head_spec,
        head_t_spec,
        pl.BlockSpec((1, 1, N_HEADS, 2, HEAD_DIM, past), lambda b: (b, layer, 0, 0, 0, 0)),
        pl.BlockSpec((1, 1, N_HEADS, past, VAL_DIM), lambda b: (b, layer, 0, 0, 0)),
        _const_spec((N_HEADS, 1, LANES)),
    ]
    return pl.pallas_call(
        functools.partial(_attn_cache_kernel, lambda_init=lambda_init, n_new=seq_len),
        grid=(n_batch,),
        in_specs=in_specs,
        out_specs=head_spec,
        out_shape=jax.ShapeDtypeStruct((N_HEADS, n_batch * seq_len, LANES), BF16),
        scratch_shapes=([pltpu.VMEM((N_HEADS, n_keys, LANES), BF16),
                         pltpu.VMEM((N_HEADS, VAL_DIM + ONES_ROWS, n_keys), BF16),
                         pltpu.VMEM((2, 2, 1, TQ_UNIT), F32),
                         pltpu.VMEM((2, VAL_DIM + ONES_ROWS, TQ_UNIT), F32)]
                        + [pltpu.VMEM((n_keys, TQ_UNIT), F32)] * 4),
        compiler_params=_params(1),
        name="attn_cache",
    )(*lams, q, k, vt, kct, vc, att_g)


def _ffn_kernel(x_ref, att_ref, mlp_ref, mods_ref, g2_ref, wo_ref, wfi_ref, wfo_ref, o_ref):
    tm = x_ref.shape[0]
    halves = [slice(i * (tm // 2), (i + 1) * (tm // 2)) for i in range(2)]
    x1s, xbs = [], []
    for r in halves:
        att = jnp.concatenate([att_ref[h, r, :] for h in range(N_HEADS)], axis=1)
        y = (jnp.dot(att, wo_ref[0:ATT_WIDTH, :], preferred_element_type=F32)
             + jnp.dot(mlp_ref[r, :], wo_ref[ATT_WIDTH:, :], preferred_element_type=F32))
        x1 = x_ref[r, :] + mods_ref[0, 2:3, :] * y
        xn = x1 * _rms_scale(x1) * g2_ref[...]
        x1s.append(x1)
        xbs.append((xn * (1.0 + mods_ref[0, 4:5, :]) + mods_ref[0, 3:4, :]).astype(BF16))
    accs = [None, None]
    for c0, cw in FF_CHUNKS:
        pre = []
        for xb in xbs:
            gte = jnp.dot(xb, wfi_ref[:, c0:c0 + cw], preferred_element_type=F32)
            up = jnp.dot(xb, wfi_ref[:, D_FF + c0:D_FF + c0 + cw], preferred_element_type=F32)
            pre.append((gte, up))
        for i, (gte, up) in enumerate(pre):
            act = (gte * jax.nn.sigmoid(gte) * up).astype(BF16)
            part = jnp.dot(act, wfo_ref[c0:c0 + cw, :], preferred_element_type=F32)
            accs[i] = part if accs[i] is None else accs[i] + part
    for r, x1, acc in zip(halves, x1s, accs):
        o_ref[r, :] = x1 + mods_ref[0, 5:6, :] * acc


def _ffn(x2d, att, mlp, mods3, g2, w_out, w_ffn_in, w_ffn_out, *, mods_row_fn):
    n_tok = x2d.shape[0]
    tm = TM_FFN
    return pl.pallas_call(
        _ffn_kernel,
        grid=(n_tok // tm,),
        in_specs=[
            pl.BlockSpec((tm, D_MODEL), lambda i: (i, 0)),
            pl.BlockSpec((N_HEADS, tm, LANES), lambda i: (0, i, 0)),
            pl.BlockSpec((tm, MLP_WIDTH), lambda i: (i, 0)),
            pl.BlockSpec((1, 6, D_MODEL), lambda i: (mods_row_fn(i), 0, 0)),
            _const_spec((1, D_MODEL)),
            _const_spec((D_MODEL, D_MODEL)),
            _const_spec((D_MODEL, 2 * D_FF)),
            _const_spec((D_FF, D_MODEL)),
        ],
        out_specs=pl.BlockSpec((tm, D_MODEL), lambda i: (i, 0)),
        out_shape=jax.ShapeDtypeStruct((n_tok, D_MODEL), F32),
        compiler_params=_params(1),
        name="ffn",
    )(x2d, att, mlp, mods3, g2, w_out, w_ffn_in, w_ffn_out)


def _rope_tables(n):
    pos = np.arange(n)
    row = (pos // GRID_W).astype(np.float32)
    col = (pos % GRID_W).astype(np.float32)
    inv = (ROPE_THETA ** (-np.arange(0, ROPE_AXIS_DIM, 2, dtype=np.float32) / ROPE_AXIS_DIM)
           ).astype(np.float32)
    ang_r = row[:, None] * inv[None, :]
    ang_c = col[:, None] * inv[None, :]
    cos64 = np.concatenate([np.cos(ang_r)] * 2 + [np.cos(ang_c)] * 2, axis=1)
    sin64 = np.concatenate([-np.sin(ang_r), np.sin(ang_r), -np.sin(ang_c), np.sin(ang_c)], axis=1)
    return (np.tile(cos64, (1, 2)).astype(np.float32), np.tile(sin64, (1, 2)).astype(np.float32))


def kernel(x_prompt, x_sample, cache_k_ctx, cache_v_ctx, c, c_ctx, norm1_g, norm2_g, w_ada, b_ada, w_in, q_norm_g, k_norm_g, lambda_q1, lambda_k1, lambda_q2, lambda_k2, att_out_g, sgu_norm_g, sgu_w, sgu_b, mlp_out_g, w_out, w_ffn_in, w_ffn_out):
    n_ctx, ctx_len, _ = x_prompt.shape
    n_dec, dec_len, _ = x_sample.shape
    depth = norm1_g.shape[0]

    rope_tabs = _rope_tables(dec_len)
    q_scale = LOG2E / math.sqrt(HEAD_DIM)
    group = np.arange(ATT_WIDTH) // HEAD_DIM
    pool = jnp.asarray(np.where(group[:, None] == group[None, :], 1.0 / HEAD_DIM, 0.0), BF16)

    cond = jnp.concatenate(
        [c, c_ctx[None, :], jnp.zeros((MODS_ROWS - n_dec - 1, D_MODEL), F32)], axis=0)

    xp = x_prompt.reshape(n_ctx * ctx_len, D_MODEL)
    xs = x_sample.reshape(n_dec * dec_len, D_MODEL)
    cache_kt = jnp.swapaxes(cache_k_ctx, -1, -2)
    k_states, v_states = [], []
    ctx_row = lambda i: CTX_ROW
    dec_row_proj = lambda i: i // (dec_len // TM_PROJ)
    dec_row_ffn = lambda i: i // (dec_len // TM_FFN)

    for l in range(depth):
        lambda_init = 0.8 - 0.6 * math.exp(-0.3 * l)
        mods3 = _mods(cond, w_ada[l], b_ada[l][None, :]).reshape(MODS_ROWS, 6, D_MODEL)
        g1 = norm1_g[l][None, :]
        g2 = norm2_g[l][None, :]
        w_in_b = w_in[l].astype(BF16)
        w_out_b = w_out[l].astype(BF16)
        w_fi_b = w_ffn_in[l].astype(BF16)
        w_fo_b = w_ffn_out[l].astype(BF16)
        qg = jnp.tile(q_norm_g[l] * q_scale, ATT_WIDTH // HEAD_DIM)[None, :]
        kg = jnp.tile(k_norm_g[l], ATT_WIDTH // HEAD_DIM)[None, :]
        sgun = sgu_norm_g[l][None, :]
        sguw = sgu_w[l].astype(BF16)
        sgub = jnp.broadcast_to(sgu_b[l][:, :, None], (N_GROUPS, CHUNK, GROUP_DIM))
        mlpg = mlp_out_g[l][None, :]
        att_g = att_out_g[l].reshape(N_HEADS, 1, VAL_DIM)
        lams = (lambda_q1[l][None, :], lambda_k1[l][None, :],
                lambda_q2[l][None, :], lambda_k2[l][None, :])

        q, k, vt, mlp, k_c, v_c = _proj(
            xp, mods3, g1, w_in_b, pool, qg, kg, None, sgun, sguw, sgub, mlpg,
            seq_len=ctx_len, mods_row_fn=ctx_row)
        att = _attn_ctx(lams, q, k, vt, att_g, seq_len=ctx_len, lambda_init=lambda_init)
        xp = _ffn(xp, att, mlp, mods3, g2, w_out_b, w_fi_b, w_fo_b, mods_row_fn=ctx_row)
        k_states.append(k_c)
        v_states.append(v_c)

        q, k, vt, mlp = _proj(
            xs, mods3, g1, w_in_b, pool, qg, kg, rope_tabs, sgun, sguw, sgub, mlpg,
            seq_len=dec_len, mods_row_fn=dec_row_proj)
        att = _attn_cache(lams, q, k, vt, cache_kt, cache_v_ctx, att_g, layer=l,
                          n_batch=n_dec, seq_len=dec_len, lambda_init=lambda_init)
        xs = _ffn(xs, att, mlp, mods3, g2, w_out_b, w_fi_b, w_fo_b, mods_row_fn=dec_row_ffn)

    state_k = jnp.swapaxes(jnp.concatenate(k_states, axis=1), -1, -2)
    state_v = jnp.concatenate(v_states, axis=1)
    return (xp.reshape(n_ctx, ctx_len, D_MODEL), xs.reshape(n_dec, dec_len, D_MODEL),
            state_k, state_v)
```

```python
import functools
import math

import jax
import jax.numpy as jnp
import numpy as np
from jax import lax
from jax.experimental import pallas as pl
from jax.experimental.pallas import tpu as pltpu

D_MODEL = 1024
ATT_WIDTH = 512
N_HEADS = 4
HEAD_DIM = 64
VAL_DIM = 128
MLP_WIDTH = 512
N_GROUPS = 4
GROUP_DIM = 128
CHUNK = 128
D_FF = 2816
IN_WIDTH = 2560
GRID_W = 64
ROPE_THETA = 10000.0
ROPE_AXIS_DIM = 32
EPS = 1e-6
LOG2E = 1.4426950408889634
LANES = 128

F32 = jnp.float32
BF16 = jnp.bfloat16

VMEM_LIMIT_BYTES = 56 * 1024 * 1024
MODS_ROWS = 16
CTX_ROW = 8

TM_PROJ = 512
TM_FFN = 512
TM_ATTN_CTX = 512
CTX_AHEAD = 3
FF_CHUNKS = ((0, 1024), (1024, 1024), (2048, 768))

ONES_ROWS = 16
TQ_UNIT = 256
KEY_BLOCK = 256
SCORE_LEAD = 3


def _const_spec(shape):
    zeros = (0,) * len(shape)
    return pl.BlockSpec(shape, lambda *_: zeros, pipeline_mode=pl.Buffered(1))


def _params(n_grid):
    return pltpu.CompilerParams(
        dimension_semantics=("arbitrary",) * n_grid,
        vmem_limit_bytes=VMEM_LIMIT_BYTES,
    )


def _rms_scale(x):
    return lax.rsqrt(jnp.mean(x * x, axis=-1, keepdims=True) + EPS)


def _mods_kernel(cond_ref, w_ref, b_ref, o_ref):
    cnd = cond_ref[...]
    act = (cnd * jax.nn.sigmoid(cnd)).astype(BF16)
    o_ref[...] = jnp.dot(act, w_ref[...].astype(BF16), preferred_element_type=F32) + b_ref[...]


def _mods(cond, w_ada, b_ada):
    tn = 1536
    n_out = w_ada.shape[1]
    return pl.pallas_call(
        _mods_kernel,
        grid=(n_out // tn,),
        in_specs=[
            _const_spec((MODS_ROWS, D_MODEL)),
            pl.BlockSpec((D_MODEL, tn), lambda j: (0, j)),
            pl.BlockSpec((1, tn), lambda j: (0, j)),
        ],
        out_specs=pl.BlockSpec((MODS_ROWS, tn), lambda j: (0, j)),
        out_shape=jax.ShapeDtypeStruct((MODS_ROWS, n_out), F32),
        compiler_params=_params(1),
        name="mods",
    )(cond, w_ada, b_ada)


def _proj_kernel(*refs, rope, seq_len, tm):
    it = iter(refs)
    x_ref, mods_ref, g1_ref, w_in_ref, qg_ref, kg_ref = (next(it) for _ in range(6))
    if rope:
        cos_ref, sin_ref = (next(it) for _ in range(2))
    sgun_ref, sguw_ref, sgub_ref, mlpg_ref = (next(it) for _ in range(4))
    qt_ref, k_ref, vt_ref, mlp_ref = (next(it) for _ in range(4))
    if not rope:
        kst_ref, vst_ref = (next(it) for _ in range(2))
    gate_ref = next(it)

    x = x_ref[...]
    xn = x * _rms_scale(x) * g1_ref[...]
    xm = xn * (1.0 + mods_ref[0, 1:2, :]) + mods_ref[0, 0:1, :]
    xb = xm.astype(BF16)

    def section(lo, hi):
        return jnp.dot(xb, w_in_ref[:, lo:hi], preferred_element_type=F32)

    def head_t(sec, h, g_ref):
        t = sec[:, h * LANES:(h + 1) * LANES].T
        maps = []
        for mp in range(2):
            tmap = t[mp * HEAD_DIM:(mp + 1) * HEAD_DIM, :]
            maps.append(tmap * lax.rsqrt(jnp.mean(tmap * tmap, axis=0, keepdims=True) + EPS))
        gain = jnp.concatenate([g_ref[...]] * (tm // LANES), axis=1)
        tn = jnp.concatenate(maps, axis=0) * gain
        if rope:
            swapped = jnp.concatenate(
                [tn[r0 + off:r0 + off + 16, :] for r0 in range(0, LANES, 32) for off in (16, 0)],
                axis=0)
            tn = tn * cos_ref[...] + swapped * sin_ref[...]
        return tn

    hq = section(0, ATT_WIDTH)
    hk = section(ATT_WIDTH, 2 * ATT_WIDTH)
    for h in range(N_HEADS):
        qt_ref[h] = head_t(hq, h, qg_ref).astype(BF16)
        kt = head_t(hk, h, kg_ref)
        k_ref[h] = kt.T.astype(BF16)
        if not rope:
            for s in range(tm // seq_len):
                for i in range(2):
                    kst_ref[s, 0, h, i, :, :] = kt[i * HEAD_DIM:(i + 1) * HEAD_DIM,
                                                   s * seq_len:(s + 1) * seq_len]

    hv = section(2 * ATT_WIDTH, 3 * ATT_WIDTH)
    hvt = hv.T.astype(BF16)
    for h in range(N_HEADS):
        vt_ref[h] = hvt[h * VAL_DIM:(h + 1) * VAL_DIM, :]
        if not rope:
            for s in range(tm // seq_len):
                vst_ref[s, 0, h, :, :] = hv[s * seq_len:(s + 1) * seq_len,
                                            h * VAL_DIM:(h + 1) * VAL_DIM]

    hu = section(3 * ATT_WIDTH, 3 * ATT_WIDTH + MLP_WIDTH)
    hg = section(3 * ATT_WIDTH + MLP_WIDTH, IN_WIDTH)
    for g in range(N_GROUPS):
        cols = slice(g * GROUP_DIM, (g + 1) * GROUP_DIM)
        gg = hg[:, cols]
        gn = (gg * _rms_scale(gg) * sgun_ref[:, cols]).astype(BF16)
        ug = hu[:, cols]
        wg = sguw_ref[g]
        bg = sgub_ref[g]
        for n in range(tm // CHUNK):
            rows = slice(n * CHUNK, (n + 1) * CHUNK)
            sp = jnp.dot(wg, gn[rows, :], preferred_element_type=F32) + bg
            gate_ref[rows, cols] = ug[rows, :] * sp
    o = gate_ref[...]
    mlp_ref[...] = (o * _rms_scale(o) * mlpg_ref[...]).astype(BF16)


def _proj(x2d, mods3, g1, w_in, qg, kg, rope_tabs, sgun, sguw, sgub, mlpg,
          *, seq_len, mods_row_fn):
    n_tok = x2d.shape[0]
    tm = TM_PROJ
    rope = rope_tabs is not None
    blocks_per_seq = seq_len // tm if rope else None

    in_specs = [
        pl.BlockSpec((tm, D_MODEL), lambda i: (i, 0)),
        pl.BlockSpec((1, 6, D_MODEL), lambda i: (mods_row_fn(i), 0, 0)),
        _const_spec((1, D_MODEL)),
        _const_spec((D_MODEL, IN_WIDTH)),
        _const_spec((LANES, LANES)),
        _const_spec((LANES, LANES)),
    ]
    args = [x2d, mods3, g1, w_in, qg, kg]
    if rope:
        tab_spec = pl.BlockSpec((LANES, tm), lambda i: (0, i % blocks_per_seq))
        in_specs += [tab_spec] * 2
        args += list(rope_tabs)
    in_specs += [
        _const_spec((1, MLP_WIDTH)),
        _const_spec((N_GROUPS, CHUNK, CHUNK)),
        _const_spec((N_GROUPS, CHUNK, GROUP_DIM)),
        _const_spec((1, MLP_WIDTH)),
    ]
    args += [sgun, sguw, sgub, mlpg]

    head_spec = pl.BlockSpec((N_HEADS, tm, LANES), lambda i: (0, i, 0))
    head_t_spec = pl.BlockSpec((N_HEADS, LANES, tm), lambda i: (0, 0, i))
    head_shape = jax.ShapeDtypeStruct((N_HEADS, n_tok, LANES), BF16)
    head_t_shape = jax.ShapeDtypeStruct((N_HEADS, LANES, n_tok), BF16)
    out_specs = [head_t_spec, head_spec, head_t_spec,
                 pl.BlockSpec((tm, MLP_WIDTH), lambda i: (i, 0))]
    out_shape = [head_t_shape, head_shape, head_t_shape,
                 jax.ShapeDtypeStruct((n_tok, MLP_WIDTH), BF16)]
    if not rope:
        n_seq = n_tok // seq_len
        spb = tm // seq_len
        out_specs += [
            pl.BlockSpec((spb, 1, N_HEADS, 2, HEAD_DIM, seq_len), lambda i: (i, 0, 0, 0, 0, 0)),
            pl.BlockSpec((spb, 1, N_HEADS, seq_len, VAL_DIM), lambda i: (i, 0, 0, 0, 0)),
        ]
        out_shape += [
            jax.ShapeDtypeStruct((n_seq, 1, N_HEADS, 2, HEAD_DIM, seq_len), F32),
            jax.ShapeDtypeStruct((n_seq, 1, N_HEADS, seq_len, VAL_DIM), F32),
        ]

    return pl.pallas_call(
        functools.partial(_proj_kernel, rope=rope, seq_len=seq_len, tm=tm),
        grid=(n_tok // tm,),
        in_specs=in_specs,
        out_specs=out_specs,
        out_shape=out_shape,
        scratch_shapes=[pltpu.VMEM((tm, MLP_WIDTH), F32)],
        compiler_params=_params(1),
        name="proj_rope" if rope else "proj_ctx",
    )(*args)


def _lambda_full(lq1, lk1, lq2, lk2, lambda_init):
    return (jnp.exp(jnp.sum(lq1[...] * lk1[...], keepdims=True))
            - jnp.exp(jnp.sum(lq2[...] * lk2[...], keepdims=True))
            + lambda_init)


def _map_queries(qt):
    row = lax.broadcasted_iota(jnp.int32, qt.shape, 0)
    zero = jnp.zeros_like(qt)
    return (jnp.where(row < HEAD_DIM, qt, zero), jnp.where(row >= HEAD_DIM, qt, zero))


def _combine_maps(o1, d1, o2, d2, lam, out_gain):
    ot = o1 * (1.0 / d1) - o2 * (lam / d2)
    ot = ot * lax.rsqrt(jnp.mean(ot * ot, axis=0, keepdims=True) + EPS)
    return (ot.T * out_gain).astype(BF16)


def _attn_ctx_kernel(lq1, lk1, lq2, lk2, qt_ref, k_ref, vt_ref, ag_ref, o_ref, st_buf,
                     *, lambda_init, seq_len):
    lam = _lambda_full(lq1, lk1, lq2, lk2, lambda_init)
    units = [(slice(s * seq_len, (s + 1) * seq_len), h)
             for s in range(k_ref.shape[1] // seq_len) for h in range(N_HEADS)]

    def scores(u):
        rows, h = units[u]
        kk = k_ref[h, rows, :]
        maxes = []
        for mp, qm in enumerate(_map_queries(qt_ref[h, :, rows])):
            st = jnp.dot(kk, qm, preferred_element_type=F32)
            st_buf[u % CTX_AHEAD, mp] = st
            maxes.append(jnp.max(st, axis=0, keepdims=True))
        return maxes

    def finish(u, maxes):
        rows, h = units[u]
        vt = vt_ref[h, :, rows]
        outs, dens = [], []
        for mp in range(2):
            e = jnp.exp2(st_buf[u % CTX_AHEAD, mp] - maxes[mp])
            dens.append(jnp.sum(e, axis=0, keepdims=True))
            outs.append(jnp.dot(vt, e.astype(BF16), preferred_element_type=F32))
        out_gain = (1.0 - lambda_init) * ag_ref[h]
        o_ref[h, rows, :] = _combine_maps(outs[0], dens[0], outs[1], dens[1], lam, out_gain)

    pending = [scores(u) for u in range(CTX_AHEAD - 1)]
    for u in range(len(units)):
        if u + CTX_AHEAD - 1 < len(units):
            pending.append(scores(u + CTX_AHEAD - 1))
        finish(u, pending.pop(0))


def _attn_ctx(lams, q, k, vt, att_g, *, seq_len, lambda_init):
    n_tok = k.shape[1]
    tm = TM_ATTN_CTX
    head_spec = pl.BlockSpec((N_HEADS, tm, LANES), lambda i: (0, i, 0))
    head_t_spec = pl.BlockSpec((N_HEADS, LANES, tm), lambda i: (0, 0, i))
    return pl.pallas_call(
        functools.partial(_attn_ctx_kernel, lambda_init=lambda_init, seq_len=seq_len),
        grid=(n_tok // tm,),
        in_specs=[_const_spec((1, HEAD_DIM))] * 4 + [
            head_t_spec,
            head_spec,
            head_t_spec,
            _const_spec((N_HEADS, 1, LANES)),
        ],
        out_specs=head_spec,
        out_shape=jax.ShapeDtypeStruct((N_HEADS, n_tok, LANES), BF16),
        scratch_shapes=[pltpu.VMEM((CTX_AHEAD, 2, seq_len, seq_len), F32)],
        compiler_params=_params(1),
        name="attn_ctx",
    )(*lams, q, k, vt, att_g)


def _attn_cache_kernel(lq1, lk1, lq2, lk2, qt_ref, k_ref, vt_ref, kct_ref, vc_ref, ag_ref, o_ref,
                       k_all, vt_all, m_buf, acc_buf, *bufs, lambda_init, n_new):
    st = (bufs[0:2], bufs[2:4])
    n_chunks = n_new // TQ_UNIT
    n_units = N_HEADS * n_chunks
    n_keys = k_all.shape[1]

    past = n_keys - n_new
    for h in range(N_HEADS):
        k_all[h, 0:n_new, :] = k_ref[h]
        k_all[h, n_new:, :] = kct_ref[0, 0, h].reshape(2 * HEAD_DIM, past).T.astype(BF16)
        vt_all[h, 0:VAL_DIM, 0:n_new] = vt_ref[h]
        vt_all[h, 0:VAL_DIM, n_new:] = vc_ref[0, 0, h].T.astype(BF16)
        vt_all[h, VAL_DIM:, :] = jnp.ones((ONES_ROWS, n_keys), BF16)

    lam = _lambda_full(lq1, lk1, lq2, lk2, lambda_init)

    def head_rows(u):
        c = u % n_chunks
        return u // n_chunks, pl.ds(pl.multiple_of(c * TQ_UNIT, TQ_UNIT), TQ_UNIT)

    def stage(fin, sc, defer_out=False):
        if sc is not None:
            sc_head, sc_rows = head_rows(sc[0])
            qms = _map_queries(qt_ref[sc_head, :, sc_rows])
            mrun = [None, None]
        if fin is not None:
            fin_head = fin[0] // n_chunks
            ms = [m_buf[fin[1], mp] for mp in range(2)]
            accs = [None, None]
        n_kb = n_keys // KEY_BLOCK
        lead = SCORE_LEAD if (sc is not None and fin is not None) else 0
        for step in range(n_kb + lead):
            if sc is not None and step < n_kb:
                kr = slice(step * KEY_BLOCK, (step + 1) * KEY_BLOCK)
                kk = k_all[sc_head, kr, :]
                for mp in range(2):
                    s = jnp.dot(kk, qms[mp], preferred_element_type=F32)
                    st[sc[1]][mp][kr, :] = s
                    smax = jnp.max(s.reshape(KEY_BLOCK // 8, 8, TQ_UNIT), axis=0)
                    mrun[mp] = smax if mrun[mp] is None else jnp.maximum(mrun[mp], smax)
            if fin is not None and step >= lead:
                kr = slice((step - lead) * KEY_BLOCK, (step - lead + 1) * KEY_BLOCK)
                vt = vt_all[fin_head, :, kr]
                for mp in range(2):
                    p = jnp.exp2(st[fin[1]][mp][kr, :] - ms[mp]).astype(BF16)
                    d = jnp.dot(vt, p, preferred_element_type=F32)
                    accs[mp] = d if accs[mp] is None else accs[mp] + d
        if sc is not None:
            for mp in range(2):
                m_buf[sc[1], mp] = jnp.max(mrun[mp], axis=0, keepdims=True)
        if fin is not None:
            if defer_out:
                for mp in range(2):
                    acc_buf[mp] = accs[mp]
            else:
                write_out(fin[0], accs[0], accs[1])

    def write_out(u, o1, o2):
        head, rows = head_rows(u)
        out_gain = (1.0 - lambda_init) * ag_ref[head]
        o_ref[head, rows, :] = _combine_maps(
            o1[0:VAL_DIM, :], o1[VAL_DIM:VAL_DIM + 1, :],
            o2[0:VAL_DIM, :], o2[VAL_DIM:VAL_DIM + 1, :], lam, out_gain)

    stage(None, (0, 0))
    stage((0, 0), (1, 1), defer_out=True)

    def pair(i, carry):
        u = 2 * i
        write_out(u - 2, acc_buf[0], acc_buf[1])
        stage((u - 1, 1), (u, 0))
        stage((u, 0), (u + 1, 1), defer_out=True)
        return carry

    lax.fori_loop(1, n_units // 2, pair, 0)
    write_out(n_units - 2, acc_buf[0], acc_buf[1])
    stage((n_units - 1, 1), None)


def _attn_cache(lams, q, k, vt, kct, vc, att_g, *, layer, n_batch, seq_len, lambda_init):
    past = vc.shape[3]
    n_keys = seq_len + past
    assert (seq_len // TQ_UNIT) % 2 == 0 and seq_len // TQ_UNIT >= 4
    assert n_keys % KEY_BLOCK == 0
    head_spec = pl.BlockSpec((N_HEADS, seq_len, LANES), lambda b: (0, b, 0))
    head_t_spec = pl.BlockSpec((N_HEADS, LANES, seq_len), lambda b: (0, 0, b))
    in_specs = [_const_spec((1, HEAD_DIM))] * 4 + [
        head_t_spec,
        head_spec,
        head_t_spec,
        pl.BlockSpec((1, 1, N_HEADS, 2, HEAD_DIM, past), lambda b: (b, layer, 0, 0, 0, 0)),
        pl.BlockSpec((1, 1, N_HEADS, past, VAL_DIM), lambda b: (b, layer, 0, 0, 0)),
        _const_spec((N_HEADS, 1, LANES)),
    ]
    return pl.pallas_call(
        functools.partial(_attn_cache_kernel, lambda_init=lambda_init, n_new=seq_len),
        grid=(n_batch,),
        in_specs=in_specs,
        out_specs=head_spec,
        out_shape=jax.ShapeDtypeStruct((N_HEADS, n_batch * seq_len, LANES), BF16),
        scratch_shapes=([pltpu.VMEM((N_HEADS, n_keys, LANES), BF16),
                         pltpu.VMEM((N_HEADS, VAL_DIM + ONES_ROWS, n_keys), BF16),
                         pltpu.VMEM((2, 2, 1, TQ_UNIT), F32),
                         pltpu.VMEM((2, VAL_DIM + ONES_ROWS, TQ_UNIT), F32)]
                        + [pltpu.VMEM((n_keys, TQ_UNIT), F32)] * 4),
        compiler_params=_params(1),
        name="attn_cache",
    )(*lams, q, k, vt, kct, vc, att_g)


def _ffn_kernel(x_ref, att_ref, mlp_ref, mods_ref, g2_ref, wo_ref, wfi_ref, wfo_ref, o_ref):
    tm = x_ref.shape[0]
    halves = [slice(i * (tm // 2), (i + 1) * (tm // 2)) for i in range(2)]
    x1s, xbs = [], []
    for r in halves:
        att = jnp.concatenate([att_ref[h, r, :] for h in range(N_HEADS)], axis=1)
        y = (jnp.dot(att, wo_ref[0:ATT_WIDTH, :], preferred_element_type=F32)
             + jnp.dot(mlp_ref[r, :], wo_ref[ATT_WIDTH:, :], preferred_element_type=F32))
        x1 = x_ref[r, :] + mods_ref[0, 2:3, :] * y
        xn = x1 * _rms_scale(x1) * g2_ref[...]
        x1s.append(x1)
        xbs.append((xn * (1.0 + mods_ref[0, 4:5, :]) + mods_ref[0, 3:4, :]).astype(BF16))
    accs = [None, None]
    for c0, cw in FF_CHUNKS:
        pre = []
        for xb in xbs:
            gte = jnp.dot(xb, wfi_ref[:, c0:c0 + cw], preferred_element_type=F32)
            up = jnp.dot(xb, wfi_ref[:, D_FF + c0:D_FF + c0 + cw], preferred_element_type=F32)
            pre.append((gte, up))
        for i, (gte, up) in enumerate(pre):
            act = (gte * jax.nn.sigmoid(gte) * up).astype(BF16)
            part = jnp.dot(act, wfo_ref[c0:c0 + cw, :], preferred_element_type=F32)
            accs[i] = part if accs[i] is None else accs[i] + part
    for r, x1, acc in zip(halves, x1s, accs):
        o_ref[r, :] = x1 + mods_ref[0, 5:6, :] * acc


def _ffn(x2d, att, mlp, mods3, g2, w_out, w_ffn_in, w_ffn_out, *, mods_row_fn):
    n_tok = x2d.shape[0]
    tm = TM_FFN
    return pl.pallas_call(
        _ffn_kernel,
        grid=(n_tok // tm,),
        in_specs=[
            pl.BlockSpec((tm, D_MODEL), lambda i: (i, 0)),
            pl.BlockSpec((N_HEADS, tm, LANES), lambda i: (0, i, 0)),
            pl.BlockSpec((tm, MLP_WIDTH), lambda i: (i, 0)),
            pl.BlockSpec((1, 6, D_MODEL), lambda i: (mods_row_fn(i), 0, 0)),
            _const_spec((1, D_MODEL)),
            _const_spec((D_MODEL, D_MODEL)),
            _const_spec((D_MODEL, 2 * D_FF)),
            _const_spec((D_FF, D_MODEL)),
        ],
        out_specs=pl.BlockSpec((tm, D_MODEL), lambda i: (i, 0)),
        out_shape=jax.ShapeDtypeStruct((n_tok, D_MODEL), F32),
        compiler_params=_params(1),
        name="ffn",
    )(x2d, att, mlp, mods3, g2, w_out, w_ffn_in, w_ffn_out)


def _rope_tables(n):
    pos = np.arange(n)
    row = (pos // GRID_W).astype(np.float32)
    col = (pos % GRID_W).astype(np.float32)
    inv = (ROPE_THETA ** (-np.arange(0, ROPE_AXIS_DIM, 2, dtype=np.float32) / ROPE_AXIS_DIM)
           ).astype(np.float32)
    ang_r = row[:, None] * inv[None, :]
    ang_c = col[:, None] * inv[None, :]
    cos64 = np.concatenate([np.cos(ang_r)] * 2 + [np.cos(ang_c)] * 2, axis=1)
    sin64 = np.concatenate([-np.sin(ang_r), np.sin(ang_r), -np.sin(ang_c), np.sin(ang_c)], axis=1)
    return (np.ascontiguousarray(np.tile(cos64, (1, 2)).T, np.float32),
            np.ascontiguousarray(np.tile(sin64, (1, 2)).T, np.float32))


def kernel(x_prompt, x_sample, cache_k_ctx, cache_v_ctx, c, c_ctx, norm1_g, norm2_g, w_ada, b_ada, w_in, q_norm_g, k_norm_g, lambda_q1, lambda_k1, lambda_q2, lambda_k2, att_out_g, sgu_norm_g, sgu_w, sgu_b, mlp_out_g, w_out, w_ffn_in, w_ffn_out):
    n_ctx, ctx_len, _ = x_prompt.shape
    n_dec, dec_len, _ = x_sample.shape
    depth = norm1_g.shape[0]

    rope_tabs = _rope_tables(dec_len)
    q_scale = LOG2E / math.sqrt(HEAD_DIM)

    cond = jnp.concatenate(
        [c, c_ctx[None, :], jnp.zeros((MODS_ROWS - n_dec - 1, D_MODEL), F32)], axis=0)

    xp = x_prompt.reshape(n_ctx * ctx_len, D_MODEL)
    xs = x_sample.reshape(n_dec * dec_len, D_MODEL)
    cache_kt = jnp.swapaxes(cache_k_ctx, -1, -2)
    k_states, v_states = [], []
    ctx_row = lambda i: CTX_ROW
    dec_row_proj = lambda i: i // (dec_len // TM_PROJ)
    dec_row_ffn = lambda i: i // (dec_len // TM_FFN)

    for l in range(depth):
        lambda_init = 0.8 - 0.6 * math.exp(-0.3 * l)
        mods3 = _mods(cond, w_ada[l], b_ada[l][None, :]).reshape(MODS_ROWS, 6, D_MODEL)
        g1 = norm1_g[l][None, :]
        g2 = norm2_g[l][None, :]
        w_in_b = w_in[l].astype(BF16)
        w_out_b = w_out[l].astype(BF16)
        w_fi_b = w_ffn_in[l].astype(BF16)
        w_fo_b = w_ffn_out[l].astype(BF16)
        qg = jnp.broadcast_to(jnp.tile(q_norm_g[l] * q_scale, 2)[:, None], (LANES, LANES))
        kg = jnp.broadcast_to(jnp.tile(k_norm_g[l], 2)[:, None], (LANES, LANES))
        sgun = sgu_norm_g[l][None, :]
        sguw = sgu_w[l].astype(BF16)
        sgub = jnp.broadcast_to(sgu_b[l][:, :, None], (N_GROUPS, CHUNK, GROUP_DIM))
        mlpg = mlp_out_g[l][None, :]
        att_g = att_out_g[l].reshape(N_HEADS, 1, VAL_DIM)
        lams = (lambda_q1[l][None, :], lambda_k1[l][None, :],
                lambda_q2[l][None, :], lambda_k2[l][None, :])

        q, k, vt, mlp, k_c, v_c = _proj(
            xp, mods3, g1, w_in_b, qg, kg, None, sgun, sguw, sgub, mlpg,
            seq_len=ctx_len, mods_row_fn=ctx_row)
        att = _attn_ctx(lams, q, k, vt, att_g, seq_len=ctx_len, lambda_init=lambda_init)
        xp = _ffn(xp, att, mlp, mods3, g2, w_out_b, w_fi_b, w_fo_b, mods_row_fn=ctx_row)
        k_states.append(k_c)
        v_states.append(v_c)

        q, k, vt, mlp = _proj(
            xs, mods3, g1, w_in_b, qg, kg, rope_tabs, sgun, sguw, sgub, mlpg,
            seq_len=dec_len, mods_row_fn=dec_row_proj)
        att = _attn_cache(lams, q, k, vt, cache_kt, cache_v_ctx, att_g, layer=l,
                          n_batch=n_dec, seq_len=dec_len, lambda_init=lambda_init)
        xs = _ffn(xs, att, mlp, mods3, g2, w_out_b, w_fi_b, w_fo_b, mods_row_fn=dec_row_ffn)

    state_k = jnp.swapaxes(jnp.concatenate(k_states, axis=1), -1, -2)
    state_v = jnp.concatenate(v_states, axis=1)
    return (xp.reshape(n_ctx, ctx_len, D_MODEL), xs.reshape(n_dec, dec_len, D_MODEL),
            state_k, state_v)
```

```python
import functools
import math

import jax
import jax.numpy as jnp
import numpy as np
from jax import lax
from jax.experimental import pallas as pl
from jax.experimental.pallas import tpu as pltpu

D_MODEL = 1024
ATT_WIDTH = 512
N_HEADS = 4
HEAD_DIM = 64
VAL_DIM = 128
MLP_WIDTH = 512
N_GROUPS = 4
GROUP_DIM = 128
CHUNK = 128
D_FF = 2816
IN_WIDTH = 2560
GRID_W = 64
ROPE_THETA = 10000.0
ROPE_AXIS_DIM = 32
EPS = 1e-6
LOG2E = 1.4426950408889634
LANES = 128

F32 = jnp.float32
BF16 = jnp.bfloat16

VMEM_LIMIT_BYTES = 56 * 1024 * 1024
MODS_ROWS = 16
CTX_ROW = 8

TM_PROJ = 512
TM_FFN = 512
TM_ATTN_CTX = 512
CTX_AHEAD = 3
FF_CHUNKS = ((0, 1024), (1024, 1024), (2048, 768))

ONES_ROWS = 16
TQ_UNIT = 256
KEY_BLOCK = 256
SCORE_LEAD = 2


def _const_spec(shape):
    zeros = (0,) * len(shape)
    return pl.BlockSpec(shape, lambda *_: zeros, pipeline_mode=pl.Buffered(1))


def _params(n_grid):
    return pltpu.CompilerParams(
        dimension_semantics=("arbitrary",) * n_grid,
        vmem_limit_bytes=VMEM_LIMIT_BYTES,
    )


def _rms_scale(x):
    return lax.rsqrt(jnp.mean(x * x, axis=-1, keepdims=True) + EPS)


def _mods_kernel(c_ref, cctx_ref, w_ref, b_ref, o_ref):
    cnd = jnp.concatenate(
        [c_ref[...], jnp.broadcast_to(cctx_ref[...], (MODS_ROWS - CTX_ROW, D_MODEL))], axis=0)
    act = (cnd * jax.nn.sigmoid(cnd)).astype(BF16)
    o_ref[...] = jnp.dot(act, w_ref[...].astype(BF16), preferred_element_type=F32) + b_ref[...]


def _mods(c, c_ctx, w_ada, b_ada):
    tn = 1536
    n_out = w_ada.shape[1]
    assert c.shape[0] == CTX_ROW
    return pl.pallas_call(
        _mods_kernel,
        grid=(n_out // tn,),
        in_specs=[
            _const_spec((CTX_ROW, D_MODEL)),
            _const_spec((1, D_MODEL)),
            pl.BlockSpec((D_MODEL, tn), lambda j: (0, j)),
            pl.BlockSpec((1, tn), lambda j: (0, j)),
        ],
        out_specs=pl.BlockSpec((MODS_ROWS, tn), lambda j: (0, j)),
        out_shape=jax.ShapeDtypeStruct((MODS_ROWS, n_out), F32),
        compiler_params=_params(1),
        name="mods",
    )(c, c_ctx, w_ada, b_ada)


def _proj_kernel(*refs, rope, seq_len, tm):
    it = iter(refs)
    x_ref, mods_ref, g1_ref, w_in_ref, qg_ref, kg_ref = (next(it) for _ in range(6))
    if rope:
        cos_ref, sin_ref = (next(it) for _ in range(2))
    sgun_ref, sguw_ref, sgub_ref, mlpg_ref = (next(it) for _ in range(4))
    qt_ref, k_ref, vt_ref, mlp_ref = (next(it) for _ in range(4))
    if not rope:
        kst_ref, vst_ref = (next(it) for _ in range(2))
    gate_ref = next(it)

    x = x_ref[...]
    xn = x * _rms_scale(x) * g1_ref[...]
    xm = xn * (1.0 + mods_ref[0, 1:2, :]) + mods_ref[0, 0:1, :]
    xb = xm.astype(BF16)

    def section(lo, hi):
        return jnp.dot(xb, w_in_ref[:, lo:hi], preferred_element_type=F32)

    def head_t(sec, h, g_ref):
        t = sec[:, h * LANES:(h + 1) * LANES].T
        maps = []
        for mp in range(2):
            tmap = t[mp * HEAD_DIM:(mp + 1) * HEAD_DIM, :]
            maps.append(tmap * lax.rsqrt(jnp.mean(tmap * tmap, axis=0, keepdims=True) + EPS))
        gain = jnp.concatenate([g_ref[...]] * (tm // LANES), axis=1)
        tn = jnp.concatenate(maps, axis=0) * gain
        if rope:
            swapped = jnp.concatenate(
                [tn[r0 + off:r0 + off + 16, :] for r0 in range(0, LANES, 32) for off in (16, 0)],
                axis=0)
            tn = tn * cos_ref[...] + swapped * sin_ref[...]
        return tn

    hq = section(0, ATT_WIDTH)
    hk = section(ATT_WIDTH, 2 * ATT_WIDTH)
    for h in range(N_HEADS):
        qt_ref[h] = head_t(hq, h, qg_ref).astype(BF16)
        kt = head_t(hk, h, kg_ref)
        k_ref[h] = kt.T.astype(BF16)
        if not rope:
            for s in range(tm // seq_len):
                for i in range(2):
                    kst_ref[s, 0, h, i, :, :] = kt[i * HEAD_DIM:(i + 1) * HEAD_DIM,
                                                   s * seq_len:(s + 1) * seq_len]

    hv = section(2 * ATT_WIDTH, 3 * ATT_WIDTH)
    hvt = hv.T.astype(BF16)
    for h in range(N_HEADS):
        vt_ref[h] = hvt[h * VAL_DIM:(h + 1) * VAL_DIM, :]
        if not rope:
            for s in range(tm // seq_len):
                vst_ref[s, 0, h, :, :] = hv[s * seq_len:(s + 1) * seq_len,
                                            h * VAL_DIM:(h + 1) * VAL_DIM]

    hu = section(3 * ATT_WIDTH, 3 * ATT_WIDTH + MLP_WIDTH)
    hg = section(3 * ATT_WIDTH + MLP_WIDTH, IN_WIDTH)
    for g in range(N_GROUPS):
        cols = slice(g * GROUP_DIM, (g + 1) * GROUP_DIM)
        gg = hg[:, cols]
        gn = (gg * _rms_scale(gg) * sgun_ref[:, cols]).astype(BF16)
        ug = hu[:, cols]
        wg = sguw_ref[g]
        bg = sgub_ref[g]
        for n in range(tm // CHUNK):
            rows = slice(n * CHUNK, (n + 1) * CHUNK)
            sp = jnp.dot(wg, gn[rows, :], preferred_element_type=F32) + bg
            gate_ref[rows, cols] = ug[rows, :] * sp
    o = gate_ref[...]
    mlp_ref[...] = (o * _rms_scale(o) * mlpg_ref[...]).astype(BF16)


def _proj(x2d, mods3, g1, w_in, qg, kg, rope_tabs, sgun, sguw, sgub, mlpg,
          *, seq_len, mods_row_fn):
    n_tok = x2d.shape[0]
    tm = TM_PROJ
    rope = rope_tabs is not None
    blocks_per_seq = seq_len // tm if rope else None

    in_specs = [
        pl.BlockSpec((tm, D_MODEL), lambda i: (i, 0)),
        pl.BlockSpec((1, 6, D_MODEL), lambda i: (mods_row_fn(i), 0, 0)),
        _const_spec((1, D_MODEL)),
        _const_spec((D_MODEL, IN_WIDTH)),
        _const_spec((LANES, LANES)),
        _const_spec((LANES, LANES)),
    ]
    args = [x2d, mods3, g1, w_in, qg, kg]
    if rope:
        tab_spec = pl.BlockSpec((LANES, tm), lambda i: (0, i % blocks_per_seq))
        in_specs += [tab_spec] * 2
        args += list(rope_tabs)
    in_specs += [
        _const_spec((1, MLP_WIDTH)),
        _const_spec((N_GROUPS, CHUNK, CHUNK)),
        _const_spec((N_GROUPS, CHUNK, GROUP_DIM)),
        _const_spec((1, MLP_WIDTH)),
    ]
    args += [sgun, sguw, sgub, mlpg]

    head_spec = pl.BlockSpec((N_HEADS, tm, LANES), lambda i: (0, i, 0))
    head_t_spec = pl.BlockSpec((N_HEADS, LANES, tm), lambda i: (0, 0, i))
    head_shape = jax.ShapeDtypeStruct((N_HEADS, n_tok, LANES), BF16)
    head_t_shape = jax.ShapeDtypeStruct((N_HEADS, LANES, n_tok), BF16)
    out_specs = [head_t_spec, head_spec, head_t_spec,
                 pl.BlockSpec((tm, MLP_WIDTH), lambda i: (i, 0))]
    out_shape = [head_t_shape, head_shape, head_t_shape,
                 jax.ShapeDtypeStruct((n_tok, MLP_WIDTH), BF16)]
    if not rope:
        n_seq = n_tok // seq_len
        spb = tm // seq_len
        out_specs += [
            pl.BlockSpec((spb, 1, N_HEADS, 2, HEAD_DIM, seq_len), lambda i: (i, 0, 0, 0, 0, 0)),
            pl.BlockSpec((spb, 1, N_HEADS, seq_len, VAL_DIM), lambda i: (i, 0, 0, 0, 0)),
        ]
        out_shape += [
            jax.ShapeDtypeStruct((n_seq, 1, N_HEADS, 2, HEAD_DIM, seq_len), F32),
            jax.ShapeDtypeStruct((n_seq, 1, N_HEADS, seq_len, VAL_DIM), F32),
        ]

    return pl.pallas_call(
        functools.partial(_proj_kernel, rope=rope, seq_len=seq_len, tm=tm),
        grid=(n_tok // tm,),
        in_specs=in_specs,
        out_specs=out_specs,
        out_shape=out_shape,
        scratch_shapes=[pltpu.VMEM((tm, MLP_WIDTH), F32)],
        compiler_params=_params(1),
        name="proj_rope" if rope else "proj_ctx",
    )(*args)


def _lambda_full(lq1, lk1, lq2, lk2, lambda_init):
    return (jnp.exp(jnp.sum(lq1[...] * lk1[...], keepdims=True))
            - jnp.exp(jnp.sum(lq2[...] * lk2[...], keepdims=True))
            + lambda_init)


def _map_queries(qt):
    row = lax.broadcasted_iota(jnp.int32, qt.shape, 0)
    zero = jnp.zeros_like(qt)
    return (jnp.where(row < HEAD_DIM, qt, zero), jnp.where(row >= HEAD_DIM, qt, zero))


def _combine_maps(o1, d1, o2, d2, lam, out_gain):
    ot = o1 * (1.0 / d1) - o2 * (lam / d2)
    ot = ot * lax.rsqrt(jnp.mean(ot * ot, axis=0, keepdims=True) + EPS)
    return (ot.T * out_gain).astype(BF16)


def _attn_ctx_kernel(lq1, lk1, lq2, lk2, qt_ref, k_ref, vt_ref, ag_ref, o_ref, st_buf,
                     *, lambda_init, seq_len):
    lam = _lambda_full(lq1, lk1, lq2, lk2, lambda_init)
    units = [(slice(s * seq_len, (s + 1) * seq_len), h)
             for s in range(k_ref.shape[1] // seq_len) for h in range(N_HEADS)]

    def scores(u):
        rows, h = units[u]
        kk = k_ref[h, rows, :]
        maxes = []
        for mp, qm in enumerate(_map_queries(qt_ref[h, :, rows])):
            st = jnp.dot(kk, qm, preferred_element_type=F32)
            st_buf[u % CTX_AHEAD, mp] = st
            maxes.append(jnp.max(st, axis=0, keepdims=True))
        return maxes

    def finish(u, maxes):
        rows, h = units[u]
        vt = vt_ref[h, :, rows]
        outs, dens = [], []
        for mp in range(2):
            e = jnp.exp2(st_buf[u % CTX_AHEAD, mp] - maxes[mp])
            dens.append(jnp.sum(e, axis=0, keepdims=True))
            outs.append(jnp.dot(vt, e.astype(BF16), preferred_element_type=F32))
        out_gain = (1.0 - lambda_init) * ag_ref[h]
        o_ref[h, rows, :] = _combine_maps(outs[0], dens[0], outs[1], dens[1], lam, out_gain)

    pending = [scores(u) for u in range(CTX_AHEAD - 1)]
    for u in range(len(units)):
        if u + CTX_AHEAD - 1 < len(units):
            pending.append(scores(u + CTX_AHEAD - 1))
        finish(u, pending.pop(0))


def _attn_ctx(lams, q, k, vt, att_g, *, seq_len, lambda_init):
    n_tok = k.shape[1]
    tm = TM_ATTN_CTX
    head_spec = pl.BlockSpec((N_HEADS, tm, LANES), lambda i: (0, i, 0))
    head_t_spec = pl.BlockSpec((N_HEADS, LANES, tm), lambda i: (0, 0, i))
    return pl.pallas_call(
        functools.partial(_attn_ctx_kernel, lambda_init=lambda_init, seq_len=seq_len),
        grid=(n_tok // tm,),
        in_specs=[_const_spec((1, HEAD_DIM))] * 4 + [
            head_t_spec,
            head_spec,
            head_t_spec,
            _const_spec((N_HEADS, 1, LANES)),
        ],
        out_specs=head_spec,
        out_shape=jax.ShapeDtypeStruct((N_HEADS, n_tok, LANES), BF16),
        scratch_shapes=[pltpu.VMEM((CTX_AHEAD, 2, seq_len, seq_len), F32)],
        compiler_params=_params(1),
        name="attn_ctx",
    )(*lams, q, k, vt, att_g)


def _attn_cache_kernel(lq1, lk1, lq2, lk2, qt_ref, k_ref, vt_ref, kct_ref, vc_ref, ag_ref,
                       *rest, lambda_init, n_new, n_weights):
    for w_ref, wb_ref in zip(rest[:n_weights], rest[n_weights + 1:2 * n_weights + 1]):
        wb_ref[...] = w_ref[...].astype(BF16)
    o_ref = rest[n_weights]
    k_all, vt_all, m_buf, acc_buf, *bufs = rest[2 * n_weights + 1:]
    st = (bufs[0:2], bufs[2:4])
    n_chunks = n_new // TQ_UNIT
    n_units = N_HEADS * n_chunks
    n_keys = k_all.shape[1]

    past = n_keys - n_new
    for h in range(N_HEADS):
        k_all[h, 0:n_new, :] = k_ref[h]
        k_all[h, n_new:, :] = kct_ref[0, 0, h].reshape(2 * HEAD_DIM, past).T.astype(BF16)
        vt_all[h, 0:VAL_DIM, 0:n_new] = vt_ref[h]
        vt_all[h, 0:VAL_DIM, n_new:] = vc_ref[0, 0, h].T.astype(BF16)
        vt_all[h, VAL_DIM:, :] = jnp.ones((ONES_ROWS, n_keys), BF16)

    lam = _lambda_full(lq1, lk1, lq2, lk2, lambda_init)

    def head_rows(u):
        c = u % n_chunks
        return u // n_chunks, pl.ds(pl.multiple_of(c * TQ_UNIT, TQ_UNIT), TQ_UNIT)

    def stage(fin, sc, defer_out=False):
        if sc is not None:
            sc_head, sc_rows = head_rows(sc[0])
            qms = _map_queries(qt_ref[sc_head, :, sc_rows])
            mrun = [None, None]
        if fin is not None:
            fin_head = fin[0] // n_chunks
            ms = [m_buf[fin[1], mp] for mp in range(2)]
            accs = [None, None]
        n_kb = n_keys // KEY_BLOCK
        lead = SCORE_LEAD if (sc is not None and fin is not None) else 0
        for step in range(n_kb + lead):
            if sc is not None and step < n_kb:
                kr = slice(step * KEY_BLOCK, (step + 1) * KEY_BLOCK)
                kk = k_all[sc_head, kr, :]
                for mp in range(2):
                    s = jnp.dot(kk, qms[mp], preferred_element_type=F32)
                    st[sc[1]][mp][kr, :] = s
                    smax = jnp.max(s.reshape(KEY_BLOCK // 8, 8, TQ_UNIT), axis=0)
                    mrun[mp] = smax if mrun[mp] is None else jnp.maximum(mrun[mp], smax)
            if fin is not None and step >= lead:
                kr = slice((step - lead) * KEY_BLOCK, (step - lead + 1) * KEY_BLOCK)
                vt = vt_all[fin_head, :, kr]
                for mp in range(2):
                    p = jnp.exp2(st[fin[1]][mp][kr, :] - ms[mp]).astype(BF16)
                    d = jnp.dot(vt, p, preferred_element_type=F32)
                    accs[mp] = d if accs[mp] is None else accs[mp] + d
        if sc is not None:
            for mp in range(2):
                m_buf[sc[1], mp] = jnp.max(mrun[mp], axis=0, keepdims=True)
        if fin is not None:
            if defer_out:
                for mp in range(2):
                    acc_buf[mp] = accs[mp]
            else:
                write_out(fin[0], accs[0], accs[1])

    def write_out(u, o1, o2):
        head, rows = head_rows(u)
        out_gain = (1.0 - lambda_init) * ag_ref[head]
        o_ref[head, rows, :] = _combine_maps(
            o1[0:VAL_DIM, :], o1[VAL_DIM:VAL_DIM + 1, :],
            o2[0:VAL_DIM, :], o2[VAL_DIM:VAL_DIM + 1, :], lam, out_gain)

    stage(None, (0, 0))
    stage((0, 0), (1, 1), defer_out=True)

    def pair(i, carry):
        u = 2 * i
        write_out(u - 2, acc_buf[0], acc_buf[1])
        stage((u - 1, 1), (u, 0))
        stage((u, 0), (u + 1, 1), defer_out=True)
        return carry

    lax.fori_loop(1, n_units // 2, pair, 0)
    write_out(n_units - 2, acc_buf[0], acc_buf[1])
    stage((n_units - 1, 1), None)


def _attn_cache(lams, q, k, vt, kct, vc, att_g, weights, *, layer, n_batch, seq_len, lambda_init):
    past = vc.shape[3]
    n_keys = seq_len + past
    assert (seq_len // TQ_UNIT) % 2 == 0 and seq_len // TQ_UNIT >= 4
    assert n_keys % KEY_BLOCK == 0
    head_spec = pl.BlockSpec((N_HEADS, seq_len, LANES), lambda b: (0, b, 0))
    head_t_spec = pl.BlockSpec((N_HEADS, LANES, seq_len), lambda b: (0, 0, b))
    in_specs = [_const_spec((1, HEAD_DIM))] * 4 + [
        head_t_spec,
        head_spec,
        head_t_spec,
        pl.BlockSpec((1, 1, N_HEADS, 2, HEAD_DIM, past), lambda b: (b, layer, 0, 0, 0, 0)),
        pl.BlockSpec((1, 1, N_HEADS, past, VAL_DIM), lambda b: (b, layer, 0, 0, 0)),
        _const_spec((N_HEADS, 1, LANES)),
    ]
    w_specs = []
    for w in weights:
        slab = w.shape[0] // n_batch
        assert w.shape[0] % n_batch == 0 and slab % 16 == 0
        w_specs.append(pl.BlockSpec((slab, w.shape[1]), lambda b: (b, 0)))
    return pl.pallas_call(
        functools.partial(_attn_cache_kernel, lambda_init=lambda_init, n_new=seq_len,
                          n_weights=len(weights)),
        grid=(n_batch,),
        in_specs=in_specs + w_specs,
        out_specs=[head_spec] + w_specs,
        out_shape=[jax.ShapeDtypeStruct((N_HEADS, n_batch * seq_len, LANES), BF16)]
                  + [jax.ShapeDtypeStruct(w.shape, BF16) for w in weights],
        scratch_shapes=([pltpu.VMEM((N_HEADS, n_keys, LANES), BF16),
                         pltpu.VMEM((N_HEADS, VAL_DIM + ONES_ROWS, n_keys), BF16),
                         pltpu.VMEM((2, 2, 1, TQ_UNIT), F32),
                         pltpu.VMEM((2, VAL_DIM + ONES_ROWS, TQ_UNIT), F32)]
                        + [pltpu.VMEM((n_keys, TQ_UNIT), F32)] * 4),
        compiler_params=_params(1),
        name="attn_cache",
    )(*lams, q, k, vt, kct, vc, att_g, *weights)


def _ffn_kernel(x_ref, att_ref, mlp_ref, mods_ref, g2_ref, wo_ref, wfi_ref, wfo_ref, o_ref):
    tm = x_ref.shape[0]
    halves = [slice(i * (tm // 2), (i + 1) * (tm // 2)) for i in range(2)]
    x1s, xbs = [], []
    for r in halves:
        att = jnp.concatenate([att_ref[h, r, :] for h in range(N_HEADS)], axis=1)
        y = (jnp.dot(att, wo_ref[0:ATT_WIDTH, :], preferred_element_type=F32)
             + jnp.dot(mlp_ref[r, :], wo_ref[ATT_WIDTH:, :], preferred_element_type=F32))
        x1 = x_ref[r, :] + mods_ref[0, 2:3, :] * y
        xn = x1 * _rms_scale(x1) * g2_ref[...]
        x1s.append(x1)
        xbs.append((xn * (1.0 + mods_ref[0, 4:5, :]) + mods_ref[0, 3:4, :]).astype(BF16))
    accs = [None, None]
    for c0, cw in FF_CHUNKS:
        pre = []
        for xb in xbs:
            gte = jnp.dot(xb, wfi_ref[:, c0:c0 + cw], preferred_element_type=F32)
            up = jnp.dot(xb, wfi_ref[:, D_FF + c0:D_FF + c0 + cw], preferred_element_type=F32)
            pre.append((gte, up))
        for i, (gte, up) in enumerate(pre):
            act = (gte * jax.nn.sigmoid(gte) * up).astype(BF16)
            part = jnp.dot(act, wfo_ref[c0:c0 + cw, :], preferred_element_type=F32)
            accs[i] = part if accs[i] is None else accs[i] + part
    for r, x1, acc in zip(halves, x1s, accs):
        o_ref[r, :] = x1 + mods_ref[0, 5:6, :] * acc


def _ffn(x2d, att, mlp, mods3, g2, w_out, w_ffn_in, w_ffn_out, *, mods_row_fn):
    n_tok = x2d.shape[0]
    tm = TM_FFN
    return pl.pallas_call(
        _ffn_kernel,
        grid=(n_tok // tm,),
        in_specs=[
            pl.BlockSpec((tm, D_MODEL), lambda i: (i, 0)),
            pl.BlockSpec((N_HEADS, tm, LANES), lambda i: (0, i, 0)),
            pl.BlockSpec((tm, MLP_WIDTH), lambda i: (i, 0)),
            pl.BlockSpec((1, 6, D_MODEL), lambda i: (mods_row_fn(i), 0, 0)),
            _const_spec((1, D_MODEL)),
            _const_spec((D_MODEL, D_MODEL)),
            _const_spec((D_MODEL, 2 * D_FF)),
            _const_spec((D_FF, D_MODEL)),
        ],
        out_specs=pl.BlockSpec((tm, D_MODEL), lambda i: (i, 0)),
        out_shape=jax.ShapeDtypeStruct((n_tok, D_MODEL), F32),
        compiler_params=_params(1),
        name="ffn",
    )(x2d, att, mlp, mods3, g2, w_out, w_ffn_in, w_ffn_out)


def _rope_tables(n):
    pos = np.arange(n)
    row = (pos // GRID_W).astype(np.float32)
    col = (pos % GRID_W).astype(np.float32)
    inv = (ROPE_THETA ** (-np.arange(0, ROPE_AXIS_DIM, 2, dtype=np.float32) / ROPE_AXIS_DIM)
           ).astype(np.float32)
    ang_r = row[:, None] * inv[None, :]
    ang_c = col[:, None] * inv[None, :]
    cos64 = np.concatenate([np.cos(ang_r)] * 2 + [np.cos(ang_c)] * 2, axis=1)
    sin64 = np.concatenate([-np.sin(ang_r), np.sin(ang_r), -np.sin(ang_c), np.sin(ang_c)], axis=1)
    return (np.ascontiguousarray(np.tile(cos64, (1, 2)).T, np.float32),
            np.ascontiguousarray(np.tile(sin64, (1, 2)).T, np.float32))


def kernel(x_prompt, x_sample, cache_k_ctx, cache_v_ctx, c, c_ctx, norm1_g, norm2_g, w_ada, b_ada, w_in, q_norm_g, k_norm_g, lambda_q1, lambda_k1, lambda_q2, lambda_k2, att_out_g, sgu_norm_g, sgu_w, sgu_b, mlp_out_g, w_out, w_ffn_in, w_ffn_out):
    n_ctx, ctx_len, _ = x_prompt.shape
    n_dec, dec_len, _ = x_sample.shape
    depth = norm1_g.shape[0]

    rope_tabs = _rope_tables(dec_len)
    q_scale = LOG2E / math.sqrt(HEAD_DIM)

    xp = x_prompt.reshape(n_ctx * ctx_len, D_MODEL)
    xs = x_sample.reshape(n_dec * dec_len, D_MODEL)
    cache_kt = jnp.swapaxes(cache_k_ctx, -1, -2)
    k_states, v_states = [], []
    ctx_row = lambda i: CTX_ROW
    dec_row_proj = lambda i: i // (dec_len // TM_PROJ)
    dec_row_ffn = lambda i: i // (dec_len // TM_FFN)

    for l in range(depth):
        lambda_init = 0.8 - 0.6 * math.exp(-0.3 * l)
        mods3 = _mods(c, c_ctx[None, :], w_ada[l], b_ada[l][None, :]).reshape(MODS_ROWS, 6, D_MODEL)
        g1 = norm1_g[l][None, :]
        g2 = norm2_g[l][None, :]
        w_in_b = w_in[l].astype(BF16)
        qg = jnp.broadcast_to(jnp.tile(q_norm_g[l] * q_scale, 2)[:, None], (LANES, LANES))
        kg = jnp.broadcast_to(jnp.tile(k_norm_g[l], 2)[:, None], (LANES, LANES))
        sgun = sgu_norm_g[l][None, :]
        sguw = sgu_w[l].astype(BF16)
        sgub = jnp.broadcast_to(sgu_b[l][:, :, None], (N_GROUPS, CHUNK, GROUP_DIM))
        mlpg = mlp_out_g[l][None, :]
        att_g = att_out_g[l].reshape(N_HEADS, 1, VAL_DIM)
        lams = (lambda_q1[l][None, :], lambda_k1[l][None, :],
                lambda_q2[l][None, :], lambda_k2[l][None, :])

        q, k, vt, mlp_s = _proj(
            xs, mods3, g1, w_in_b, qg, kg, rope_tabs, sgun, sguw, sgub, mlpg,
            seq_len=dec_len, mods_row_fn=dec_row_proj)
        att_s, w_out_b, w_fi_b, w_fo_b = _attn_cache(
            lams, q, k, vt, cache_kt, cache_v_ctx, att_g, (w_out[l], w_ffn_in[l], w_ffn_out[l]),
            layer=l, n_batch=n_dec, seq_len=dec_len, lambda_init=lambda_init)

        q, k, vt, mlp, k_c, v_c = _proj(
            xp, mods3, g1, w_in_b, qg, kg, None, sgun, sguw, sgub, mlpg,
            seq_len=ctx_len, mods_row_fn=ctx_row)
        att = _attn_ctx(lams, q, k, vt, att_g, seq_len=ctx_len, lambda_init=lambda_init)
        xp = _ffn(xp, att, mlp, mods3, g2, w_out_b, w_fi_b, w_fo_b, mods_row_fn=ctx_row)
        k_states.append(k_c)
        v_states.append(v_c)

        xs = _ffn(xs, att_s, mlp_s, mods3, g2, w_out_b, w_fi_b, w_fo_b, mods_row_fn=dec_row_ffn)

    state_k = jnp.swapaxes(jnp.concatenate(k_states, axis=1), -1, -2)
    state_v = jnp.concatenate(v_states, axis=1)
    return (xp.reshape(n_ctx, ctx_len, D_MODEL), xs.reshape(n_dec, dec_len, D_MODEL),
            state_k, state_v)
```

```python
import functools
import math

import jax
import jax.numpy as jnp
import numpy as np
from jax import lax
from jax.experimental import pallas as pl
from jax.experimental.pallas import tpu as pltpu

D_MODEL = 1024
ATT_WIDTH = 512
N_HEADS = 4
HEAD_DIM = 64
VAL_DIM = 128
MLP_WIDTH = 512
N_GROUPS = 4
GROUP_DIM = 128
CHUNK = 128
D_FF = 2816
IN_WIDTH = 2560
GRID_W = 64
ROPE_THETA = 10000.0
ROPE_AXIS_DIM = 32
EPS = 1e-6
LOG2E = 1.4426950408889634
LANES = 128

F32 = jnp.float32
BF16 = jnp.bfloat16

VMEM_LIMIT_BYTES = 56 * 1024 * 1024
MODS_ROWS = 16
CTX_ROW = 8

TM_PROJ = 512
TM_FFN = 1024
TM_ATTN_CTX = 512
CTX_AHEAD = 3
FF_CHUNKS = ((0, 512), (512, 512), (1024, 512), (1536, 512), (2048, 512), (2560, 256))

ONES_ROWS = 16
TQ_UNIT = 256
KEY_BLOCK = 256
SCORE_LEAD = 2


def _const_spec(shape):
    zeros = (0,) * len(shape)
    return pl.BlockSpec(shape, lambda *_: zeros, pipeline_mode=pl.Buffered(1))


def _params(n_grid):
    return pltpu.CompilerParams(
        dimension_semantics=("arbitrary",) * n_grid,
        vmem_limit_bytes=VMEM_LIMIT_BYTES,
    )


def _rms_scale(x):
    return lax.rsqrt(jnp.mean(x * x, axis=-1, keepdims=True) + EPS)


def _mod(mods_ref, row, k):
    return mods_ref[pl.ds(row, 1), k * D_MODEL:(k + 1) * D_MODEL]


def _mods_kernel(c_ref, cctx_ref, w_ref, b_ref, o_ref):
    cnd = jnp.concatenate(
        [c_ref[...], jnp.broadcast_to(cctx_ref[...], (MODS_ROWS - CTX_ROW, D_MODEL))], axis=0)
    act = (cnd * jax.nn.sigmoid(cnd)).astype(BF16)
    o_ref[...] = jnp.dot(act, w_ref[...].astype(BF16), preferred_element_type=F32) + b_ref[...]


def _mods(c, c_ctx, w_ada, b_ada):
    tn = 1536
    n_out = w_ada.shape[1]
    assert c.shape[0] == CTX_ROW
    return pl.pallas_call(
        _mods_kernel,
        grid=(n_out // tn,),
        in_specs=[
            _const_spec((CTX_ROW, D_MODEL)),
            _const_spec((1, D_MODEL)),
            pl.BlockSpec((D_MODEL, tn), lambda j: (0, j)),
            pl.BlockSpec((1, tn), lambda j: (0, j)),
        ],
        out_specs=pl.BlockSpec((MODS_ROWS, tn), lambda j: (0, j)),
        out_shape=jax.ShapeDtypeStruct((MODS_ROWS, n_out), F32),
        compiler_params=_params(1),
        name="mods",
    )(c, c_ctx, w_ada, b_ada)


def _proj_kernel(*refs, rope, seq_len, tm, mods_row_fn):
    it = iter(refs)
    x_ref, mods_ref, g1_ref, w_in_ref, qg_ref, kg_ref = (next(it) for _ in range(6))
    if rope:
        cos_ref, sin_ref = (next(it) for _ in range(2))
    sgun_ref, sguw_ref, sgub_ref, mlpg_ref = (next(it) for _ in range(4))
    qt_ref, k_ref, vt_ref, mlp_ref = (next(it) for _ in range(4))
    if not rope:
        kst_ref, vst_ref = (next(it) for _ in range(2))
    gate_ref = next(it)

    x = x_ref[...]
    xn = x * _rms_scale(x) * g1_ref[...]
    row = mods_row_fn(pl.program_id(0))
    xm = xn * (1.0 + _mod(mods_ref, row, 1)) + _mod(mods_ref, row, 0)
    xb = xm.astype(BF16)

    def section(lo, hi):
        return jnp.dot(xb, w_in_ref[:, lo:hi], preferred_element_type=F32)

    def head_t(sec, h, g_ref):
        t = sec[:, h * LANES:(h + 1) * LANES].T
        maps = []
        for mp in range(2):
            tmap = t[mp * HEAD_DIM:(mp + 1) * HEAD_DIM, :]
            maps.append(tmap * lax.rsqrt(jnp.mean(tmap * tmap, axis=0, keepdims=True) + EPS))
        gain = jnp.concatenate([g_ref[...]] * (tm // LANES), axis=1)
        tn = jnp.concatenate(maps, axis=0) * gain
        if rope:
            swapped = jnp.concatenate(
                [tn[r0 + off:r0 + off + 16, :] for r0 in range(0, LANES, 32) for off in (16, 0)],
                axis=0)
            tn = tn * cos_ref[...] + swapped * sin_ref[...]
        return tn

    hq = section(0, ATT_WIDTH)
    hk = section(ATT_WIDTH, 2 * ATT_WIDTH)
    for h in range(N_HEADS):
        qt_ref[h] = head_t(hq, h, qg_ref).astype(BF16)
        kt = head_t(hk, h, kg_ref)
        k_ref[h] = kt.T.astype(BF16)
        if not rope:
            for s in range(tm // seq_len):
                for i in range(2):
                    kst_ref[s, 0, h, i, :, :] = kt[i * HEAD_DIM:(i + 1) * HEAD_DIM,
                                                   s * seq_len:(s + 1) * seq_len]

    hv = section(2 * ATT_WIDTH, 3 * ATT_WIDTH)
    hvt = hv.T.astype(BF16)
    for h in range(N_HEADS):
        vt_ref[h] = hvt[h * VAL_DIM:(h + 1) * VAL_DIM, :]
        if not rope:
            for s in range(tm // seq_len):
                vst_ref[s, 0, h, :, :] = hv[s * seq_len:(s + 1) * seq_len,
                                            h * VAL_DIM:(h + 1) * VAL_DIM]

    hu = section(3 * ATT_WIDTH, 3 * ATT_WIDTH + MLP_WIDTH)
    hg = section(3 * ATT_WIDTH + MLP_WIDTH, IN_WIDTH)
    for g in range(N_GROUPS):
        cols = slice(g * GROUP_DIM, (g + 1) * GROUP_DIM)
        gg = hg[:, cols]
        gn = (gg * _rms_scale(gg) * sgun_ref[:, cols]).astype(BF16)
        ug = hu[:, cols]
        wg = sguw_ref[g]
        bg = sgub_ref[g]
        for n in range(tm // CHUNK):
            rows = slice(n * CHUNK, (n + 1) * CHUNK)
            sp = jnp.dot(wg, gn[rows, :], preferred_element_type=F32) + bg
            gate_ref[rows, cols] = ug[rows, :] * sp
    o = gate_ref[...]
    mlp_ref[...] = (o * _rms_scale(o) * mlpg_ref[...]).astype(BF16)


def _proj(x2d, mods3, g1, w_in, qg, kg, rope_tabs, sgun, sguw, sgub, mlpg,
          *, seq_len, mods_row_fn):
    n_tok = x2d.shape[0]
    tm = TM_PROJ
    rope = rope_tabs is not None
    blocks_per_seq = seq_len // tm if rope else None

    in_specs = [
        pl.BlockSpec((tm, D_MODEL), lambda i: (i, 0)),
        _const_spec((MODS_ROWS, 6 * D_MODEL)),
        _const_spec((1, D_MODEL)),
        _const_spec((D_MODEL, IN_WIDTH)),
        _const_spec((LANES, LANES)),
        _const_spec((LANES, LANES)),
    ]
    args = [x2d, mods3, g1, w_in, qg, kg]
    if rope:
        tab_spec = pl.BlockSpec((LANES, tm), lambda i: (0, i % blocks_per_seq))
        in_specs += [tab_spec] * 2
        args += list(rope_tabs)
    in_specs += [
        _const_spec((1, MLP_WIDTH)),
        _const_spec((N_GROUPS, CHUNK, CHUNK)),
        _const_spec((N_GROUPS, CHUNK, GROUP_DIM)),
        _const_spec((1, MLP_WIDTH)),
    ]
    args += [sgun, sguw, sgub, mlpg]

    head_spec = pl.BlockSpec((N_HEADS, tm, LANES), lambda i: (0, i, 0))
    head_t_spec = pl.BlockSpec((N_HEADS, LANES, tm), lambda i: (0, 0, i))
    head_shape = jax.ShapeDtypeStruct((N_HEADS, n_tok, LANES), BF16)
    head_t_shape = jax.ShapeDtypeStruct((N_HEADS, LANES, n_tok), BF16)
    out_specs = [head_t_spec, head_spec, head_t_spec,
                 pl.BlockSpec((tm, MLP_WIDTH), lambda i: (i, 0))]
    out_shape = [head_t_shape, head_shape, head_t_shape,
                 jax.ShapeDtypeStruct((n_tok, MLP_WIDTH), BF16)]
    if not rope:
        n_seq = n_tok // seq_len
        spb = tm // seq_len
        out_specs += [
            pl.BlockSpec((spb, 1, N_HEADS, 2, HEAD_DIM, seq_len), lambda i: (i, 0, 0, 0, 0, 0)),
            pl.BlockSpec((spb, 1, N_HEADS, seq_len, VAL_DIM), lambda i: (i, 0, 0, 0, 0)),
        ]
        out_shape += [
            jax.ShapeDtypeStruct((n_seq, 1, N_HEADS, 2, HEAD_DIM, seq_len), F32),
            jax.ShapeDtypeStruct((n_seq, 1, N_HEADS, seq_len, VAL_DIM), F32),
        ]

    return pl.pallas_call(
        functools.partial(_proj_kernel, rope=rope, seq_len=seq_len, tm=tm, mods_row_fn=mods_row_fn),
        grid=(n_tok // tm,),
        in_specs=in_specs,
        out_specs=out_specs,
        out_shape=out_shape,
        scratch_shapes=[pltpu.VMEM((tm, MLP_WIDTH), F32)],
        compiler_params=_params(1),
        name="proj_rope" if rope else "proj_ctx",
    )(*args)


def _lambda_full(lq1, lk1, lq2, lk2, lambda_init):
    return (jnp.exp(jnp.sum(lq1[...] * lk1[...], keepdims=True))
            - jnp.exp(jnp.sum(lq2[...] * lk2[...], keepdims=True))
            + lambda_init)


def _map_queries(qt):
    row = lax.broadcasted_iota(jnp.int32, qt.shape, 0)
    zero = jnp.zeros_like(qt)
    return (jnp.where(row < HEAD_DIM, qt, zero), jnp.where(row >= HEAD_DIM, qt, zero))


def _combine_maps(o1, d1, o2, d2, lam, out_gain):
    ot = o1 * (1.0 / d1) - o2 * (lam / d2)
    ot = ot * lax.rsqrt(jnp.mean(ot * ot, axis=0, keepdims=True) + EPS)
    return (ot.T * out_gain).astype(BF16)


def _attn_ctx_kernel(lq1, lk1, lq2, lk2, qt_ref, k_ref, vt_ref, ag_ref, o_ref, st_buf,
                     *, lambda_init, seq_len):
    lam = _lambda_full(lq1, lk1, lq2, lk2, lambda_init)
    units = [(slice(s * seq_len, (s + 1) * seq_len), h)
             for s in range(k_ref.shape[1] // seq_len) for h in range(N_HEADS)]

    def scores(u):
        rows, h = units[u]
        kk = k_ref[h, rows, :]
        maxes = []
        for mp, qm in enumerate(_map_queries(qt_ref[h, :, rows])):
            st = jnp.dot(kk, qm, preferred_element_type=F32)
            st_buf[u % CTX_AHEAD, mp] = st
            maxes.append(jnp.max(st, axis=0, keepdims=True))
        return maxes

    def finish(u, maxes):
        rows, h = units[u]
        vt = vt_ref[h, :, rows]
        outs, dens = [], []
        for mp in range(2):
            e = jnp.exp2(st_buf[u % CTX_AHEAD, mp] - maxes[mp])
            dens.append(jnp.sum(e, axis=0, keepdims=True))
            outs.append(jnp.dot(vt, e.astype(BF16), preferred_element_type=F32))
        out_gain = (1.0 - lambda_init) * ag_ref[h]
        o_ref[h, rows, :] = _combine_maps(outs[0], dens[0], outs[1], dens[1], lam, out_gain)

    pending = [scores(u) for u in range(CTX_AHEAD - 1)]
    for u in range(len(units)):
        if u + CTX_AHEAD - 1 < len(units):
            pending.append(scores(u + CTX_AHEAD - 1))
        finish(u, pending.pop(0))


def _attn_ctx(lams, q, k, vt, att_g, *, seq_len, lambda_init):
    n_tok = k.shape[1]
    tm = TM_ATTN_CTX
    head_spec = pl.BlockSpec((N_HEADS, tm, LANES), lambda i: (0, i, 0))
    head_t_spec = pl.BlockSpec((N_HEADS, LANES, tm), lambda i: (0, 0, i))
    return pl.pallas_call(
        functools.partial(_attn_ctx_kernel, lambda_init=lambda_init, seq_len=seq_len),
        grid=(n_tok // tm,),
        in_specs=[_const_spec((1, HEAD_DIM))] * 4 + [
            head_t_spec,
            head_spec,
            head_t_spec,
            _const_spec((N_HEADS, 1, LANES)),
        ],
        out_specs=head_spec,
        out_shape=jax.ShapeDtypeStruct((N_HEADS, n_tok, LANES), BF16),
        scratch_shapes=[pltpu.VMEM((CTX_AHEAD, 2, seq_len, seq_len), F32)],
        compiler_params=_params(1),
        name="attn_ctx",
    )(*lams, q, k, vt, att_g)


def _attn_cache_kernel(lq1, lk1, lq2, lk2, qt_ref, k_ref, vt_ref, kct_ref, vc_ref, ag_ref,
                       *rest, lambda_init, n_new, n_weights):
    for w_ref, wb_ref in zip(rest[:n_weights], rest[n_weights + 1:2 * n_weights + 1]):
        wb_ref[...] = w_ref[...].astype(BF16)
    o_ref = rest[n_weights]
    k_all, vt_all, m_buf, acc_buf, *bufs = rest[2 * n_weights + 1:]
    st = (bufs[0:2], bufs[2:4])
    n_chunks = n_new // TQ_UNIT
    n_units = N_HEADS * n_chunks
    n_keys = k_all.shape[1]

    past = n_keys - n_new
    for h in range(N_HEADS):
        k_all[h, 0:n_new, :] = k_ref[h]
        k_all[h, n_new:, :] = kct_ref[0, 0, h].reshape(2 * HEAD_DIM, past).T.astype(BF16)
        vt_all[h, 0:VAL_DIM, 0:n_new] = vt_ref[h]
        vt_all[h, 0:VAL_DIM, n_new:] = vc_ref[0, 0, h].T.astype(BF16)
        vt_all[h, VAL_DIM:, :] = jnp.ones((ONES_ROWS, n_keys), BF16)

    lam = _lambda_full(lq1, lk1, lq2, lk2, lambda_init)

    def head_rows(u):
        c = u % n_chunks
        return u // n_chunks, pl.ds(pl.multiple_of(c * TQ_UNIT, TQ_UNIT), TQ_UNIT)

    def stage(fin, sc, defer_out=False):
        if sc is not None:
            sc_head, sc_rows = head_rows(sc[0])
            qms = _map_queries(qt_ref[sc_head, :, sc_rows])
            mrun = [None, None]
        if fin is not None:
            fin_head = fin[0] // n_chunks
            ms = [m_buf[fin[1], mp] for mp in range(2)]
            accs = [None, None]
        n_kb = n_keys // KEY_BLOCK
        lead = SCORE_LEAD if (sc is not None and fin is not None) else 0
        for step in range(n_kb + lead):
            if sc is not None and step < n_kb:
                kr = slice(step * KEY_BLOCK, (step + 1) * KEY_BLOCK)
                kk = k_all[sc_head, kr, :]
                for mp in range(2):
                    s = jnp.dot(kk, qms[mp], preferred_element_type=F32)
                    st[sc[1]][mp][kr, :] = s
                    smax = jnp.max(s.reshape(KEY_BLOCK // 8, 8, TQ_UNIT), axis=0)
                    mrun[mp] = smax if mrun[mp] is None else jnp.maximum(mrun[mp], smax)
            if fin is not None and step >= lead:
                kr = slice((step - lead) * KEY_BLOCK, (step - lead + 1) * KEY_BLOCK)
                vt = vt_all[fin_head, :, kr]
                for mp in range(2):
                    p = jnp.exp2(st[fin[1]][mp][kr, :] - ms[mp]).astype(BF16)
                    d = jnp.dot(vt, p, preferred_element_type=F32)
                    accs[mp] = d if accs[mp] is None else accs[mp] + d
        if sc is not None:
            for mp in range(2):
                m_buf[sc[1], mp] = jnp.max(mrun[mp], axis=0, keepdims=True)
        if fin is not None:
            if defer_out:
                for mp in range(2):
                    acc_buf[mp] = accs[mp]
            else:
                write_out(fin[0], accs[0], accs[1])

    def write_out(u, o1, o2):
        head, rows = head_rows(u)
        out_gain = (1.0 - lambda_init) * ag_ref[head]
        o_ref[head, rows, :] = _combine_maps(
            o1[0:VAL_DIM, :], o1[VAL_DIM:VAL_DIM + 1, :],
            o2[0:VAL_DIM, :], o2[VAL_DIM:VAL_DIM + 1, :], lam, out_gain)

    stage(None, (0, 0))
    stage((0, 0), (1, 1), defer_out=True)

    def pair(i, carry):
        u = 2 * i
        write_out(u - 2, acc_buf[0], acc_buf[1])
        stage((u - 1, 1), (u, 0))
        stage((u, 0), (u + 1, 1), defer_out=True)
        return carry

    lax.fori_loop(1, n_units // 2, pair, 0)
    write_out(n_units - 2, acc_buf[0], acc_buf[1])
    stage((n_units - 1, 1), None)


def _attn_cache(lams, q, k, vt, kct, vc, att_g, weights, *, layer, n_batch, seq_len, lambda_init):
    past = vc.shape[3]
    n_keys = seq_len + past
    assert (seq_len // TQ_UNIT) % 2 == 0 and seq_len // TQ_UNIT >= 4
    assert n_keys % KEY_BLOCK == 0
    head_spec = pl.BlockSpec((N_HEADS, seq_len, LANES), lambda b: (0, b, 0))
    head_t_spec = pl.BlockSpec((N_HEADS, LANES, seq_len), lambda b: (0, 0, b))
    in_specs = [_const_spec((1, HEAD_DIM))] * 4 + [
        head_t_spec,
        head_spec,
        head_t_spec,
        pl.BlockSpec((1, 1, N_HEADS, 2, HEAD_DIM, past), lambda b: (b, layer, 0, 0, 0, 0)),
        pl.BlockSpec((1, 1, N_HEADS, past, VAL_DIM), lambda b: (b, layer, 0, 0, 0)),
        _const_spec((N_HEADS, 1, LANES)),
    ]
    w_specs = []
    for w in weights:
        slab = w.shape[0] // n_batch
        assert w.shape[0] % n_batch == 0 and slab % 16 == 0
        w_specs.append(pl.BlockSpec((slab, w.shape[1]), lambda b: (b, 0)))
    return pl.pallas_call(
        functools.partial(_attn_cache_kernel, lambda_init=lambda_init, n_new=seq_len,
                          n_weights=len(weights)),
        grid=(n_batch,),
        in_specs=in_specs + w_specs,
        out_specs=[head_spec] + w_specs,
        out_shape=[jax.ShapeDtypeStruct((N_HEADS, n_batch * seq_len, LANES), BF16)]
                  + [jax.ShapeDtypeStruct(w.shape, BF16) for w in weights],
        scratch_shapes=([pltpu.VMEM((N_HEADS, n_keys, LANES), BF16),
                         pltpu.VMEM((N_HEADS, VAL_DIM + ONES_ROWS, n_keys), BF16),
                         pltpu.VMEM((2, 2, 1, TQ_UNIT), F32),
                         pltpu.VMEM((2, VAL_DIM + ONES_ROWS, TQ_UNIT), F32)]
                        + [pltpu.VMEM((n_keys, TQ_UNIT), F32)] * 4),
        compiler_params=_params(1),
        name="attn_cache",
    )(*lams, q, k, vt, kct, vc, att_g, *weights)


def _ffn_kernel(x_ref, att_ref, mlp_ref, mods_ref, g2_ref, wo_ref, wfi_ref, wfo_ref, o_ref,
                *, mods_row_fn):
    row = mods_row_fn(pl.program_id(0))
    gate1, shift2, scale2, gate2 = (_mod(mods_ref, row, k) for k in (2, 3, 4, 5))
    tm = x_ref.shape[0]
    halves = [slice(i * (tm // 2), (i + 1) * (tm // 2)) for i in range(2)]
    x1s, xbs = [], []
    for r in halves:
        att = jnp.concatenate([att_ref[h, r, :] for h in range(N_HEADS)], axis=1)
        y = (jnp.dot(att, wo_ref[0:ATT_WIDTH, :], preferred_element_type=F32)
             + jnp.dot(mlp_ref[r, :], wo_ref[ATT_WIDTH:, :], preferred_element_type=F32))
        x1 = x_ref[r, :] + gate1 * y
        xn = x1 * _rms_scale(x1) * g2_ref[...]
        x1s.append(x1)
        xbs.append((xn * (1.0 + scale2) + shift2).astype(BF16))
    accs = [None, None]
    for c0, cw in FF_CHUNKS:
        pre = []
        for xb in xbs:
            gte = jnp.dot(xb, wfi_ref[:, c0:c0 + cw], preferred_element_type=F32)
            up = jnp.dot(xb, wfi_ref[:, D_FF + c0:D_FF + c0 + cw], preferred_element_type=F32)
            pre.append((gte, up))
        for i, (gte, up) in enumerate(pre):
            act = (gte * jax.nn.sigmoid(gte) * up).astype(BF16)
            part = jnp.dot(act, wfo_ref[c0:c0 + cw, :], preferred_element_type=F32)
            accs[i] = part if accs[i] is None else accs[i] + part
    for r, x1, acc in zip(halves, x1s, accs):
        o_ref[r, :] = x1 + gate2 * acc


def _ffn(x2d, att, mlp, mods3, g2, w_out, w_ffn_in, w_ffn_out, *, mods_row_fn):
    n_tok = x2d.shape[0]
    tm = TM_FFN
    return pl.pallas_call(
        functools.partial(_ffn_kernel, mods_row_fn=mods_row_fn),
        grid=(n_tok // tm,),
        in_specs=[
            pl.BlockSpec((tm, D_MODEL), lambda i: (i, 0)),
            pl.BlockSpec((N_HEADS, tm, LANES), lambda i: (0, i, 0)),
            pl.BlockSpec((tm, MLP_WIDTH), lambda i: (i, 0)),
            _const_spec((MODS_ROWS, 6 * D_MODEL)),
            _const_spec((1, D_MODEL)),
            _const_spec((D_MODEL, D_MODEL)),
            _const_spec((D_MODEL, 2 * D_FF)),
            _const_spec((D_FF, D_MODEL)),
        ],
        out_specs=pl.BlockSpec((tm, D_MODEL), lambda i: (i, 0)),
        out_shape=jax.ShapeDtypeStruct((n_tok, D_MODEL), F32),
        compiler_params=_params(1),
        name="ffn",
    )(x2d, att, mlp, mods3, g2, w_out, w_ffn_in, w_ffn_out)


def _rope_tables(n):
    pos = np.arange(n)
    row = (pos // GRID_W).astype(np.float32)
    col = (pos % GRID_W).astype(np.float32)
    inv = (ROPE_THETA ** (-np.arange(0, ROPE_AXIS_DIM, 2, dtype=np.float32) / ROPE_AXIS_DIM)
           ).astype(np.float32)
    ang_r = row[:, None] * inv[None, :]
    ang_c = col[:, None] * inv[None, :]
    cos64 = np.concatenate([np.cos(ang_r)] * 2 + [np.cos(ang_c)] * 2, axis=1)
    sin64 = np.concatenate([-np.sin(ang_r), np.sin(ang_r), -np.sin(ang_c), np.sin(ang_c)], axis=1)
    return (np.ascontiguousarray(np.tile(cos64, (1, 2)).T, np.float32),
            np.ascontiguousarray(np.tile(sin64, (1, 2)).T, np.float32))


def kernel(x_prompt, x_sample, cache_k_ctx, cache_v_ctx, c, c_ctx, norm1_g, norm2_g, w_ada, b_ada, w_in, q_norm_g, k_norm_g, lambda_q1, lambda_k1, lambda_q2, lambda_k2, att_out_g, sgu_norm_g, sgu_w, sgu_b, mlp_out_g, w_out, w_ffn_in, w_ffn_out):
    n_ctx, ctx_len, _ = x_prompt.shape
    n_dec, dec_len, _ = x_sample.shape
    depth = norm1_g.shape[0]

    rope_tabs = _rope_tables(dec_len)
    q_scale = LOG2E / math.sqrt(HEAD_DIM)

    xp = x_prompt.reshape(n_ctx * ctx_len, D_MODEL)
    xs = x_sample.reshape(n_dec * dec_len, D_MODEL)
    cache_kt = jnp.swapaxes(cache_k_ctx, -1, -2)
    k_states, v_states = [], []
    ctx_row = lambda i: CTX_ROW
    dec_row_proj = lambda i: i // (dec_len // TM_PROJ)
    dec_row_ffn = lambda i: i // (dec_len // TM_FFN)

    for l in range(depth):
        lambda_init = 0.8 - 0.6 * math.exp(-0.3 * l)
        mods3 = _mods(c, c_ctx[None, :], w_ada[l], b_ada[l][None, :])
        g1 = norm1_g[l][None, :]
        g2 = norm2_g[l][None, :]
        w_in_b = w_in[l].astype(BF16)
        qg = jnp.broadcast_to(jnp.tile(q_norm_g[l] * q_scale, 2)[:, None], (LANES, LANES))
        kg = jnp.broadcast_to(jnp.tile(k_norm_g[l], 2)[:, None], (LANES, LANES))
        sgun = sgu_norm_g[l][None, :]
        sguw = sgu_w[l].astype(BF16)
        sgub = jnp.broadcast_to(sgu_b[l][:, :, None], (N_GROUPS, CHUNK, GROUP_DIM))
        mlpg = mlp_out_g[l][None, :]
        att_g = att_out_g[l].reshape(N_HEADS, 1, VAL_DIM)
        lams = (lambda_q1[l][None, :], lambda_k1[l][None, :],
                lambda_q2[l][None, :], lambda_k2[l][None, :])

        q, k, vt, mlp_s = _proj(
            xs, mods3, g1, w_in_b, qg, kg, rope_tabs, sgun, sguw, sgub, mlpg,
            seq_len=dec_len, mods_row_fn=dec_row_proj)
        att_s, w_out_b, w_fi_b, w_fo_b = _attn_cache(
            lams, q, k, vt, cache_kt, cache_v_ctx, att_g, (w_out[l], w_ffn_in[l], w_ffn_out[l]),
            layer=l, n_batch=n_dec, seq_len=dec_len, lambda_init=lambda_init)

        q, k, vt, mlp, k_c, v_c = _proj(
            xp, mods3, g1, w_in_b, qg, kg, None, sgun, sguw, sgub, mlpg,
            seq_len=ctx_len, mods_row_fn=ctx_row)
        att = _attn_ctx(lams, q, k, vt, att_g, seq_len=ctx_len, lambda_init=lambda_init)
        xp = _ffn(xp, att, mlp, mods3, g2, w_out_b, w_fi_b, w_fo_b, mods_row_fn=ctx_row)
        k_states.append(k_c)
        v_states.append(v_c)

        xs = _ffn(xs, att_s, mlp_s, mods3, g2, w_out_b, w_fi_b, w_fo_b, mods_row_fn=dec_row_ffn)

    state_k = jnp.swapaxes(jnp.concatenate(k_states, axis=1), -1, -2)
    state_v = jnp.concatenate(v_states, axis=1)
    return (xp.reshape(n_ctx, ctx_len, D_MODEL), xs.reshape(n_dec, dec_len, D_MODEL),
            state_k, state_v)
```

```python
import functools
import math

import jax
import jax.numpy as jnp
import numpy as np
from jax import lax
from jax.experimental import pallas as pl
from jax.experimental.pallas import tpu as pltpu

D_MODEL = 1024
ATT_WIDTH = 512
N_HEADS = 4
HEAD_DIM = 64
VAL_DIM = 128
MLP_WIDTH = 512
N_GROUPS = 4
GROUP_DIM = 128
CHUNK = 128
D_FF = 2816
IN_WIDTH = 2560
GRID_W = 64
ROPE_THETA = 10000.0
ROPE_AXIS_DIM = 32
EPS = 1e-6
LOG2E = 1.4426950408889634
LANES = 128

F32 = jnp.float32
BF16 = jnp.bfloat16

VMEM_LIMIT_BYTES = 56 * 1024 * 1024
MODS_ROWS = 16
CTX_ROW = 8

TM_PROJ = 1024
TM_FFN = 512
TM_ATTN_CTX = 512
CTX_AHEAD = 3
FF_CHUNKS = ((0, 1024), (1024, 1024), (2048, 768))

ONES_ROWS = 16
TQ_UNIT = 256
KEY_BLOCK = 256
SCORE_LEAD = 2


def _const_spec(shape):
    zeros = (0,) * len(shape)
    return pl.BlockSpec(shape, lambda *_: zeros, pipeline_mode=pl.Buffered(1))


def _params(n_grid):
    return pltpu.CompilerParams(
        dimension_semantics=("arbitrary",) * n_grid,
        vmem_limit_bytes=VMEM_LIMIT_BYTES,
    )


def _rms_scale(x):
    return lax.rsqrt(jnp.mean(x * x, axis=-1, keepdims=True) + EPS)


def _mod(mods_ref, row, k):
    return mods_ref[pl.ds(row, 1), k * D_MODEL:(k + 1) * D_MODEL]


def _mods_kernel(c_ref, cctx_ref, w_ref, b_ref, o_ref):
    cnd = jnp.concatenate(
        [c_ref[...], jnp.broadcast_to(cctx_ref[...], (MODS_ROWS - CTX_ROW, D_MODEL))], axis=0)
    act = (cnd * jax.nn.sigmoid(cnd)).astype(BF16)
    o_ref[...] = jnp.dot(act, w_ref[...].astype(BF16), preferred_element_type=F32) + b_ref[...]


def _mods(c, c_ctx, w_ada, b_ada):
    tn = 1536
    n_out = w_ada.shape[1]
    assert c.shape[0] == CTX_ROW
    return pl.pallas_call(
        _mods_kernel,
        grid=(n_out // tn,),
        in_specs=[
            _const_spec((CTX_ROW, D_MODEL)),
            _const_spec((1, D_MODEL)),
            pl.BlockSpec((D_MODEL, tn), lambda j: (0, j)),
            pl.BlockSpec((1, tn), lambda j: (0, j)),
        ],
        out_specs=pl.BlockSpec((MODS_ROWS, tn), lambda j: (0, j)),
        out_shape=jax.ShapeDtypeStruct((MODS_ROWS, n_out), F32),
        compiler_params=_params(1),
        name="mods",
    )(c, c_ctx, w_ada, b_ada)


def _proj_kernel(*refs, rope, seq_len, tm, mods_row_fn):
    it = iter(refs)
    x_ref, mods_ref, g1_ref, w_in_ref, qg_ref, kg_ref = (next(it) for _ in range(6))
    if rope:
        cos_ref, sin_ref = (next(it) for _ in range(2))
    sgun_ref, sguw_ref, sgub_ref, mlpg_ref = (next(it) for _ in range(4))
    qt_ref, k_ref, vt_ref, mlp_ref = (next(it) for _ in range(4))
    if not rope:
        kst_ref, vst_ref = (next(it) for _ in range(2))
    gate_ref = next(it)

    x = x_ref[...]
    xn = x * _rms_scale(x) * g1_ref[...]
    row = mods_row_fn(pl.program_id(0))
    xm = xn * (1.0 + _mod(mods_ref, row, 1)) + _mod(mods_ref, row, 0)
    xb = xm.astype(BF16)

    def section(lo, hi):
        return jnp.dot(xb, w_in_ref[:, lo:hi], preferred_element_type=F32)

    def head_t(sec, h, g_ref):
        t = sec[:, h * LANES:(h + 1) * LANES].T
        maps = []
        for mp in range(2):
            tmap = t[mp * HEAD_DIM:(mp + 1) * HEAD_DIM, :]
            maps.append(tmap * lax.rsqrt(jnp.mean(tmap * tmap, axis=0, keepdims=True) + EPS))
        gain = jnp.concatenate([g_ref[...]] * (tm // LANES), axis=1)
        tn = jnp.concatenate(maps, axis=0) * gain
        if rope:
            swapped = jnp.concatenate(
                [tn[r0 + off:r0 + off + 16, :] for r0 in range(0, LANES, 32) for off in (16, 0)],
                axis=0)
            tn = tn * cos_ref[...] + swapped * sin_ref[...]
        return tn

    hu = section(3 * ATT_WIDTH, 3 * ATT_WIDTH + MLP_WIDTH)
    hg = section(3 * ATT_WIDTH + MLP_WIDTH, IN_WIDTH)
    for g in range(N_GROUPS):
        cols = slice(g * GROUP_DIM, (g + 1) * GROUP_DIM)
        gg = hg[:, cols]
        gn = (gg * _rms_scale(gg) * sgun_ref[:, cols]).astype(BF16)
        ug = hu[:, cols]
        wg = sguw_ref[g]
        bg = sgub_ref[g]
        for n in range(tm // CHUNK):
            rows = slice(n * CHUNK, (n + 1) * CHUNK)
            sp = jnp.dot(wg, gn[rows, :], preferred_element_type=F32) + bg
            gate_ref[rows, cols] = ug[rows, :] * sp
    o = gate_ref[...]
    mlp_ref[...] = (o * _rms_scale(o) * mlpg_ref[...]).astype(BF16)

    hq = section(0, ATT_WIDTH)
    hk = section(ATT_WIDTH, 2 * ATT_WIDTH)
    for h in range(N_HEADS):
        qt_ref[h] = head_t(hq, h, qg_ref).astype(BF16)
        kt = head_t(hk, h, kg_ref)
        k_ref[h] = kt.T.astype(BF16)
        if not rope:
            for s in range(tm // seq_len):
                for i in range(2):
                    kst_ref[s, 0, h, i, :, :] = kt[i * HEAD_DIM:(i + 1) * HEAD_DIM,
                                                   s * seq_len:(s + 1) * seq_len]

    hv = section(2 * ATT_WIDTH, 3 * ATT_WIDTH)
    hvt = hv.T.astype(BF16)
    for h in range(N_HEADS):
        vt_ref[h] = hvt[h * VAL_DIM:(h + 1) * VAL_DIM, :]
        if not rope:
            for s in range(tm // seq_len):
                vst_ref[s, 0, h, :, :] = hv[s * seq_len:(s + 1) * seq_len,
                                            h * VAL_DIM:(h + 1) * VAL_DIM]


def _proj(x2d, mods3, g1, w_in, qg, kg, rope_tabs, sgun, sguw, sgub, mlpg,
          *, seq_len, mods_row_fn):
    n_tok = x2d.shape[0]
    tm = TM_PROJ
    rope = rope_tabs is not None
    blocks_per_seq = seq_len // tm if rope else None

    in_specs = [
        pl.BlockSpec((tm, D_MODEL), lambda i: (i, 0)),
        _const_spec((MODS_ROWS, 6 * D_MODEL)),
        _const_spec((1, D_MODEL)),
        _const_spec((D_MODEL, IN_WIDTH)),
        _const_spec((LANES, LANES)),
        _const_spec((LANES, LANES)),
    ]
    args = [x2d, mods3, g1, w_in, qg, kg]
    if rope:
        tab_spec = pl.BlockSpec((LANES, tm), lambda i: (0, i % blocks_per_seq))
        in_specs += [tab_spec] * 2
        args += list(rope_tabs)
    in_specs += [
        _const_spec((1, MLP_WIDTH)),
        _const_spec((N_GROUPS, CHUNK, CHUNK)),
        _const_spec((N_GROUPS, CHUNK, GROUP_DIM)),
        _const_spec((1, MLP_WIDTH)),
    ]
    args += [sgun, sguw, sgub, mlpg]

    head_spec = pl.BlockSpec((N_HEADS, tm, LANES), lambda i: (0, i, 0))
    head_t_spec = pl.BlockSpec((N_HEADS, LANES, tm), lambda i: (0, 0, i))
    head_shape = jax.ShapeDtypeStruct((N_HEADS, n_tok, LANES), BF16)
    head_t_shape = jax.ShapeDtypeStruct((N_HEADS, LANES, n_tok), BF16)
    out_specs = [head_t_spec, head_spec, head_t_spec,
                 pl.BlockSpec((tm, MLP_WIDTH), lambda i: (i, 0))]
    out_shape = [head_t_shape, head_shape, head_t_shape,
                 jax.ShapeDtypeStruct((n_tok, MLP_WIDTH), BF16)]
    if not rope:
        n_seq = n_tok // seq_len
        spb = tm // seq_len
        out_specs += [
            pl.BlockSpec((spb, 1, N_HEADS, 2, HEAD_DIM, seq_len), lambda i: (i, 0, 0, 0, 0, 0)),
            pl.BlockSpec((spb, 1, N_HEADS, seq_len, VAL_DIM), lambda i: (i, 0, 0, 0, 0)),
        ]
        out_shape += [
            jax.ShapeDtypeStruct((n_seq, 1, N_HEADS, 2, HEAD_DIM, seq_len), F32),
            jax.ShapeDtypeStruct((n_seq, 1, N_HEADS, seq_len, VAL_DIM), F32),
        ]

    return pl.pallas_call(
        functools.partial(_proj_kernel, rope=rope, seq_len=seq_len, tm=tm, mods_row_fn=mods_row_fn),
        grid=(n_tok // tm,),
        in_specs=in_specs,
        out_specs=out_specs,
        out_shape=out_shape,
        scratch_shapes=[pltpu.VMEM((tm, MLP_WIDTH), F32)],
        compiler_params=_params(1),
        name="proj_rope" if rope else "proj_ctx",
    )(*args)


def _lambda_full(lq1, lk1, lq2, lk2, lambda_init):
    return (jnp.exp(jnp.sum(lq1[...] * lk1[...], keepdims=True))
            - jnp.exp(jnp.sum(lq2[...] * lk2[...], keepdims=True))
            + lambda_init)


def _map_queries(qt):
    row = lax.broadcasted_iota(jnp.int32, qt.shape, 0)
    zero = jnp.zeros_like(qt)
    return (jnp.where(row < HEAD_DIM, qt, zero), jnp.where(row >= HEAD_DIM, qt, zero))


def _combine_maps(o1, d1, o2, d2, lam, out_gain):
    ot = o1 * (1.0 / d1) - o2 * (lam / d2)
    ot = ot * lax.rsqrt(jnp.mean(ot * ot, axis=0, keepdims=True) + EPS)
    return (ot.T * out_gain).astype(BF16)


def _attn_ctx_kernel(lq1, lk1, lq2, lk2, qt_ref, k_ref, vt_ref, ag_ref, o_ref, st_buf,
                     *, lambda_init, seq_len):
    lam = _lambda_full(lq1, lk1, lq2, lk2, lambda_init)
    units = [(slice(s * seq_len, (s + 1) * seq_len), h)
             for s in range(k_ref.shape[1] // seq_len) for h in range(N_HEADS)]

    def scores(u):
        rows, h = units[u]
        kk = k_ref[h, rows, :]
        maxes = []
        for mp, qm in enumerate(_map_queries(qt_ref[h, :, rows])):
            st = jnp.dot(kk, qm, preferred_element_type=F32)
            st_buf[u % CTX_AHEAD, mp] = st
            maxes.append(jnp.max(st, axis=0, keepdims=True))
        return maxes

    def finish(u, maxes):
        rows, h = units[u]
        vt = vt_ref[h, :, rows]
        outs, dens = [], []
        for mp in range(2):
            e = jnp.exp2(st_buf[u % CTX_AHEAD, mp] - maxes[mp])
            dens.append(jnp.sum(e, axis=0, keepdims=True))
            outs.append(jnp.dot(vt, e.astype(BF16), preferred_element_type=F32))
        out_gain = (1.0 - lambda_init) * ag_ref[h]
        o_ref[h, rows, :] = _combine_maps(outs[0], dens[0], outs[1], dens[1], lam, out_gain)

    pending = [scores(u) for u in range(CTX_AHEAD - 1)]
    for u in range(len(units)):
        if u + CTX_AHEAD - 1 < len(units):
            pending.append(scores(u + CTX_AHEAD - 1))
        finish(u, pending.pop(0))


def _attn_ctx(lams, q, k, vt, att_g, *, seq_len, lambda_init):
    n_tok = k.shape[1]
    tm = TM_ATTN_CTX
    head_spec = pl.BlockSpec((N_HEADS, tm, LANES), lambda i: (0, i, 0))
    head_t_spec = pl.BlockSpec((N_HEADS, LANES, tm), lambda i: (0, 0, i))
    return pl.pallas_call(
        functools.partial(_attn_ctx_kernel, lambda_init=lambda_init, seq_len=seq_len),
        grid=(n_tok // tm,),
        in_specs=[_const_spec((1, HEAD_DIM))] * 4 + [
            head_t_spec,
            head_spec,
            head_t_spec,
            _const_spec((N_HEADS, 1, LANES)),
        ],
        out_specs=head_spec,
        out_shape=jax.ShapeDtypeStruct((N_HEADS, n_tok, LANES), BF16),
        scratch_shapes=[pltpu.VMEM((CTX_AHEAD, 2, seq_len, seq_len), F32)],
        compiler_params=_params(1),
        name="attn_ctx",
    )(*lams, q, k, vt, att_g)


def _attn_cache_kernel(lq1, lk1, lq2, lk2, qt_ref, k_ref, vt_ref, kct_ref, vc_ref, ag_ref,
                       *rest, lambda_init, n_new, n_weights):
    for w_ref, wb_ref in zip(rest[:n_weights], rest[n_weights + 1:2 * n_weights + 1]):
        wb_ref[...] = w_ref[...].astype(BF16)
    o_ref = rest[n_weights]
    k_all, vt_all, m_buf, acc_buf, *bufs = rest[2 * n_weights + 1:]
    st = (bufs[0:2], bufs[2:4])
    n_chunks = n_new // TQ_UNIT
    n_units = N_HEADS * n_chunks
    n_keys = k_all.shape[1]

    past = n_keys - n_new
    for h in range(N_HEADS):
        k_all[h, 0:n_new, :] = k_ref[h]
        k_all[h, n_new:, :] = kct_ref[0, 0, h].reshape(2 * HEAD_DIM, past).T.astype(BF16)
        vt_all[h, 0:VAL_DIM, 0:n_new] = vt_ref[h]
        vt_all[h, 0:VAL_DIM, n_new:] = vc_ref[0, 0, h].T.astype(BF16)
        vt_all[h, VAL_DIM:, :] = jnp.ones((ONES_ROWS, n_keys), BF16)

    lam = _lambda_full(lq1, lk1, lq2, lk2, lambda_init)

    def head_rows(u):
        c = u % n_chunks
        return u // n_chunks, pl.ds(pl.multiple_of(c * TQ_UNIT, TQ_UNIT), TQ_UNIT)

    def stage(fin, sc, defer_out=False):
        if sc is not None:
            sc_head, sc_rows = head_rows(sc[0])
            qms = _map_queries(qt_ref[sc_head, :, sc_rows])
            mrun = [None, None]
        if fin is not None:
            fin_head = fin[0] // n_chunks
            ms = [m_buf[fin[1], mp] for mp in range(2)]
            accs = [None, None]
        n_kb = n_keys // KEY_BLOCK
        lead = SCORE_LEAD if (sc is not None and fin is not None) else 0
        for step in range(n_kb + lead):
            if sc is not None and step < n_kb:
                kr = slice(step * KEY_BLOCK, (step + 1) * KEY_BLOCK)
                kk = k_all[sc_head, kr, :]
                for mp in range(2):
                    s = jnp.dot(kk, qms[mp], preferred_element_type=F32)
                    st[sc[1]][mp][kr, :] = s
                    smax = jnp.max(s.reshape(KEY_BLOCK // 8, 8, TQ_UNIT), axis=0)
                    mrun[mp] = smax if mrun[mp] is None else jnp.maximum(mrun[mp], smax)
            if fin is not None and step >= lead:
                kr = slice((step - lead) * KEY_BLOCK, (step - lead + 1) * KEY_BLOCK)
                vt = vt_all[fin_head, :, kr]
                for mp in range(2):
                    p = jnp.exp2(st[fin[1]][mp][kr, :] - ms[mp]).astype(BF16)
                    d = jnp.dot(vt, p, preferred_element_type=F32)
                    accs[mp] = d if accs[mp] is None else accs[mp] + d
        if sc is not None:
            for mp in range(2):
                m_buf[sc[1], mp] = jnp.max(mrun[mp], axis=0, keepdims=True)
        if fin is not None:
            if defer_out:
                for mp in range(2):
                    acc_buf[mp] = accs[mp]
            else:
                write_out(fin[0], accs[0], accs[1])

    def write_out(u, o1, o2):
        head, rows = head_rows(u)
        out_gain = (1.0 - lambda_init) * ag_ref[head]
        o_ref[head, rows, :] = _combine_maps(
            o1[0:VAL_DIM, :], o1[VAL_DIM:VAL_DIM + 1, :],
            o2[0:VAL_DIM, :], o2[VAL_DIM:VAL_DIM + 1, :], lam, out_gain)

    stage(None, (0, 0))
    stage((0, 0), (1, 1), defer_out=True)

    def pair(i, carry):
        u = 2 * i
        write_out(u - 2, acc_buf[0], acc_buf[1])
        stage((u - 1, 1), (u, 0))
        stage((u, 0), (u + 1, 1), defer_out=True)
        return carry

    lax.fori_loop(1, n_units // 2, pair, 0)
    write_out(n_units - 2, acc_buf[0], acc_buf[1])
    stage((n_units - 1, 1), None)


def _attn_cache(lams, q, k, vt, kct, vc, att_g, weights, *, layer, n_batch, seq_len, lambda_init):
    past = vc.shape[3]
    n_keys = seq_len + past
    assert (seq_len // TQ_UNIT) % 2 == 0 and seq_len // TQ_UNIT >= 4
    assert n_keys % KEY_BLOCK == 0
    head_spec = pl.BlockSpec((N_HEADS, seq_len, LANES), lambda b: (0, b, 0))
    head_t_spec = pl.BlockSpec((N_HEADS, LANES, seq_len), lambda b: (0, 0, b))
    in_specs = [_const_spec((1, HEAD_DIM))] * 4 + [
        head_t_spec,
        head_spec,
        head_t_spec,
        pl.BlockSpec((1, 1, N_HEADS, 2, HEAD_DIM, past), lambda b: (b, layer, 0, 0, 0, 0)),
        pl.BlockSpec((1, 1, N_HEADS, past, VAL_DIM), lambda b: (b, layer, 0, 0, 0)),
        _const_spec((N_HEADS, 1, LANES)),
    ]
    w_specs = []
    for w in weights:
        slab = w.shape[0] // n_batch
        assert w.shape[0] % n_batch == 0 and slab % 16 == 0
        w_specs.append(pl.BlockSpec((slab, w.shape[1]), lambda b: (b, 0)))
    return pl.pallas_call(
        functools.partial(_attn_cache_kernel, lambda_init=lambda_init, n_new=seq_len,
                          n_weights=len(weights)),
        grid=(n_batch,),
        in_specs=in_specs + w_specs,
        out_specs=[head_spec] + w_specs,
        out_shape=[jax.ShapeDtypeStruct((N_HEADS, n_batch * seq_len, LANES), BF16)]
                  + [jax.ShapeDtypeStruct(w.shape, BF16) for w in weights],
        scratch_shapes=([pltpu.VMEM((N_HEADS, n_keys, LANES), BF16),
                         pltpu.VMEM((N_HEADS, VAL_DIM + ONES_ROWS, n_keys), BF16),
                         pltpu.VMEM((2, 2, 1, TQ_UNIT), F32),
                         pltpu.VMEM((2, VAL_DIM + ONES_ROWS, TQ_UNIT), F32)]
                        + [pltpu.VMEM((n_keys, TQ_UNIT), F32)] * 4),
        compiler_params=_params(1),
        name="attn_cache",
    )(*lams, q, k, vt, kct, vc, att_g, *weights)


def _ffn_kernel(x_ref, att_ref, mlp_ref, mods_ref, g2_ref, wo_ref, wfi_ref, wfo_ref, o_ref,
                *, mods_row_fn):
    row = mods_row_fn(pl.program_id(0))
    gate1, shift2, scale2, gate2 = (_mod(mods_ref, row, k) for k in (2, 3, 4, 5))
    tm = x_ref.shape[0]
    halves = [slice(i * (tm // 2), (i + 1) * (tm // 2)) for i in range(2)]
    x1s, xbs = [], []
    for r in halves:
        att = jnp.concatenate([att_ref[h, r, :] for h in range(N_HEADS)], axis=1)
        y = (jnp.dot(att, wo_ref[0:ATT_WIDTH, :], preferred_element_type=F32)
             + jnp.dot(mlp_ref[r, :], wo_ref[ATT_WIDTH:, :], preferred_element_type=F32))
        x1 = x_ref[r, :] + gate1 * y
        xn = x1 * _rms_scale(x1) * g2_ref[...]
        x1s.append(x1)
        xbs.append((xn * (1.0 + scale2) + shift2).astype(BF16))
    accs = [None, None]
    for c0, cw in FF_CHUNKS:
        pre = []
        for xb in xbs:
            gte = jnp.dot(xb, wfi_ref[:, c0:c0 + cw], preferred_element_type=F32)
            up = jnp.dot(xb, wfi_ref[:, D_FF + c0:D_FF + c0 + cw], preferred_element_type=F32)
            pre.append((gte, up))
        for i, (gte, up) in enumerate(pre):
            act = (gte * jax.nn.sigmoid(gte) * up).astype(BF16)
            part = jnp.dot(act, wfo_ref[c0:c0 + cw, :], preferred_element_type=F32)
            accs[i] = part if accs[i] is None else accs[i] + part
    for r, x1, acc in zip(halves, x1s, accs):
        o_ref[r, :] = x1 + gate2 * acc


def _ffn(x2d, att, mlp, mods3, g2, w_out, w_ffn_in, w_ffn_out, *, mods_row_fn):
    n_tok = x2d.shape[0]
    tm = TM_FFN
    return pl.pallas_call(
        functools.partial(_ffn_kernel, mods_row_fn=mods_row_fn),
        grid=(n_tok // tm,),
        in_specs=[
            pl.BlockSpec((tm, D_MODEL), lambda i: (i, 0)),
            pl.BlockSpec((N_HEADS, tm, LANES), lambda i: (0, i, 0)),
            pl.BlockSpec((tm, MLP_WIDTH), lambda i: (i, 0)),
            _const_spec((MODS_ROWS, 6 * D_MODEL)),
            _const_spec((1, D_MODEL)),
            _const_spec((D_MODEL, D_MODEL)),
            _const_spec((D_MODEL, 2 * D_FF)),
            _const_spec((D_FF, D_MODEL)),
        ],
        out_specs=pl.BlockSpec((tm, D_MODEL), lambda i: (i, 0)),
        out_shape=jax.ShapeDtypeStruct((n_tok, D_MODEL), F32),
        compiler_params=_params(1),
        name="ffn",
    )(x2d, att, mlp, mods3, g2, w_out, w_ffn_in, w_ffn_out)


def _rope_tables(n):
    pos = np.arange(n)
    row = (pos // GRID_W).astype(np.float32)
    col = (pos % GRID_W).astype(np.float32)
    inv = (ROPE_THETA ** (-np.arange(0, ROPE_AXIS_DIM, 2, dtype=np.float32) / ROPE_AXIS_DIM)
           ).astype(np.float32)
    ang_r = row[:, None] * inv[None, :]
    ang_c = col[:, None] * inv[None, :]
    cos64 = np.concatenate([np.cos(ang_r)] * 2 + [np.cos(ang_c)] * 2, axis=1)
    sin64 = np.concatenate([-np.sin(ang_r), np.sin(ang_r), -np.sin(ang_c), np.sin(ang_c)], axis=1)
    return (np.ascontiguousarray(np.tile(cos64, (1, 2)).T, np.float32),
            np.ascontiguousarray(np.tile(sin64, (1, 2)).T, np.float32))


def kernel(x_prompt, x_sample, cache_k_ctx, cache_v_ctx, c, c_ctx, norm1_g, norm2_g, w_ada, b_ada, w_in, q_norm_g, k_norm_g, lambda_q1, lambda_k1, lambda_q2, lambda_k2, att_out_g, sgu_norm_g, sgu_w, sgu_b, mlp_out_g, w_out, w_ffn_in, w_ffn_out):
    n_ctx, ctx_len, _ = x_prompt.shape
    n_dec, dec_len, _ = x_sample.shape
    depth = norm1_g.shape[0]

    rope_tabs = _rope_tables(dec_len)
    q_scale = LOG2E / math.sqrt(HEAD_DIM)

    xp = x_prompt.reshape(n_ctx * ctx_len, D_MODEL)
    xs = x_sample.reshape(n_dec * dec_len, D_MODEL)
    cache_kt = jnp.swapaxes(cache_k_ctx, -1, -2)
    k_states, v_states = [], []
    ctx_row = lambda i: CTX_ROW
    dec_row_proj = lambda i: i // (dec_len // TM_PROJ)
    dec_row_ffn = lambda i: i // (dec_len // TM_FFN)

    for l in range(depth):
        lambda_init = 0.8 - 0.6 * math.exp(-0.3 * l)
        mods3 = _mods(c, c_ctx[None, :], w_ada[l], b_ada[l][None, :])
        g1 = norm1_g[l][None, :]
        g2 = norm2_g[l][None, :]
        w_in_b = w_in[l].astype(BF16)
        qg = jnp.broadcast_to(jnp.tile(q_norm_g[l] * q_scale, 2)[:, None], (LANES, LANES))
        kg = jnp.broadcast_to(jnp.tile(k_norm_g[l], 2)[:, None], (LANES, LANES))
        sgun = sgu_norm_g[l][None, :]
        sguw = sgu_w[l].astype(BF16)
        sgub = jnp.broadcast_to(sgu_b[l][:, :, None], (N_GROUPS, CHUNK, GROUP_DIM))
        mlpg = mlp_out_g[l][None, :]
        att_g = att_out_g[l].reshape(N_HEADS, 1, VAL_DIM)
        lams = (lambda_q1[l][None, :], lambda_k1[l][None, :],
                lambda_q2[l][None, :], lambda_k2[l][None, :])

        q, k, vt, mlp_s = _proj(
            xs, mods3, g1, w_in_b, qg, kg, rope_tabs, sgun, sguw, sgub, mlpg,
            seq_len=dec_len, mods_row_fn=dec_row_proj)
        att_s, w_out_b, w_fi_b, w_fo_b = _attn_cache(
            lams, q, k, vt, cache_kt, cache_v_ctx, att_g, (w_out[l], w_ffn_in[l], w_ffn_out[l]),
            layer=l, n_batch=n_dec, seq_len=dec_len, lambda_init=lambda_init)

        q, k, vt, mlp, k_c, v_c = _proj(
            xp, mods3, g1, w_in_b, qg, kg, None, sgun, sguw, sgub, mlpg,
            seq_len=ctx_len, mods_row_fn=ctx_row)
        att = _attn_ctx(lams, q, k, vt, att_g, seq_len=ctx_len, lambda_init=lambda_init)
        xp = _ffn(xp, att, mlp, mods3, g2, w_out_b, w_fi_b, w_fo_b, mods_row_fn=ctx_row)
        k_states.append(k_c)
        v_states.append(v_c)

        xs = _ffn(xs, att_s, mlp_s, mods3, g2, w_out_b, w_fi_b, w_fo_b, mods_row_fn=dec_row_ffn)

    state_k = jnp.swapaxes(jnp.concatenate(k_states, axis=1), -1, -2)
    state_v = jnp.concatenate(v_states, axis=1)
    return (xp.reshape(n_ctx, ctx_len, D_MODEL), xs.reshape(n_dec, dec_len, D_MODEL),
            state_k, state_v)
```

```python
import functools
import math

import jax
import jax.numpy as jnp
import numpy as np
from jax import lax
from jax.experimental import pallas as pl
from jax.experimental.pallas import tpu as pltpu

D_MODEL = 1024
ATT_WIDTH = 512
N_HEADS = 4
HEAD_DIM = 64
VAL_DIM = 128
MLP_WIDTH = 512
N_GROUPS = 4
GROUP_DIM = 128
CHUNK = 128
D_FF = 2816
IN_WIDTH = 2560
GRID_W = 64
ROPE_THETA = 10000.0
ROPE_AXIS_DIM = 32
EPS = 1e-6
LOG2E = 1.4426950408889634
LANES = 128

F32 = jnp.float32
BF16 = jnp.bfloat16

VMEM_LIMIT_BYTES = 56 * 1024 * 1024
MODS_ROWS = 16
CTX_ROW = 8

TM_PROJ = 1024
TM_FFN = 512
TM_ATTN_CTX = 512
CTX_AHEAD = 3
FF_CHUNKS = ((0, 1024), (1024, 1024), (2048, 768))

ONES_ROWS = 16
TQ_UNIT = 256
KEY_BLOCK = 256
SCORE_LEAD = 2
PAIRS_PER_TRIP = 5


def _const_spec(shape):
    zeros = (0,) * len(shape)
    return pl.BlockSpec(shape, lambda *_: zeros, pipeline_mode=pl.Buffered(1))


def _params(n_grid):
    return pltpu.CompilerParams(
        dimension_semantics=("arbitrary",) * n_grid,
        vmem_limit_bytes=VMEM_LIMIT_BYTES,
    )


def _rms_scale(x):
    return lax.rsqrt(jnp.mean(x * x, axis=-1, keepdims=True) + EPS)


def _mod(mods_ref, row, k):
    return mods_ref[pl.ds(row, 1), k * D_MODEL:(k + 1) * D_MODEL]


def _mods_kernel(c_ref, cctx_ref, w_ref, b_ref, o_ref):
    cnd = jnp.concatenate(
        [c_ref[...], jnp.broadcast_to(cctx_ref[...], (MODS_ROWS - CTX_ROW, D_MODEL))], axis=0)
    act = (cnd * jax.nn.sigmoid(cnd)).astype(BF16)
    o_ref[...] = jnp.dot(act, w_ref[...].astype(BF16), preferred_element_type=F32) + b_ref[...]


def _mods(c, c_ctx, w_ada, b_ada):
    tn = 1536
    n_out = w_ada.shape[1]
    assert c.shape[0] == CTX_ROW
    return pl.pallas_call(
        _mods_kernel,
        grid=(n_out // tn,),
        in_specs=[
            _const_spec((CTX_ROW, D_MODEL)),
            _const_spec((1, D_MODEL)),
            pl.BlockSpec((D_MODEL, tn), lambda j: (0, j)),
            pl.BlockSpec((1, tn), lambda j: (0, j)),
        ],
        out_specs=pl.BlockSpec((MODS_ROWS, tn), lambda j: (0, j)),
        out_shape=jax.ShapeDtypeStruct((MODS_ROWS, n_out), F32),
        compiler_params=_params(1),
        name="mods",
    )(c, c_ctx, w_ada, b_ada)


def _proj_kernel(*refs, rope, seq_len, tm, mods_row_fn):
    it = iter(refs)
    x_ref, mods_ref, g1_ref, w_in_ref, qg_ref, kg_ref = (next(it) for _ in range(6))
    if rope:
        cos_ref, sin_ref = (next(it) for _ in range(2))
    sgun_ref, sguw_ref, sgub_ref, mlpg_ref = (next(it) for _ in range(4))
    qt_ref, k_ref, vt_ref, mlp_ref = (next(it) for _ in range(4))
    if not rope:
        kst_ref, vst_ref = (next(it) for _ in range(2))
    gate_ref = next(it)

    x = x_ref[...]
    xn = x * _rms_scale(x) * g1_ref[...]
    row = mods_row_fn(pl.program_id(0))
    xm = xn * (1.0 + _mod(mods_ref, row, 1)) + _mod(mods_ref, row, 0)
    xb = xm.astype(BF16)

    def section(lo, hi):
        return jnp.dot(xb, w_in_ref[:, lo:hi], preferred_element_type=F32)

    def head_t(sec, h, g_ref):
        t = sec[:, h * LANES:(h + 1) * LANES].T
        maps = []
        for mp in range(2):
            tmap = t[mp * HEAD_DIM:(mp + 1) * HEAD_DIM, :]
            maps.append(tmap * lax.rsqrt(jnp.mean(tmap * tmap, axis=0, keepdims=True) + EPS))
        gain = jnp.concatenate([g_ref[...]] * (tm // LANES), axis=1)
        tn = jnp.concatenate(maps, axis=0) * gain
        if rope:
            half = ROPE_AXIS_DIM // 2
            swapped = jnp.concatenate(
                [tn[r0 + off:r0 + off + half, :]
                 for r0 in range(0, LANES, ROPE_AXIS_DIM) for off in (half, 0)], axis=0)
            tn = tn * cos_ref[...] + swapped * sin_ref[...]
        return tn

    hu = section(3 * ATT_WIDTH, 3 * ATT_WIDTH + MLP_WIDTH)
    hg = section(3 * ATT_WIDTH + MLP_WIDTH, IN_WIDTH)
    for g in range(N_GROUPS):
        cols = slice(g * GROUP_DIM, (g + 1) * GROUP_DIM)
        gg = hg[:, cols]
        gn = (gg * _rms_scale(gg) * sgun_ref[:, cols]).astype(BF16)
        ug = hu[:, cols]
        wg = sguw_ref[g]
        bg = sgub_ref[g]
        for n in range(tm // CHUNK):
            rows = slice(n * CHUNK, (n + 1) * CHUNK)
            sp = jnp.dot(wg, gn[rows, :], preferred_element_type=F32) + bg
            gate_ref[rows, cols] = ug[rows, :] * sp
    o = gate_ref[...]
    mlp_ref[...] = (o * _rms_scale(o) * mlpg_ref[...]).astype(BF16)

    hq = section(0, ATT_WIDTH)
    hk = section(ATT_WIDTH, 2 * ATT_WIDTH)
    for h in range(N_HEADS):
        qt_ref[h] = head_t(hq, h, qg_ref).astype(BF16)
        kt = head_t(hk, h, kg_ref)
        k_ref[h] = kt.T.astype(BF16)
        if not rope:
            for s in range(tm // seq_len):
                for i in range(2):
                    kst_ref[s, 0, h, i, :, :] = kt[i * HEAD_DIM:(i + 1) * HEAD_DIM,
                                                   s * seq_len:(s + 1) * seq_len]

    hv = section(2 * ATT_WIDTH, 3 * ATT_WIDTH)
    hvt = hv.T.astype(BF16)
    for h in range(N_HEADS):
        vt_ref[h] = hvt[h * VAL_DIM:(h + 1) * VAL_DIM, :]
        if not rope:
            for s in range(tm // seq_len):
                vst_ref[s, 0, h, :, :] = hv[s * seq_len:(s + 1) * seq_len,
                                            h * VAL_DIM:(h + 1) * VAL_DIM]


def _proj(x2d, mods3, g1, w_in, qg, kg, rope_tabs, sgun, sguw, sgub, mlpg,
          *, seq_len, mods_row_fn):
    n_tok = x2d.shape[0]
    tm = TM_PROJ
    rope = rope_tabs is not None
    blocks_per_seq = seq_len // tm if rope else None

    in_specs = [
        pl.BlockSpec((tm, D_MODEL), lambda i: (i, 0)),
        _const_spec((MODS_ROWS, 6 * D_MODEL)),
        _const_spec((1, D_MODEL)),
        _const_spec((D_MODEL, IN_WIDTH)),
        _const_spec((LANES, LANES)),
        _const_spec((LANES, LANES)),
    ]
    args = [x2d, mods3, g1, w_in, qg, kg]
    if rope:
        tab_spec = pl.BlockSpec((LANES, tm), lambda i: (0, i % blocks_per_seq))
        in_specs += [tab_spec] * 2
        args += list(rope_tabs)
    in_specs += [
        _const_spec((1, MLP_WIDTH)),
        _const_spec((N_GROUPS, CHUNK, CHUNK)),
        _const_spec((N_GROUPS, CHUNK, GROUP_DIM)),
        _const_spec((1, MLP_WIDTH)),
    ]
    args += [sgun, sguw, sgub, mlpg]

    head_spec = pl.BlockSpec((N_HEADS, tm, LANES), lambda i: (0, i, 0))
    head_t_spec = pl.BlockSpec((N_HEADS, LANES, tm), lambda i: (0, 0, i))
    head_shape = jax.ShapeDtypeStruct((N_HEADS, n_tok, LANES), BF16)
    head_t_shape = jax.ShapeDtypeStruct((N_HEADS, LANES, n_tok), BF16)
    out_specs = [head_t_spec, head_spec, head_t_spec,
                 pl.BlockSpec((tm, MLP_WIDTH), lambda i: (i, 0))]
    out_shape = [head_t_shape, head_shape, head_t_shape,
                 jax.ShapeDtypeStruct((n_tok, MLP_WIDTH), BF16)]
    if not rope:
        n_seq = n_tok // seq_len
        spb = tm // seq_len
        out_specs += [
            pl.BlockSpec((spb, 1, N_HEADS, 2, HEAD_DIM, seq_len), lambda i: (i, 0, 0, 0, 0, 0)),
            pl.BlockSpec((spb, 1, N_HEADS, seq_len, VAL_DIM), lambda i: (i, 0, 0, 0, 0)),
        ]
        out_shape += [
            jax.ShapeDtypeStruct((n_seq, 1, N_HEADS, 2, HEAD_DIM, seq_len), F32),
            jax.ShapeDtypeStruct((n_seq, 1, N_HEADS, seq_len, VAL_DIM), F32),
        ]

    return pl.pallas_call(
        functools.partial(_proj_kernel, rope=rope, seq_len=seq_len, tm=tm, mods_row_fn=mods_row_fn),
        grid=(n_tok // tm,),
        in_specs=in_specs,
        out_specs=out_specs,
        out_shape=out_shape,
        scratch_shapes=[pltpu.VMEM((tm, MLP_WIDTH), F32)],
        compiler_params=_params(1),
        name="proj_rope" if rope else "proj_ctx",
    )(*args)


def _lambda_full(lq1, lk1, lq2, lk2, lambda_init):
    return (jnp.exp(jnp.sum(lq1[...] * lk1[...], keepdims=True))
            - jnp.exp(jnp.sum(lq2[...] * lk2[...], keepdims=True))
            + lambda_init)


def _map_queries(qt):
    row = lax.broadcasted_iota(jnp.int32, qt.shape, 0)
    zero = jnp.zeros_like(qt)
    return (jnp.where(row < HEAD_DIM, qt, zero), jnp.where(row >= HEAD_DIM, qt, zero))


def _combine_maps(o1, d1, o2, d2, lam, out_gain):
    ot = o1 * (1.0 / d1) - o2 * (lam / d2)
    ot = ot * lax.rsqrt(jnp.mean(ot * ot, axis=0, keepdims=True) + EPS)
    return (ot.T * out_gain).astype(BF16)


def _attn_ctx_kernel(lq1, lk1, lq2, lk2, qt_ref, k_ref, vt_ref, ag_ref, o_ref, st_buf,
                     *, lambda_init, seq_len):
    lam = _lambda_full(lq1, lk1, lq2, lk2, lambda_init)
    units = [(slice(s * seq_len, (s + 1) * seq_len), h)
             for s in range(k_ref.shape[1] // seq_len) for h in range(N_HEADS)]

    def scores(u):
        rows, h = units[u]
        kk = k_ref[h, rows, :]
        maxes = []
        for mp, qm in enumerate(_map_queries(qt_ref[h, :, rows])):
            st = jnp.dot(kk, qm, preferred_element_type=F32)
            st_buf[u % CTX_AHEAD, mp] = st
            maxes.append(jnp.max(st, axis=0, keepdims=True))
        return maxes

    def finish(u, maxes):
        rows, h = units[u]
        vt = vt_ref[h, :, rows]
        outs, dens = [], []
        for mp in range(2):
            e = jnp.exp2(st_buf[u % CTX_AHEAD, mp] - maxes[mp])
            dens.append(jnp.sum(e, axis=0, keepdims=True))
            outs.append(jnp.dot(vt, e.astype(BF16), preferred_element_type=F32))
        out_gain = (1.0 - lambda_init) * ag_ref[h]
        o_ref[h, rows, :] = _combine_maps(outs[0], dens[0], outs[1], dens[1], lam, out_gain)

    pending = [scores(u) for u in range(CTX_AHEAD - 1)]
    for u in range(len(units)):
        if u + CTX_AHEAD - 1 < len(units):
            pending.append(scores(u + CTX_AHEAD - 1))
        finish(u, pending.pop(0))


def _attn_ctx(lams, q, k, vt, att_g, *, seq_len, lambda_init):
    n_tok = k.shape[1]
    tm = TM_ATTN_CTX
    head_spec = pl.BlockSpec((N_HEADS, tm, LANES), lambda i: (0, i, 0))
    head_t_spec = pl.BlockSpec((N_HEADS, LANES, tm), lambda i: (0, 0, i))
    return pl.pallas_call(
        functools.partial(_attn_ctx_kernel, lambda_init=lambda_init, seq_len=seq_len),
        grid=(n_tok // tm,),
        in_specs=[_const_spec((1, HEAD_DIM))] * 4 + [
            head_t_spec,
            head_spec,
            head_t_spec,
            _const_spec((N_HEADS, 1, LANES)),
        ],
        out_specs=head_spec,
        out_shape=jax.ShapeDtypeStruct((N_HEADS, n_tok, LANES), BF16),
        scratch_shapes=[pltpu.VMEM((CTX_AHEAD, 2, seq_len, seq_len), F32)],
        compiler_params=_params(1),
        name="attn_ctx",
    )(*lams, q, k, vt, att_g)


def _attn_cache_kernel(lq1, lk1, lq2, lk2, qt_ref, k_ref, vt_ref, kct_ref, vc_ref, ag_ref,
                       *rest, lambda_init, n_new, n_weights):
    for w_ref, wb_ref in zip(rest[:n_weights], rest[n_weights + 1:2 * n_weights + 1]):
        wb_ref[...] = w_ref[...].astype(BF16)
    o_ref = rest[n_weights]
    k_all, vt_all, m_buf, acc_buf, *bufs = rest[2 * n_weights + 1:]
    st = (bufs[0:2], bufs[2:4])
    n_chunks = n_new // TQ_UNIT
    n_units = N_HEADS * n_chunks
    n_keys = k_all.shape[1]

    past = n_keys - n_new
    for h in range(N_HEADS):
        k_all[h, 0:n_new, :] = k_ref[h]
        k_all[h, n_new:, :] = kct_ref[0, 0, h].reshape(2 * HEAD_DIM, past).T.astype(BF16)
        vt_all[h, 0:VAL_DIM, 0:n_new] = vt_ref[h]
        vt_all[h, 0:VAL_DIM, n_new:] = vc_ref[0, 0, h].T.astype(BF16)
        vt_all[h, VAL_DIM:, :] = jnp.ones((ONES_ROWS, n_keys), BF16)

    lam = _lambda_full(lq1, lk1, lq2, lk2, lambda_init)

    def head_rows(u):
        c = u % n_chunks
        return u // n_chunks, pl.ds(pl.multiple_of(c * TQ_UNIT, TQ_UNIT), TQ_UNIT)

    def stage(fin, sc, defer_out=False):
        if sc is not None:
            sc_head, sc_rows = head_rows(sc[0])
            qms = _map_queries(qt_ref[sc_head, :, sc_rows])
            mrun = [None, None]
        if fin is not None:
            fin_head = fin[0] // n_chunks
            ms = [m_buf[fin[1], mp] for mp in range(2)]
            accs = [None, None]
        n_kb = n_keys // KEY_BLOCK
        lead = SCORE_LEAD if (sc is not None and fin is not None) else 0
        for step in range(n_kb + lead):
            if sc is not None and step < n_kb:
                kr = slice(step * KEY_BLOCK, (step + 1) * KEY_BLOCK)
                kk = k_all[sc_head, kr, :]
                for mp in range(2):
                    s = jnp.dot(kk, qms[mp], preferred_element_type=F32)
                    st[sc[1]][mp][kr, :] = s
                    smax = jnp.max(s.reshape(KEY_BLOCK // 8, 8, TQ_UNIT), axis=0)
                    mrun[mp] = smax if mrun[mp] is None else jnp.maximum(mrun[mp], smax)
            if fin is not None and step >= lead:
                kr = slice((step - lead) * KEY_BLOCK, (step - lead + 1) * KEY_BLOCK)
                vt = vt_all[fin_head, :, kr]
                for mp in range(2):
                    p = jnp.exp2(st[fin[1]][mp][kr, :] - ms[mp]).astype(BF16)
                    d = jnp.dot(vt, p, preferred_element_type=F32)
                    accs[mp] = d if accs[mp] is None else accs[mp] + d
        if sc is not None:
            for mp in range(2):
                m_buf[sc[1], mp] = jnp.max(mrun[mp], axis=0, keepdims=True)
        if fin is not None:
            if defer_out:
                for mp in range(2):
                    acc_buf[mp] = accs[mp]
            else:
                write_out(fin[0], accs[0], accs[1])

    def write_out(u, o1, o2):
        head, rows = head_rows(u)
        out_gain = (1.0 - lambda_init) * ag_ref[head]
        o_ref[head, rows, :] = _combine_maps(
            o1[0:VAL_DIM, :], o1[VAL_DIM:VAL_DIM + 1, :],
            o2[0:VAL_DIM, :], o2[VAL_DIM:VAL_DIM + 1, :], lam, out_gain)

    stage(None, (0, 0))
    stage((0, 0), (1, 1), defer_out=True)

    def pair(i):
        u = 2 * i
        write_out(u - 2, acc_buf[0], acc_buf[1])
        stage((u - 1, 1), (u, 0))
        stage((u, 0), (u + 1, 1), defer_out=True)

    def trip(j, carry):
        for p in range(PAIRS_PER_TRIP):
            pair(PAIRS_PER_TRIP * j + 1 + p)
        return carry

    n_pairs = n_units // 2
    n_trips = (n_pairs - 1) // PAIRS_PER_TRIP
    lax.fori_loop(0, n_trips, trip, 0)
    for i in range(n_trips * PAIRS_PER_TRIP + 1, n_pairs):
        pair(i)
    write_out(n_units - 2, acc_buf[0], acc_buf[1])
    stage((n_units - 1, 1), None)


def _attn_cache(lams, q, k, vt, kct, vc, att_g, weights, *, layer, n_batch, seq_len, lambda_init):
    past = vc.shape[3]
    n_keys = seq_len + past
    assert (seq_len // TQ_UNIT) % 2 == 0 and seq_len // TQ_UNIT >= 4
    assert n_keys % KEY_BLOCK == 0
    head_spec = pl.BlockSpec((N_HEADS, seq_len, LANES), lambda b: (0, b, 0))
    head_t_spec = pl.BlockSpec((N_HEADS, LANES, seq_len), lambda b: (0, 0, b))
    in_specs = [_const_spec((1, HEAD_DIM))] * 4 + [
        head_t_spec,
        head_spec,
        head_t_spec,
        pl.BlockSpec((1, 1, N_HEADS, 2, HEAD_DIM, past), lambda b: (b, layer, 0, 0, 0, 0)),
        pl.BlockSpec((1, 1, N_HEADS, past, VAL_DIM), lambda b: (b, layer, 0, 0, 0)),
        _const_spec((N_HEADS, 1, LANES)),
    ]
    w_specs = []
    for w in weights:
        slab = w.shape[0] // n_batch
        assert w.shape[0] % n_batch == 0 and slab % 16 == 0
        w_specs.append(pl.BlockSpec((slab, w.shape[1]), lambda b: (b, 0)))
    return pl.pallas_call(
        functools.partial(_attn_cache_kernel, lambda_init=lambda_init, n_new=seq_len,
                          n_weights=len(weights)),
        grid=(n_batch,),
        in_specs=in_specs + w_specs,
        out_specs=[head_spec] + w_specs,
        out_shape=[jax.ShapeDtypeStruct((N_HEADS, n_batch * seq_len, LANES), BF16)]
                  + [jax.ShapeDtypeStruct(w.shape, BF16) for w in weights],
        scratch_shapes=([pltpu.VMEM((N_HEADS, n_keys, LANES), BF16),
                         pltpu.VMEM((N_HEADS, VAL_DIM + ONES_ROWS, n_keys), BF16),
                         pltpu.VMEM((2, 2, 1, TQ_UNIT), F32),
                         pltpu.VMEM((2, VAL_DIM + ONES_ROWS, TQ_UNIT), F32)]
                        + [pltpu.VMEM((n_keys, TQ_UNIT), F32)] * 4),
        compiler_params=_params(1),
        name="attn_cache",
    )(*lams, q, k, vt, kct, vc, att_g, *weights)


def _ffn_kernel(x_ref, att_ref, mlp_ref, mods_ref, g2_ref, wo_ref, wfi_ref, wfo_ref, o_ref,
                *, mods_row_fn):
    row = mods_row_fn(pl.program_id(0))
    gate1, shift2, scale2, gate2 = (_mod(mods_ref, row, k) for k in (2, 3, 4, 5))
    tm = x_ref.shape[0]
    halves = [slice(i * (tm // 2), (i + 1) * (tm // 2)) for i in range(2)]
    x1s, xbs = [], []
    for r in halves:
        att = jnp.concatenate([att_ref[h, r, :] for h in range(N_HEADS)], axis=1)
        y = (jnp.dot(att, wo_ref[0:ATT_WIDTH, :], preferred_element_type=F32)
             + jnp.dot(mlp_ref[r, :], wo_ref[ATT_WIDTH:, :], preferred_element_type=F32))
        x1 = x_ref[r, :] + gate1 * y
        xn = x1 * _rms_scale(x1) * g2_ref[...]
        x1s.append(x1)
        xbs.append((xn * (1.0 + scale2) + shift2).astype(BF16))
    accs = [None, None]
    for c0, cw in FF_CHUNKS:
        pre = []
        for xb in xbs:
            gte = jnp.dot(xb, wfi_ref[:, c0:c0 + cw], preferred_element_type=F32)
            up = jnp.dot(xb, wfi_ref[:, D_FF + c0:D_FF + c0 + cw], preferred_element_type=F32)
            pre.append((gte, up))
        for i, (gte, up) in enumerate(pre):
            act = (gte * jax.nn.sigmoid(gte) * up).astype(BF16)
            part = jnp.dot(act, wfo_ref[c0:c0 + cw, :], preferred_element_type=F32)
            accs[i] = part if accs[i] is None else accs[i] + part
    for r, x1, acc in zip(halves, x1s, accs):
        o_ref[r, :] = x1 + gate2 * acc


def _ffn(x2d, att, mlp, mods3, g2, w_out, w_ffn_in, w_ffn_out, *, mods_row_fn):
    n_tok = x2d.shape[0]
    tm = TM_FFN
    return pl.pallas_call(
        functools.partial(_ffn_kernel, mods_row_fn=mods_row_fn),
        grid=(n_tok // tm,),
        in_specs=[
            pl.BlockSpec((tm, D_MODEL), lambda i: (i, 0)),
            pl.BlockSpec((N_HEADS, tm, LANES), lambda i: (0, i, 0)),
            pl.BlockSpec((tm, MLP_WIDTH), lambda i: (i, 0)),
            _const_spec((MODS_ROWS, 6 * D_MODEL)),
            _const_spec((1, D_MODEL)),
            _const_spec((D_MODEL, D_MODEL)),
            _const_spec((D_MODEL, 2 * D_FF)),
            _const_spec((D_FF, D_MODEL)),
        ],
        out_specs=pl.BlockSpec((tm, D_MODEL), lambda i: (i, 0)),
        out_shape=jax.ShapeDtypeStruct((n_tok, D_MODEL), F32),
        compiler_params=_params(1),
        name="ffn",
    )(x2d, att, mlp, mods3, g2, w_out, w_ffn_in, w_ffn_out)


def _rope_tables(n):
    pos = np.arange(n)
    row = (pos // GRID_W).astype(np.float32)
    col = (pos % GRID_W).astype(np.float32)
    inv = (ROPE_THETA ** (-np.arange(0, ROPE_AXIS_DIM, 2, dtype=np.float32) / ROPE_AXIS_DIM)
           ).astype(np.float32)
    ang_r = row[:, None] * inv[None, :]
    ang_c = col[:, None] * inv[None, :]
    cos64 = np.concatenate([np.cos(ang_r)] * 2 + [np.cos(ang_c)] * 2, axis=1)
    sin64 = np.concatenate([-np.sin(ang_r), np.sin(ang_r), -np.sin(ang_c), np.sin(ang_c)], axis=1)
    return (np.ascontiguousarray(np.tile(cos64, (1, 2)).T, np.float32),
            np.ascontiguousarray(np.tile(sin64, (1, 2)).T, np.float32))


def kernel(x_prompt, x_sample, cache_k_ctx, cache_v_ctx, c, c_ctx, norm1_g, norm2_g, w_ada, b_ada, w_in, q_norm_g, k_norm_g, lambda_q1, lambda_k1, lambda_q2, lambda_k2, att_out_g, sgu_norm_g, sgu_w, sgu_b, mlp_out_g, w_out, w_ffn_in, w_ffn_out):
    n_ctx, ctx_len, _ = x_prompt.shape
    n_dec, dec_len, _ = x_sample.shape
    depth = norm1_g.shape[0]

    rope_tabs = _rope_tables(dec_len)
    q_scale = LOG2E / math.sqrt(HEAD_DIM)

    xp = x_prompt.reshape(n_ctx * ctx_len, D_MODEL)
    xs = x_sample.reshape(n_dec * dec_len, D_MODEL)
    cache_kt = jnp.swapaxes(cache_k_ctx, -1, -2)
    k_states, v_states = [], []
    ctx_row = lambda i: CTX_ROW
    dec_row_proj = lambda i: i // (dec_len // TM_PROJ)
    dec_row_ffn = lambda i: i // (dec_len // TM_FFN)

    for l in range(depth):
        lambda_init = 0.8 - 0.6 * math.exp(-0.3 * l)
        mods3 = _mods(c, c_ctx[None, :], w_ada[l], b_ada[l][None, :])
        g1 = norm1_g[l][None, :]
        g2 = norm2_g[l][None, :]
        w_in_b = w_in[l].astype(BF16)
        qg = jnp.broadcast_to(jnp.tile(q_norm_g[l] * q_scale, 2)[:, None], (LANES, LANES))
        kg = jnp.broadcast_to(jnp.tile(k_norm_g[l], 2)[:, None], (LANES, LANES))
        sgun = sgu_norm_g[l][None, :]
        sguw = sgu_w[l].astype(BF16)
        sgub = jnp.broadcast_to(sgu_b[l][:, :, None], (N_GROUPS, CHUNK, GROUP_DIM))
        mlpg = mlp_out_g[l][None, :]
        att_g = att_out_g[l].reshape(N_HEADS, 1, VAL_DIM)
        lams = (lambda_q1[l][None, :], lambda_k1[l][None, :],
                lambda_q2[l][None, :], lambda_k2[l][None, :])

        q, k, vt, mlp_s = _proj(
            xs, mods3, g1, w_in_b, qg, kg, rope_tabs, sgun, sguw, sgub, mlpg,
            seq_len=dec_len, mods_row_fn=dec_row_proj)
        att_s, w_out_b, w_fi_b, w_fo_b = _attn_cache(
            lams, q, k, vt, cache_kt, cache_v_ctx, att_g, (w_out[l], w_ffn_in[l], w_ffn_out[l]),
            layer=l, n_batch=n_dec, seq_len=dec_len, lambda_init=lambda_init)

        q, k, vt, mlp, k_c, v_c = _proj(
            xp, mods3, g1, w_in_b, qg, kg, None, sgun, sguw, sgub, mlpg,
            seq_len=ctx_len, mods_row_fn=ctx_row)
        att = _attn_ctx(lams, q, k, vt, att_g, seq_len=ctx_len, lambda_init=lambda_init)
        xp = _ffn(xp, att, mlp, mods3, g2, w_out_b, w_fi_b, w_fo_b, mods_row_fn=ctx_row)
        k_states.append(k_c)
        v_states.append(v_c)

        xs = _ffn(xs, att_s, mlp_s, mods3, g2, w_out_b, w_fi_b, w_fo_b, mods_row_fn=dec_row_ffn)

    state_k = jnp.swapaxes(jnp.concatenate(k_states, axis=1), -1, -2)
    state_v = jnp.concatenate(v_states, axis=1)
    return (xp.reshape(n_ctx, ctx_len, D_MODEL), xs.reshape(n_dec, dec_len, D_MODEL),
            state_k, state_v)
```

```python
import functools
import math

import jax
import jax.numpy as jnp
import numpy as np
from jax import lax
from jax.experimental import pallas as pl
from jax.experimental.pallas import tpu as pltpu

D_MODEL = 1024
ATT_WIDTH = 512
N_HEADS = 4
HEAD_DIM = 64
VAL_DIM = 128
MLP_WIDTH = 512
N_GROUPS = 4
GROUP_DIM = 128
CHUNK = 128
D_FF = 2816
IN_WIDTH = 2560
GRID_W = 64
ROPE_THETA = 10000.0
ROPE_AXIS_DIM = 32
EPS = 1e-6
LOG2E = 1.4426950408889634
LANES = 128

F32 = jnp.float32
BF16 = jnp.bfloat16

VMEM_LIMIT_BYTES = 56 * 1024 * 1024
MODS_ROWS = 16
CTX_ROW = 8

TM_PROJ = 1024
TM_FFN = 512
TM_ATTN_CTX = 2048
CTX_AHEAD = 3
FF_CHUNKS = ((0, 1024), (1024, 1024), (2048, 768))

ONES_ROWS = 16
TQ_UNIT = 256
KEY_BLOCK = 256
SCORE_LEAD = 2
PAIRS_PER_TRIP = 5


def _const_spec(shape):
    zeros = (0,) * len(shape)
    return pl.BlockSpec(shape, lambda *_: zeros, pipeline_mode=pl.Buffered(1))


def _params(n_grid):
    return pltpu.CompilerParams(
        dimension_semantics=("arbitrary",) * n_grid,
        vmem_limit_bytes=VMEM_LIMIT_BYTES,
    )


def _rms_scale(x):
    return lax.rsqrt(jnp.mean(x * x, axis=-1, keepdims=True) + EPS)


def _mod(mods_ref, row, k):
    return mods_ref[pl.ds(row, 1), k * D_MODEL:(k + 1) * D_MODEL]


def _mods_kernel(c_ref, cctx_ref, w_ref, b_ref, w_in_ref, o_ref, w_in_b_ref):
    cnd = jnp.concatenate(
        [c_ref[...], jnp.broadcast_to(cctx_ref[...], (MODS_ROWS - CTX_ROW, D_MODEL))], axis=0)
    act = (cnd * jax.nn.sigmoid(cnd)).astype(BF16)
    o_ref[...] = jnp.dot(act, w_ref[...].astype(BF16), preferred_element_type=F32) + b_ref[...]
    w_in_b_ref[...] = w_in_ref[...].astype(BF16)


def _mods(c, c_ctx, w_ada, b_ada, w_in):
    tn = 1536
    n_out = w_ada.shape[1]
    n_steps = n_out // tn
    slab = w_in.shape[0] // n_steps
    assert c.shape[0] == CTX_ROW and w_in.shape[0] % n_steps == 0 and slab % 16 == 0
    w_in_spec = pl.BlockSpec((slab, w_in.shape[1]), lambda j: (j, 0))
    return pl.pallas_call(
        _mods_kernel,
        grid=(n_steps,),
        in_specs=[
            _const_spec((CTX_ROW, D_MODEL)),
            _const_spec((1, D_MODEL)),
            pl.BlockSpec((D_MODEL, tn), lambda j: (0, j)),
            pl.BlockSpec((1, tn), lambda j: (0, j)),
            w_in_spec,
        ],
        out_specs=[pl.BlockSpec((MODS_ROWS, tn), lambda j: (0, j)), w_in_spec],
        out_shape=[jax.ShapeDtypeStruct((MODS_ROWS, n_out), F32),
                   jax.ShapeDtypeStruct(w_in.shape, BF16)],
        compiler_params=_params(1),
        name="mods",
    )(c, c_ctx, w_ada, b_ada, w_in)


def _proj_kernel(*refs, rope, seq_len, tm, mods_row_fn):
    it = iter(refs)
    x_ref, mods_ref, g1_ref, w_in_ref, qg_ref, kg_ref = (next(it) for _ in range(6))
    if rope:
        cos_ref, sin_ref = (next(it) for _ in range(2))
    sgun_ref, sguw_ref, sgub_ref, mlpg_ref = (next(it) for _ in range(4))
    qt_ref, k_ref, vt_ref, mlp_ref = (next(it) for _ in range(4))
    if not rope:
        kst_ref, vst_ref = (next(it) for _ in range(2))
    gate_ref = next(it)

    x = x_ref[...]
    xn = x * _rms_scale(x) * g1_ref[...]
    row = mods_row_fn(pl.program_id(0))
    xm = xn * (1.0 + _mod(mods_ref, row, 1)) + _mod(mods_ref, row, 0)
    xb = xm.astype(BF16)

    def section(lo, hi):
        return jnp.dot(xb, w_in_ref[:, lo:hi], preferred_element_type=F32)

    def head_t(sec, h, g_ref):
        t = sec[:, h * LANES:(h + 1) * LANES].T
        maps = []
        for mp in range(2):
            tmap = t[mp * HEAD_DIM:(mp + 1) * HEAD_DIM, :]
            maps.append(tmap * lax.rsqrt(jnp.mean(tmap * tmap, axis=0, keepdims=True) + EPS))
        gain = jnp.concatenate([g_ref[...]] * (tm // LANES), axis=1)
        tn = jnp.concatenate(maps, axis=0) * gain
        if rope:
            half = ROPE_AXIS_DIM // 2
            swapped = jnp.concatenate(
                [tn[r0 + off:r0 + off + half, :]
                 for r0 in range(0, LANES, ROPE_AXIS_DIM) for off in (half, 0)], axis=0)
            tn = tn * cos_ref[...] + swapped * sin_ref[...]
        return tn

    hu = section(3 * ATT_WIDTH, 3 * ATT_WIDTH + MLP_WIDTH)
    hg = section(3 * ATT_WIDTH + MLP_WIDTH, IN_WIDTH)
    for g in range(N_GROUPS):
        cols = slice(g * GROUP_DIM, (g + 1) * GROUP_DIM)
        gg = hg[:, cols]
        gn = (gg * _rms_scale(gg) * sgun_ref[:, cols]).astype(BF16)
        ug = hu[:, cols]
        wg = sguw_ref[g]
        bg = sgub_ref[g]
        for n in range(tm // CHUNK):
            rows = slice(n * CHUNK, (n + 1) * CHUNK)
            sp = jnp.dot(wg, gn[rows, :], preferred_element_type=F32) + bg
            gate_ref[rows, cols] = ug[rows, :] * sp
    o = gate_ref[...]
    mlp_ref[...] = (o * _rms_scale(o) * mlpg_ref[...]).astype(BF16)

    hq = section(0, ATT_WIDTH)
    hk = section(ATT_WIDTH, 2 * ATT_WIDTH)
    for h in range(N_HEADS):
        qt_ref[h] = head_t(hq, h, qg_ref).astype(BF16)
        kt = head_t(hk, h, kg_ref)
        k_ref[h] = kt.T.astype(BF16)
        if not rope:
            for s in range(tm // seq_len):
                for i in range(2):
                    kst_ref[s, 0, h, i, :, :] = kt[i * HEAD_DIM:(i + 1) * HEAD_DIM,
                                                   s * seq_len:(s + 1) * seq_len]

    hv = section(2 * ATT_WIDTH, 3 * ATT_WIDTH)
    hvt = hv.T.astype(BF16)
    for h in range(N_HEADS):
        vt_ref[h] = hvt[h * VAL_DIM:(h + 1) * VAL_DIM, :]
        if not rope:
            for s in range(tm // seq_len):
                vst_ref[s, 0, h, :, :] = hv[s * seq_len:(s + 1) * seq_len,
                                            h * VAL_DIM:(h + 1) * VAL_DIM]


def _proj(x2d, mods3, g1, w_in, qg, kg, rope_tabs, sgun, sguw, sgub, mlpg,
          *, seq_len, mods_row_fn):
    n_tok = x2d.shape[0]
    tm = TM_PROJ
    rope = rope_tabs is not None
    blocks_per_seq = seq_len // tm if rope else None

    in_specs = [
        pl.BlockSpec((tm, D_MODEL), lambda i: (i, 0)),
        _const_spec((MODS_ROWS, 6 * D_MODEL)),
        _const_spec((1, D_MODEL)),
        _const_spec((D_MODEL, IN_WIDTH)),
        _const_spec((LANES, LANES)),
        _const_spec((LANES, LANES)),
    ]
    args = [x2d, mods3, g1, w_in, qg, kg]
    if rope:
        tab_spec = pl.BlockSpec((LANES, tm), lambda i: (0, i % blocks_per_seq))
        in_specs += [tab_spec] * 2
        args += list(rope_tabs)
    in_specs += [
        _const_spec((1, MLP_WIDTH)),
        _const_spec((N_GROUPS, CHUNK, CHUNK)),
        _const_spec((N_GROUPS, CHUNK, GROUP_DIM)),
        _const_spec((1, MLP_WIDTH)),
    ]
    args += [sgun, sguw, sgub, mlpg]

    head_spec = pl.BlockSpec((N_HEADS, tm, LANES), lambda i: (0, i, 0))
    head_t_spec = pl.BlockSpec((N_HEADS, LANES, tm), lambda i: (0, 0, i))
    head_shape = jax.ShapeDtypeStruct((N_HEADS, n_tok, LANES), BF16)
    head_t_shape = jax.ShapeDtypeStruct((N_HEADS, LANES, n_tok), BF16)
    out_specs = [head_t_spec, head_spec, head_t_spec,
                 pl.BlockSpec((tm, MLP_WIDTH), lambda i: (i, 0))]
    out_shape = [head_t_shape, head_shape, head_t_shape,
                 jax.ShapeDtypeStruct((n_tok, MLP_WIDTH), BF16)]
    if not rope:
        n_seq = n_tok // seq_len
        spb = tm // seq_len
        out_specs += [
            pl.BlockSpec((spb, 1, N_HEADS, 2, HEAD_DIM, seq_len), lambda i: (i, 0, 0, 0, 0, 0)),
            pl.BlockSpec((spb, 1, N_HEADS, seq_len, VAL_DIM), lambda i: (i, 0, 0, 0, 0)),
        ]
        out_shape += [
            jax.ShapeDtypeStruct((n_seq, 1, N_HEADS, 2, HEAD_DIM, seq_len), F32),
            jax.ShapeDtypeStruct((n_seq, 1, N_HEADS, seq_len, VAL_DIM), F32),
        ]

    return pl.pallas_call(
        functools.partial(_proj_kernel, rope=rope, seq_len=seq_len, tm=tm, mods_row_fn=mods_row_fn),
        grid=(n_tok // tm,),
        in_specs=in_specs,
        out_specs=out_specs,
        out_shape=out_shape,
        scratch_shapes=[pltpu.VMEM((tm, MLP_WIDTH), F32)],
        compiler_params=_params(1),
        name="proj_rope" if rope else "proj_ctx",
    )(*args)


def _lambda_full(lq1, lk1, lq2, lk2, lambda_init):
    return (jnp.exp(jnp.sum(lq1[...] * lk1[...], keepdims=True))
            - jnp.exp(jnp.sum(lq2[...] * lk2[...], keepdims=True))
            + lambda_init)


def _map_queries(qt):
    row = lax.broadcasted_iota(jnp.int32, qt.shape, 0)
    zero = jnp.zeros_like(qt)
    return (jnp.where(row < HEAD_DIM, qt, zero), jnp.where(row >= HEAD_DIM, qt, zero))


def _combine_maps(o1, d1, o2, d2, lam, out_gain):
    ot = o1 * (1.0 / d1) - o2 * (lam / d2)
    ot = ot * lax.rsqrt(jnp.mean(ot * ot, axis=0, keepdims=True) + EPS)
    return (ot.T * out_gain).astype(BF16)


def _attn_ctx_kernel(lq1, lk1, lq2, lk2, qt_ref, k_ref, vt_ref, ag_ref, o_ref, st_buf,
                     *, lambda_init, seq_len):
    lam = _lambda_full(lq1, lk1, lq2, lk2, lambda_init)
    units = [(slice(s * seq_len, (s + 1) * seq_len), h)
             for s in range(k_ref.shape[1] // seq_len) for h in range(N_HEADS)]

    def scores(u):
        rows, h = units[u]
        kk = k_ref[h, rows, :]
        maxes = []
        for mp, qm in enumerate(_map_queries(qt_ref[h, :, rows])):
            st = jnp.dot(kk, qm, preferred_element_type=F32)
            st_buf[u % CTX_AHEAD, mp] = st
            maxes.append(jnp.max(st, axis=0, keepdims=True))
        return maxes

    def finish(u, maxes):
        rows, h = units[u]
        vt = vt_ref[h, :, rows]
        outs, dens = [], []
        for mp in range(2):
            e = jnp.exp2(st_buf[u % CTX_AHEAD, mp] - maxes[mp])
            dens.append(jnp.sum(e, axis=0, keepdims=True))
            outs.append(jnp.dot(vt, e.astype(BF16), preferred_element_type=F32))
        out_gain = (1.0 - lambda_init) * ag_ref[h]
        o_ref[h, rows, :] = _combine_maps(outs[0], dens[0], outs[1], dens[1], lam, out_gain)

    pending = [scores(u) for u in range(CTX_AHEAD - 1)]
    for u in range(len(units)):
        if u + CTX_AHEAD - 1 < len(units):
            pending.append(scores(u + CTX_AHEAD - 1))
        finish(u, pending.pop(0))


def _attn_ctx(lams, q, k, vt, att_g, *, seq_len, lambda_init):
    n_tok = k.shape[1]
    tm = TM_ATTN_CTX
    head_spec = pl.BlockSpec((N_HEADS, tm, LANES), lambda i: (0, i, 0))
    head_t_spec = pl.BlockSpec((N_HEADS, LANES, tm), lambda i: (0, 0, i))
    return pl.pallas_call(
        functools.partial(_attn_ctx_kernel, lambda_init=lambda_init, seq_len=seq_len),
        grid=(n_tok // tm,),
        in_specs=[_const_spec((1, HEAD_DIM))] * 4 + [
            head_t_spec,
            head_spec,
            head_t_spec,
            _const_spec((N_HEADS, 1, LANES)),
        ],
        out_specs=head_spec,
        out_shape=jax.ShapeDtypeStruct((N_HEADS, n_tok, LANES), BF16),
        scratch_shapes=[pltpu.VMEM((CTX_AHEAD, 2, seq_len, seq_len), F32)],
        compiler_params=_params(1),
        name="attn_ctx",
    )(*lams, q, k, vt, att_g)


def _attn_cache_kernel(lq1, lk1, lq2, lk2, qt_ref, k_ref, vt_ref, kct_ref, vc_ref, ag_ref,
                       *rest, lambda_init, n_new, n_weights):
    for w_ref, wb_ref in zip(rest[:n_weights], rest[n_weights + 1:2 * n_weights + 1]):
        wb_ref[...] = w_ref[...].astype(BF16)
    o_ref = rest[n_weights]
    k_all, vt_all, m_buf, acc_buf, *bufs = rest[2 * n_weights + 1:]
    st = (bufs[0:2], bufs[2:4])
    n_chunks = n_new // TQ_UNIT
    n_units = N_HEADS * n_chunks
    n_keys = k_all.shape[1]

    past = n_keys - n_new
    for h in range(N_HEADS):
        k_all[h, 0:n_new, :] = k_ref[h]
        k_all[h, n_new:, :] = kct_ref[0, 0, h].reshape(2 * HEAD_DIM, past).T.astype(BF16)
        vt_all[h, 0:VAL_DIM, 0:n_new] = vt_ref[h]
        vt_all[h, 0:VAL_DIM, n_new:] = vc_ref[0, 0, h].T.astype(BF16)
        vt_all[h, VAL_DIM:, :] = jnp.ones((ONES_ROWS, n_keys), BF16)

    lam = _lambda_full(lq1, lk1, lq2, lk2, lambda_init)

    def head_rows(u):
        c = u % n_chunks
        return u // n_chunks, pl.ds(pl.multiple_of(c * TQ_UNIT, TQ_UNIT), TQ_UNIT)

    def stage(fin, sc, defer_out=False):
        if sc is not None:
            sc_head, sc_rows = head_rows(sc[0])
            qms = _map_queries(qt_ref[sc_head, :, sc_rows])
            mrun = [None, None]
        if fin is not None:
            fin_head = fin[0] // n_chunks
            ms = [m_buf[fin[1], mp] for mp in range(2)]
            accs = [None, None]
        n_kb = n_keys // KEY_BLOCK
        lead = SCORE_LEAD if (sc is not None and fin is not None) else 0
        for step in range(n_kb + lead):
            if sc is not None and step < n_kb:
                kr = slice(step * KEY_BLOCK, (step + 1) * KEY_BLOCK)
                kk = k_all[sc_head, kr, :]
                for mp in range(2):
                    s = jnp.dot(kk, qms[mp], preferred_element_type=F32)
                    st[sc[1]][mp][kr, :] = s
                    smax = jnp.max(s.reshape(KEY_BLOCK // 8, 8, TQ_UNIT), axis=0)
                    mrun[mp] = smax if mrun[mp] is None else jnp.maximum(mrun[mp], smax)
            if fin is not None and step >= lead:
                kr = slice((step - lead) * KEY_BLOCK, (step - lead + 1) * KEY_BLOCK)
                vt = vt_all[fin_head, :, kr]
                for mp in range(2):
                    p = jnp.exp2(st[fin[1]][mp][kr, :] - ms[mp]).astype(BF16)
                    d = jnp.dot(vt, p, preferred_element_type=F32)
                    accs[mp] = d if accs[mp] is None else accs[mp] + d
        if sc is not None:
            for mp in range(2):
                m_buf[sc[1], mp] = jnp.max(mrun[mp], axis=0, keepdims=True)
        if fin is not None:
            if defer_out:
                for mp in range(2):
                    acc_buf[mp] = accs[mp]
            else:
                write_out(fin[0], accs[0], accs[1])

    def write_out(u, o1, o2):
        head, rows = head_rows(u)
        out_gain = (1.0 - lambda_init) * ag_ref[head]
        o_ref[head, rows, :] = _combine_maps(
            o1[0:VAL_DIM, :], o1[VAL_DIM:VAL_DIM + 1, :],
            o2[0:VAL_DIM, :], o2[VAL_DIM:VAL_DIM + 1, :], lam, out_gain)

    stage(None, (0, 0))
    stage((0, 0), (1, 1), defer_out=True)

    def pair(i):
        u = 2 * i
        write_out(u - 2, acc_buf[0], acc_buf[1])
        stage((u - 1, 1), (u, 0))
        stage((u, 0), (u + 1, 1), defer_out=True)

    def trip(j, carry):
        for p in range(PAIRS_PER_TRIP):
            pair(PAIRS_PER_TRIP * j + 1 + p)
        return carry

    n_pairs = n_units // 2
    n_trips = (n_pairs - 1) // PAIRS_PER_TRIP
    lax.fori_loop(0, n_trips, trip, 0)
    for i in range(n_trips * PAIRS_PER_TRIP + 1, n_pairs):
        pair(i)
    write_out(n_units - 2, acc_buf[0], acc_buf[1])
    stage((n_units - 1, 1), None)


def _attn_cache(lams, q, k, vt, kct, vc, att_g, weights, *, layer, n_batch, seq_len, lambda_init):
    past = vc.shape[3]
    n_keys = seq_len + past
    assert (seq_len // TQ_UNIT) % 2 == 0 and seq_len // TQ_UNIT >= 4
    assert n_keys % KEY_BLOCK == 0
    head_spec = pl.BlockSpec((N_HEADS, seq_len, LANES), lambda b: (0, b, 0))
    head_t_spec = pl.BlockSpec((N_HEADS, LANES, seq_len), lambda b: (0, 0, b))
    in_specs = [_const_spec((1, HEAD_DIM))] * 4 + [
        head_t_spec,
        head_spec,
        head_t_spec,
        pl.BlockSpec((1, 1, N_HEADS, 2, HEAD_DIM, past), lambda b: (b, layer, 0, 0, 0, 0)),
        pl.BlockSpec((1, 1, N_HEADS, past, VAL_DIM), lambda b: (b, layer, 0, 0, 0)),
        _const_spec((N_HEADS, 1, LANES)),
    ]
    w_specs = []
    for w in weights:
        slab = w.shape[0] // n_batch
        assert w.shape[0] % n_batch == 0 and slab % 16 == 0
        w_specs.append(pl.BlockSpec((slab, w.shape[1]), lambda b: (b, 0)))
    return pl.pallas_call(
        functools.partial(_attn_cache_kernel, lambda_init=lambda_init, n_new=seq_len,
                          n_weights=len(weights)),
        grid=(n_batch,),
        in_specs=in_specs + w_specs,
        out_specs=[head_spec] + w_specs,
        out_shape=[jax.ShapeDtypeStruct((N_HEADS, n_batch * seq_len, LANES), BF16)]
                  + [jax.ShapeDtypeStruct(w.shape, BF16) for w in weights],
        scratch_shapes=([pltpu.VMEM((N_HEADS, n_keys, LANES), BF16),
                         pltpu.VMEM((N_HEADS, VAL_DIM + ONES_ROWS, n_keys), BF16),
                         pltpu.VMEM((2, 2, 1, TQ_UNIT), F32),
                         pltpu.VMEM((2, VAL_DIM + ONES_ROWS, TQ_UNIT), F32)]
                        + [pltpu.VMEM((n_keys, TQ_UNIT), F32)] * 4),
        compiler_params=_params(1),
        name="attn_cache",
    )(*lams, q, k, vt, kct, vc, att_g, *weights)


def _ffn_kernel(x_ref, att_ref, mlp_ref, mods_ref, g2_ref, wo_ref, wfi_ref, wfo_ref, o_ref,
                *, mods_row_fn):
    row = mods_row_fn(pl.program_id(0))
    gate1, shift2, scale2, gate2 = (_mod(mods_ref, row, k) for k in (2, 3, 4, 5))
    tm = x_ref.shape[0]
    halves = [slice(i * (tm // 2), (i + 1) * (tm // 2)) for i in range(2)]
    x1s, xbs = [], []
    for r in halves:
        att = jnp.concatenate([att_ref[h, r, :] for h in range(N_HEADS)], axis=1)
        y = (jnp.dot(att, wo_ref[0:ATT_WIDTH, :], preferred_element_type=F32)
             + jnp.dot(mlp_ref[r, :], wo_ref[ATT_WIDTH:, :], preferred_element_type=F32))
        x1 = x_ref[r, :] + gate1 * y
        xn = x1 * _rms_scale(x1) * g2_ref[...]
        x1s.append(x1)
        xbs.append((xn * (1.0 + scale2) + shift2).astype(BF16))
    accs = [None, None]
    for c0, cw in FF_CHUNKS:
        pre = []
        for xb in xbs:
            gte = jnp.dot(xb, wfi_ref[:, c0:c0 + cw], preferred_element_type=F32)
            up = jnp.dot(xb, wfi_ref[:, D_FF + c0:D_FF + c0 + cw], preferred_element_type=F32)
            pre.append((gte, up))
        for i, (gte, up) in enumerate(pre):
            act = (gte * jax.nn.sigmoid(gte) * up).astype(BF16)
            part = jnp.dot(act, wfo_ref[c0:c0 + cw, :], preferred_element_type=F32)
            accs[i] = part if accs[i] is None else accs[i] + part
    for r, x1, acc in zip(halves, x1s, accs):
        o_ref[r, :] = x1 + gate2 * acc


def _ffn(x2d, att, mlp, mods3, g2, w_out, w_ffn_in, w_ffn_out, *, mods_row_fn):
    n_tok = x2d.shape[0]
    tm = TM_FFN
    return pl.pallas_call(
        functools.partial(_ffn_kernel, mods_row_fn=mods_row_fn),
        grid=(n_tok // tm,),
        in_specs=[
            pl.BlockSpec((tm, D_MODEL), lambda i: (i, 0)),
            pl.BlockSpec((N_HEADS, tm, LANES), lambda i: (0, i, 0)),
            pl.BlockSpec((tm, MLP_WIDTH), lambda i: (i, 0)),
            _const_spec((MODS_ROWS, 6 * D_MODEL)),
            _const_spec((1, D_MODEL)),
            _const_spec((D_MODEL, D_MODEL)),
            _const_spec((D_MODEL, 2 * D_FF)),
            _const_spec((D_FF, D_MODEL)),
        ],
        out_specs=pl.BlockSpec((tm, D_MODEL), lambda i: (i, 0)),
        out_shape=jax.ShapeDtypeStruct((n_tok, D_MODEL), F32),
        compiler_params=_params(1),
        name="ffn",
    )(x2d, att, mlp, mods3, g2, w_out, w_ffn_in, w_ffn_out)


def _rope_tables(n):
    pos = np.arange(n)
    row = (pos // GRID_W).astype(np.float32)
    col = (pos % GRID_W).astype(np.float32)
    inv = (ROPE_THETA ** (-np.arange(0, ROPE_AXIS_DIM, 2, dtype=np.float32) / ROPE_AXIS_DIM)
           ).astype(np.float32)
    ang_r = row[:, None] * inv[None, :]
    ang_c = col[:, None] * inv[None, :]
    cos64 = np.concatenate([np.cos(ang_r)] * 2 + [np.cos(ang_c)] * 2, axis=1)
    sin64 = np.concatenate([-np.sin(ang_r), np.sin(ang_r), -np.sin(ang_c), np.sin(ang_c)], axis=1)
    return (np.ascontiguousarray(np.tile(cos64, (1, 2)).T, np.float32),
            np.ascontiguousarray(np.tile(sin64, (1, 2)).T, np.float32))


def kernel(x_prompt, x_sample, cache_k_ctx, cache_v_ctx, c, c_ctx, norm1_g, norm2_g, w_ada, b_ada, w_in, q_norm_g, k_norm_g, lambda_q1, lambda_k1, lambda_q2, lambda_k2, att_out_g, sgu_norm_g, sgu_w, sgu_b, mlp_out_g, w_out, w_ffn_in, w_ffn_out):
    n_ctx, ctx_len, _ = x_prompt.shape
    n_dec, dec_len, _ = x_sample.shape
    depth = norm1_g.shape[0]

    rope_tabs = _rope_tables(dec_len)
    q_scale = LOG2E / math.sqrt(HEAD_DIM)

    xp = x_prompt.reshape(n_ctx * ctx_len, D_MODEL)
    xs = x_sample.reshape(n_dec * dec_len, D_MODEL)
    cache_kt = jnp.swapaxes(cache_k_ctx, -1, -2)
    k_states, v_states = [], []
    ctx_row = lambda i: CTX_ROW
    dec_row_proj = lambda i: i // (dec_len // TM_PROJ)
    dec_row_ffn = lambda i: i // (dec_len // TM_FFN)

    for l in range(depth):
        lambda_init = 0.8 - 0.6 * math.exp(-0.3 * l)
        mods3, w_in_b = _mods(c, c_ctx[None, :], w_ada[l], b_ada[l][None, :], w_in[l])
        g1 = norm1_g[l][None, :]
        g2 = norm2_g[l][None, :]
        qg = jnp.broadcast_to(jnp.tile(q_norm_g[l] * q_scale, 2)[:, None], (LANES, LANES))
        kg = jnp.broadcast_to(jnp.tile(k_norm_g[l], 2)[:, None], (LANES, LANES))
        sgun = sgu_norm_g[l][None, :]
        sguw = sgu_w[l].astype(BF16)
        sgub = jnp.broadcast_to(sgu_b[l][:, :, None], (N_GROUPS, CHUNK, GROUP_DIM))
        mlpg = mlp_out_g[l][None, :]
        att_g = att_out_g[l].reshape(N_HEADS, 1, VAL_DIM)
        lams = (lambda_q1[l][None, :], lambda_k1[l][None, :],
                lambda_q2[l][None, :], lambda_k2[l][None, :])

        q, k, vt, mlp_s = _proj(
            xs, mods3, g1, w_in_b, qg, kg, rope_tabs, sgun, sguw, sgub, mlpg,
            seq_len=dec_len, mods_row_fn=dec_row_proj)
        att_s, w_out_b, w_fi_b, w_fo_b = _attn_cache(
            lams, q, k, vt, cache_kt, cache_v_ctx, att_g, (w_out[l], w_ffn_in[l], w_ffn_out[l]),
            layer=l, n_batch=n_dec, seq_len=dec_len, lambda_init=lambda_init)

        q, k, vt, mlp, k_c, v_c = _proj(
            xp, mods3, g1, w_in_b, qg, kg, None, sgun, sguw, sgub, mlpg,
            seq_len=ctx_len, mods_row_fn=ctx_row)
        att = _attn_ctx(lams, q, k, vt, att_g, seq_len=ctx_len, lambda_init=lambda_init)
        xp = _ffn(xp, att, mlp, mods3, g2, w_out_b, w_fi_b, w_fo_b, mods_row_fn=ctx_row)
        k_states.append(k_c)
        v_states.append(v_c)

        xs = _ffn(xs, att_s, mlp_s, mods3, g2, w_out_b, w_fi_b, w_fo_b, mods_row_fn=dec_row_ffn)

    state_k = jnp.swapaxes(jnp.concatenate(k_states, axis=1), -1, -2)
    state_v = jnp.concatenate(v_states, axis=1)
    return (xp.reshape(n_ctx, ctx_len, D_MODEL), xs.reshape(n_dec, dec_len, D_MODEL),
            state_k, state_v)
```

```python
import functools
import math

import jax
import jax.numpy as jnp
import numpy as np
from jax import lax
from jax.experimental import pallas as pl
from jax.experimental.pallas import tpu as pltpu

D_MODEL = 1024
ATT_WIDTH = 512
N_HEADS = 4
HEAD_DIM = 64
VAL_DIM = 128
MLP_WIDTH = 512
N_GROUPS = 4
GROUP_DIM = 128
CHUNK = 128
D_FF = 2816
IN_WIDTH = 2560
GRID_W = 64
ROPE_THETA = 10000.0
ROPE_AXIS_DIM = 32
EPS = 1e-6
LOG2E = 1.4426950408889634
LANES = 128

F32 = jnp.float32
BF16 = jnp.bfloat16

VMEM_LIMIT_BYTES = 56 * 1024 * 1024
MODS_ROWS = 16
CTX_ROW = 8

TM_PROJ = 1024
TM_FFN = 512
TM_ATTN_CTX = 2048
CTX_AHEAD = 3
FF_CHUNKS = ((0, 1024), (1024, 1024), (2048, 768))

ONES_ROWS = 16
TQ_UNIT = 256
KEY_BLOCK = 256
SCORE_LEAD = 2
PAIRS_PER_TRIP = 5


def _const_spec(shape):
    zeros = (0,) * len(shape)
    return pl.BlockSpec(shape, lambda *_: zeros, pipeline_mode=pl.Buffered(1))


def _params(n_grid):
    return pltpu.CompilerParams(
        dimension_semantics=("arbitrary",) * n_grid,
        vmem_limit_bytes=VMEM_LIMIT_BYTES,
    )


def _rms_scale(x):
    return lax.rsqrt(jnp.mean(x * x, axis=-1, keepdims=True) + EPS)


def _mod(mods_ref, row, k):
    return mods_ref[pl.ds(row, 1), k * D_MODEL:(k + 1) * D_MODEL]


def _mods_kernel(c_ref, cctx_ref, w_ref, b_ref, *rest):
    n_weights = (len(rest) - 1) // 2
    o_ref = rest[n_weights]
    cnd = jnp.concatenate(
        [c_ref[...], jnp.broadcast_to(cctx_ref[...], (MODS_ROWS - CTX_ROW, D_MODEL))], axis=0)
    act = (cnd * jax.nn.sigmoid(cnd)).astype(BF16)
    o_ref[...] = jnp.dot(act, w_ref[...].astype(BF16), preferred_element_type=F32) + b_ref[...]
    for w32_ref, w16_ref in zip(rest[:n_weights], rest[n_weights + 1:]):
        w16_ref[...] = w32_ref[...].astype(BF16)


def _mods(c, c_ctx, w_ada, b_ada, weights):
    tn = 1536
    n_out = w_ada.shape[1]
    n_steps = n_out // tn
    assert c.shape[0] == CTX_ROW
    w_specs = []
    for w in weights:
        slab = w.shape[0] // n_steps
        assert w.shape[0] % n_steps == 0 and slab % 16 == 0
        w_specs.append(pl.BlockSpec((slab, w.shape[1]), lambda j: (j, 0)))
    return pl.pallas_call(
        _mods_kernel,
        grid=(n_steps,),
        in_specs=[
            _const_spec((CTX_ROW, D_MODEL)),
            _const_spec((1, D_MODEL)),
            pl.BlockSpec((D_MODEL, tn), lambda j: (0, j)),
            pl.BlockSpec((1, tn), lambda j: (0, j)),
        ] + w_specs,
        out_specs=[pl.BlockSpec((MODS_ROWS, tn), lambda j: (0, j))] + w_specs,
        out_shape=[jax.ShapeDtypeStruct((MODS_ROWS, n_out), F32)]
                  + [jax.ShapeDtypeStruct(w.shape, BF16) for w in weights],
        compiler_params=_params(1),
        name="mods",
    )(c, c_ctx, w_ada, b_ada, *weights)


def _proj_kernel(*refs, rope, seq_len, tm, mods_row_fn):
    it = iter(refs)
    x_ref, mods_ref, g1_ref, w_in_ref, qg_ref, kg_ref = (next(it) for _ in range(6))
    if rope:
        cos_ref, sin_ref = (next(it) for _ in range(2))
    sgun_ref, sguw_ref, sgub_ref, mlpg_ref = (next(it) for _ in range(4))
    qt_ref, k_ref, vt_ref, mlp_ref = (next(it) for _ in range(4))
    if not rope:
        kst_ref, vst_ref = (next(it) for _ in range(2))
    gate_ref = next(it)

    x = x_ref[...]
    xn = x * _rms_scale(x) * g1_ref[...]
    row = mods_row_fn(pl.program_id(0))
    xm = xn * (1.0 + _mod(mods_ref, row, 1)) + _mod(mods_ref, row, 0)
    xb = xm.astype(BF16)

    def section(lo, hi):
        return jnp.dot(xb, w_in_ref[:, lo:hi], preferred_element_type=F32)

    def head_t(sec, h, g_ref):
        t = sec[:, h * LANES:(h + 1) * LANES].T
        maps = []
        for mp in range(2):
            tmap = t[mp * HEAD_DIM:(mp + 1) * HEAD_DIM, :]
            maps.append(tmap * lax.rsqrt(jnp.mean(tmap * tmap, axis=0, keepdims=True) + EPS))
        gain = jnp.concatenate([g_ref[...]] * (tm // LANES), axis=1)
        tn = jnp.concatenate(maps, axis=0) * gain
        if rope:
            half = ROPE_AXIS_DIM // 2
            swapped = jnp.concatenate(
                [tn[r0 + off:r0 + off + half, :]
                 for r0 in range(0, LANES, ROPE_AXIS_DIM) for off in (half, 0)], axis=0)
            tn = tn * cos_ref[...] + swapped * sin_ref[...]
        return tn

    hu = section(3 * ATT_WIDTH, 3 * ATT_WIDTH + MLP_WIDTH)
    hg = section(3 * ATT_WIDTH + MLP_WIDTH, IN_WIDTH)
    for g in range(N_GROUPS):
        cols = slice(g * GROUP_DIM, (g + 1) * GROUP_DIM)
        gg = hg[:, cols]
        gn = (gg * _rms_scale(gg) * sgun_ref[:, cols]).astype(BF16)
        ug = hu[:, cols]
        wg = sguw_ref[g]
        bg = sgub_ref[g]
        for n in range(tm // CHUNK):
            rows = slice(n * CHUNK, (n + 1) * CHUNK)
            sp = jnp.dot(wg, gn[rows, :], preferred_element_type=F32) + bg
            gate_ref[rows, cols] = ug[rows, :] * sp
    o = gate_ref[...]
    mlp_ref[...] = (o * _rms_scale(o) * mlpg_ref[...]).astype(BF16)

    hq = section(0, ATT_WIDTH)
    hk = section(ATT_WIDTH, 2 * ATT_WIDTH)
    for h in range(N_HEADS):
        qt_ref[h] = head_t(hq, h, qg_ref).astype(BF16)
        kt = head_t(hk, h, kg_ref)
        k_ref[h] = kt.T.astype(BF16)
        if not rope:
            for s in range(tm // seq_len):
                for i in range(2):
                    kst_ref[s, 0, h, i, :, :] = kt[i * HEAD_DIM:(i + 1) * HEAD_DIM,
                                                   s * seq_len:(s + 1) * seq_len]

    hv = section(2 * ATT_WIDTH, 3 * ATT_WIDTH)
    hvt = hv.T.astype(BF16)
    for h in range(N_HEADS):
        vt_ref[h] = hvt[h * VAL_DIM:(h + 1) * VAL_DIM, :]
        if not rope:
            for s in range(tm // seq_len):
                vst_ref[s, 0, h, :, :] = hv[s * seq_len:(s + 1) * seq_len,
                                            h * VAL_DIM:(h + 1) * VAL_DIM]


def _proj(x2d, mods3, g1, w_in, qg, kg, rope_tabs, sgun, sguw, sgub, mlpg,
          *, seq_len, mods_row_fn):
    n_tok = x2d.shape[0]
    tm = TM_PROJ
    rope = rope_tabs is not None
    blocks_per_seq = seq_len // tm if rope else None

    in_specs = [
        pl.BlockSpec((tm, D_MODEL), lambda i: (i, 0)),
        _const_spec((MODS_ROWS, 6 * D_MODEL)),
        _const_spec((1, D_MODEL)),
        _const_spec((D_MODEL, IN_WIDTH)),
        _const_spec((LANES, LANES)),
        _const_spec((LANES, LANES)),
    ]
    args = [x2d, mods3, g1, w_in, qg, kg]
    if rope:
        tab_spec = pl.BlockSpec((LANES, tm), lambda i: (0, i % blocks_per_seq))
        in_specs += [tab_spec] * 2
        args += list(rope_tabs)
    in_specs += [
        _const_spec((1, MLP_WIDTH)),
        _const_spec((N_GROUPS, CHUNK, CHUNK)),
        _const_spec((N_GROUPS, CHUNK, GROUP_DIM)),
        _const_spec((1, MLP_WIDTH)),
    ]
    args += [sgun, sguw, sgub, mlpg]

    head_spec = pl.BlockSpec((N_HEADS, tm, LANES), lambda i: (0, i, 0))
    head_t_spec = pl.BlockSpec((N_HEADS, LANES, tm), lambda i: (0, 0, i))
    head_shape = jax.ShapeDtypeStruct((N_HEADS, n_tok, LANES), BF16)
    head_t_shape = jax.ShapeDtypeStruct((N_HEADS, LANES, n_tok), BF16)
    out_specs = [head_t_spec, head_spec, head_t_spec,
                 pl.BlockSpec((tm, MLP_WIDTH), lambda i: (i, 0))]
    out_shape = [head_t_shape, head_shape, head_t_shape,
                 jax.ShapeDtypeStruct((n_tok, MLP_WIDTH), BF16)]
    if not rope:
        n_seq = n_tok // seq_len
        spb = tm // seq_len
        out_specs += [
            pl.BlockSpec((spb, 1, N_HEADS, 2, HEAD_DIM, seq_len), lambda i: (i, 0, 0, 0, 0, 0)),
            pl.BlockSpec((spb, 1, N_HEADS, seq_len, VAL_DIM), lambda i: (i, 0, 0, 0, 0)),
        ]
        out_shape += [
            jax.ShapeDtypeStruct((n_seq, 1, N_HEADS, 2, HEAD_DIM, seq_len), F32),
            jax.ShapeDtypeStruct((n_seq, 1, N_HEADS, seq_len, VAL_DIM), F32),
        ]

    return pl.pallas_call(
        functools.partial(_proj_kernel, rope=rope, seq_len=seq_len, tm=tm, mods_row_fn=mods_row_fn),
        grid=(n_tok // tm,),
        in_specs=in_specs,
        out_specs=out_specs,
        out_shape=out_shape,
        scratch_shapes=[pltpu.VMEM((tm, MLP_WIDTH), F32)],
        compiler_params=_params(1),
        name="proj_rope" if rope else "proj_ctx",
    )(*args)


def _lambda_full(lq1, lk1, lq2, lk2, lambda_init):
    return (jnp.exp(jnp.sum(lq1[...] * lk1[...], keepdims=True))
            - jnp.exp(jnp.sum(lq2[...] * lk2[...], keepdims=True))
            + lambda_init)


def _map_queries(qt):
    row = lax.broadcasted_iota(jnp.int32, qt.shape, 0)
    zero = jnp.zeros_like(qt)
    return (jnp.where(row < HEAD_DIM, qt, zero), jnp.where(row >= HEAD_DIM, qt, zero))


def _combine_maps(o1, d1, o2, d2, lam, out_gain):
    ot = o1 * (1.0 / d1) - o2 * (lam / d2)
    ot = ot * lax.rsqrt(jnp.mean(ot * ot, axis=0, keepdims=True) + EPS)
    return (ot.T * out_gain).astype(BF16)


def _attn_ctx_kernel(lq1, lk1, lq2, lk2, qt_ref, k_ref, vt_ref, ag_ref, o_ref, st_buf,
                     *, lambda_init, seq_len):
    lam = _lambda_full(lq1, lk1, lq2, lk2, lambda_init)
    units = [(slice(s * seq_len, (s + 1) * seq_len), h)
             for s in range(k_ref.shape[1] // seq_len) for h in range(N_HEADS)]

    def scores(u):
        rows, h = units[u]
        kk = k_ref[h, rows, :]
        maxes = []
        for mp, qm in enumerate(_map_queries(qt_ref[h, :, rows])):
            st = jnp.dot(kk, qm, preferred_element_type=F32)
            st_buf[u % CTX_AHEAD, mp] = st
            maxes.append(jnp.max(st, axis=0, keepdims=True))
        return maxes

    def finish(u, maxes):
        rows, h = units[u]
        vt = vt_ref[h, :, rows]
        outs, dens = [], []
        for mp in range(2):
            e = jnp.exp2(st_buf[u % CTX_AHEAD, mp] - maxes[mp])
            dens.append(jnp.sum(e, axis=0, keepdims=True))
            outs.append(jnp.dot(vt, e.astype(BF16), preferred_element_type=F32))
        out_gain = (1.0 - lambda_init) * ag_ref[h]
        o_ref[h, rows, :] = _combine_maps(outs[0], dens[0], outs[1], dens[1], lam, out_gain)

    pending = [scores(u) for u in range(CTX_AHEAD - 1)]
    for u in range(len(units)):
        if u + CTX_AHEAD - 1 < len(units):
            pending.append(scores(u + CTX_AHEAD - 1))
        finish(u, pending.pop(0))


def _attn_ctx(lams, q, k, vt, att_g, *, seq_len, lambda_init):
    n_tok = k.shape[1]
    tm = TM_ATTN_CTX
    head_spec = pl.BlockSpec((N_HEADS, tm, LANES), lambda i: (0, i, 0))
    head_t_spec = pl.BlockSpec((N_HEADS, LANES, tm), lambda i: (0, 0, i))
    return pl.pallas_call(
        functools.partial(_attn_ctx_kernel, lambda_init=lambda_init, seq_len=seq_len),
        grid=(n_tok // tm,),
        in_specs=[_const_spec((1, HEAD_DIM))] * 4 + [
            head_t_spec,
            head_spec,
            head_t_spec,
            _const_spec((N_HEADS, 1, LANES)),
        ],
        out_specs=head_spec,
        out_shape=jax.ShapeDtypeStruct((N_HEADS, n_tok, LANES), BF16),
        scratch_shapes=[pltpu.VMEM((CTX_AHEAD, 2, seq_len, seq_len), F32)],
        compiler_params=_params(1),
        name="attn_ctx",
    )(*lams, q, k, vt, att_g)


def _attn_cache_kernel(lq1, lk1, lq2, lk2, qt_ref, k_ref, vt_ref, kct_ref, vc_ref, ag_ref,
                       *rest, lambda_init, n_new, n_weights):
    for w_ref, wb_ref in zip(rest[:n_weights], rest[n_weights + 1:2 * n_weights + 1]):
        wb_ref[...] = w_ref[...].astype(BF16)
    o_ref = rest[n_weights]
    k_past, vt_past, m_buf, acc_buf, *bufs = rest[2 * n_weights + 1:]
    st = (bufs[0:2], bufs[2:4])
    n_chunks = n_new // TQ_UNIT
    n_units = N_HEADS * n_chunks
    past = k_past.shape[1]
    n_keys = n_new + past

    for h in range(N_HEADS):
        k_past[h] = kct_ref[0, 0, h].reshape(2 * HEAD_DIM, past).T.astype(BF16)
        vt_past[h] = vc_ref[0, 0, h].T.astype(BF16)
    ones = jnp.ones((ONES_ROWS, KEY_BLOCK), BF16)

    def key_block(head, kb):
        k0 = kb * KEY_BLOCK
        if k0 < n_new:
            return k_ref[head, k0:k0 + KEY_BLOCK, :]
        return k_past[head, k0 - n_new:k0 - n_new + KEY_BLOCK, :]

    def value_block(head, kb):
        k0 = kb * KEY_BLOCK
        if k0 < n_new:
            vt = vt_ref[head, :, k0:k0 + KEY_BLOCK]
        else:
            vt = vt_past[head, :, k0 - n_new:k0 - n_new + KEY_BLOCK]
        return jnp.concatenate([vt, ones], axis=0)

    lam = _lambda_full(lq1, lk1, lq2, lk2, lambda_init)

    def head_rows(u):
        c = u % n_chunks
        return u // n_chunks, pl.ds(pl.multiple_of(c * TQ_UNIT, TQ_UNIT), TQ_UNIT)

    def stage(fin, sc, defer_out=False):
        if sc is not None:
            sc_head, sc_rows = head_rows(sc[0])
            qms = _map_queries(qt_ref[sc_head, :, sc_rows])
            mrun = [None, None]
        if fin is not None:
            fin_head = fin[0] // n_chunks
            ms = [m_buf[fin[1], mp] for mp in range(2)]
            accs = [None, None]
        n_kb = n_keys // KEY_BLOCK
        lead = SCORE_LEAD if (sc is not None and fin is not None) else 0
        for step in range(n_kb + lead):
            if sc is not None and step < n_kb:
                kr = slice(step * KEY_BLOCK, (step + 1) * KEY_BLOCK)
                kk = key_block(sc_head, step)
                for mp in range(2):
                    s = jnp.dot(kk, qms[mp], preferred_element_type=F32)
                    st[sc[1]][mp][kr, :] = s
                    smax = jnp.max(s.reshape(KEY_BLOCK // 8, 8, TQ_UNIT), axis=0)
                    mrun[mp] = smax if mrun[mp] is None else jnp.maximum(mrun[mp], smax)
            if fin is not None and step >= lead:
                kr = slice((step - lead) * KEY_BLOCK, (step - lead + 1) * KEY_BLOCK)
                vt = value_block(fin_head, step - lead)
                for mp in range(2):
                    p = jnp.exp2(st[fin[1]][mp][kr, :] - ms[mp]).astype(BF16)
                    d = jnp.dot(vt, p, preferred_element_type=F32)
                    accs[mp] = d if accs[mp] is None else accs[mp] + d
        if sc is not None:
            for mp in range(2):
                m_buf[sc[1], mp] = jnp.max(mrun[mp], axis=0, keepdims=True)
        if fin is not None:
            if defer_out:
                for mp in range(2):
                    acc_buf[mp] = accs[mp]
            else:
                write_out(fin[0], accs[0], accs[1])

    def write_out(u, o1, o2):
        head, rows = head_rows(u)
        out_gain = (1.0 - lambda_init) * ag_ref[head]
        o_ref[head, rows, :] = _combine_maps(
            o1[0:VAL_DIM, :], o1[VAL_DIM:VAL_DIM + 1, :],
            o2[0:VAL_DIM, :], o2[VAL_DIM:VAL_DIM + 1, :], lam, out_gain)

    stage(None, (0, 0))
    stage((0, 0), (1, 1), defer_out=True)

    def pair(i):
        u = 2 * i
        write_out(u - 2, acc_buf[0], acc_buf[1])
        stage((u - 1, 1), (u, 0))
        stage((u, 0), (u + 1, 1), defer_out=True)

    def trip(j, carry):
        for p in range(PAIRS_PER_TRIP):
            pair(PAIRS_PER_TRIP * j + 1 + p)
        return carry

    n_pairs = n_units // 2
    n_trips = (n_pairs - 1) // PAIRS_PER_TRIP
    lax.fori_loop(0, n_trips, trip, 0)
    for i in range(n_trips * PAIRS_PER_TRIP + 1, n_pairs):
        pair(i)
    write_out(n_units - 2, acc_buf[0], acc_buf[1])
    stage((n_units - 1, 1), None)


def _attn_cache(lams, q, k, vt, kct, vc, att_g, weights, *, layer, n_batch, seq_len, lambda_init):
    past = vc.shape[3]
    n_keys = seq_len + past
    assert (seq_len // TQ_UNIT) % 2 == 0 and seq_len // TQ_UNIT >= 4
    assert seq_len % KEY_BLOCK == 0 and past % KEY_BLOCK == 0
    head_spec = pl.BlockSpec((N_HEADS, seq_len, LANES), lambda b: (0, b, 0))
    head_t_spec = pl.BlockSpec((N_HEADS, LANES, seq_len), lambda b: (0, 0, b))
    in_specs = [_const_spec((1, HEAD_DIM))] * 4 + [
        head_t_spec,
        head_spec,
        head_t_spec,
        pl.BlockSpec((1, 1, N_HEADS, 2, HEAD_DIM, past), lambda b: (b, layer, 0, 0, 0, 0)),
        pl.BlockSpec((1, 1, N_HEADS, past, VAL_DIM), lambda b: (b, layer, 0, 0, 0)),
        _const_spec((N_HEADS, 1, LANES)),
    ]
    w_specs = []
    for w in weights:
        slab = w.shape[0] // n_batch
        assert w.shape[0] % n_batch == 0 and slab % 16 == 0
        w_specs.append(pl.BlockSpec((slab, w.shape[1]), lambda b: (b, 0)))
    return pl.pallas_call(
        functools.partial(_attn_cache_kernel, lambda_init=lambda_init, n_new=seq_len,
                          n_weights=len(weights)),
        grid=(n_batch,),
        in_specs=in_specs + w_specs,
        out_specs=[head_spec] + w_specs,
        out_shape=[jax.ShapeDtypeStruct((N_HEADS, n_batch * seq_len, LANES), BF16)]
                  + [jax.ShapeDtypeStruct(w.shape, BF16) for w in weights],
        scratch_shapes=([pltpu.VMEM((N_HEADS, past, LANES), BF16),
                         pltpu.VMEM((N_HEADS, VAL_DIM, past), BF16),
                         pltpu.VMEM((2, 2, 1, TQ_UNIT), F32),
                         pltpu.VMEM((2, VAL_DIM + ONES_ROWS, TQ_UNIT), F32)]
                        + [pltpu.VMEM((n_keys, TQ_UNIT), F32)] * 4),
        compiler_params=_params(1),
        name="attn_cache",
    )(*lams, q, k, vt, kct, vc, att_g, *weights)


def _ffn_kernel(xp_ref, attp_ref, mlpp_ref, xs_ref, atts_ref, mlps_ref, mods_ref, g2_ref,
                wo_ref, wfi_ref, wfo_ref, op_ref, os_ref, *, n_ctx_blk, dec_row_fn):
    i = pl.program_id(0)

    @pl.when(i < n_ctx_blk)
    def _():
        _ffn_block(xp_ref, attp_ref, mlpp_ref, mods_ref, g2_ref, wo_ref, wfi_ref, wfo_ref, op_ref,
                   CTX_ROW)

    @pl.when(i >= n_ctx_blk)
    def _():
        _ffn_block(xs_ref, atts_ref, mlps_ref, mods_ref, g2_ref, wo_ref, wfi_ref, wfo_ref, os_ref,
                   dec_row_fn(i - n_ctx_blk))


def _ffn_block(x_ref, att_ref, mlp_ref, mods_ref, g2_ref, wo_ref, wfi_ref, wfo_ref, o_ref, row):
    gate1, shift2, scale2, gate2 = (_mod(mods_ref, row, k) for k in (2, 3, 4, 5))
    tm = x_ref.shape[0]
    halves = [slice(i * (tm // 2), (i + 1) * (tm // 2)) for i in range(2)]
    x1s, xbs = [], []
    for r in halves:
        att = jnp.concatenate([att_ref[h, r, :] for h in range(N_HEADS)], axis=1)
        y = (jnp.dot(att, wo_ref[0:ATT_WIDTH, :], preferred_element_type=F32)
             + jnp.dot(mlp_ref[r, :], wo_ref[ATT_WIDTH:, :], preferred_element_type=F32))
        x1 = x_ref[r, :] + gate1 * y
        xn = x1 * _rms_scale(x1) * g2_ref[...]
        x1s.append(x1)
        xbs.append((xn * (1.0 + scale2) + shift2).astype(BF16))
    accs = [None, None]
    for c0, cw in FF_CHUNKS:
        pre = []
        for xb in xbs:
            gte = jnp.dot(xb, wfi_ref[:, c0:c0 + cw], preferred_element_type=F32)
            up = jnp.dot(xb, wfi_ref[:, D_FF + c0:D_FF + c0 + cw], preferred_element_type=F32)
            pre.append((gte, up))
        for i, (gte, up) in enumerate(pre):
            act = (gte * jax.nn.sigmoid(gte) * up).astype(BF16)
            part = jnp.dot(act, wfo_ref[c0:c0 + cw, :], preferred_element_type=F32)
            accs[i] = part if accs[i] is None else accs[i] + part
    for r, x1, acc in zip(halves, x1s, accs):
        o_ref[r, :] = x1 + gate2 * acc


def _ffn(ctx_inputs, dec_inputs, mods, g2, w_out, w_ffn_in, w_ffn_out, *, dec_row_fn):
    tm = TM_FFN
    n_p = ctx_inputs[0].shape[0] // tm
    n_s = dec_inputs[0].shape[0] // tm
    ctx_blk = lambda i: jnp.minimum(i, n_p - 1)
    dec_blk = lambda i: jnp.maximum(i - n_p, 0)

    def token_specs(blk):
        return [pl.BlockSpec((tm, D_MODEL), lambda i: (blk(i), 0)),
                pl.BlockSpec((N_HEADS, tm, LANES), lambda i: (0, blk(i), 0)),
                pl.BlockSpec((tm, MLP_WIDTH), lambda i: (blk(i), 0))]

    return pl.pallas_call(
        functools.partial(_ffn_kernel, n_ctx_blk=n_p, dec_row_fn=dec_row_fn),
        grid=(n_p + n_s,),
        in_specs=token_specs(ctx_blk) + token_specs(dec_blk) + [
            _const_spec((MODS_ROWS, 6 * D_MODEL)),
            _const_spec((1, D_MODEL)),
            _const_spec((D_MODEL, D_MODEL)),
            _const_spec((D_MODEL, 2 * D_FF)),
            _const_spec((D_FF, D_MODEL)),
        ],
        out_specs=[pl.BlockSpec((tm, D_MODEL), lambda i: (ctx_blk(i), 0)),
                   pl.BlockSpec((tm, D_MODEL), lambda i: (dec_blk(i), 0))],
        out_shape=[jax.ShapeDtypeStruct(ctx_inputs[0].shape, F32),
                   jax.ShapeDtypeStruct(dec_inputs[0].shape, F32)],
        compiler_params=_params(1),
        name="ffn",
    )(*ctx_inputs, *dec_inputs, mods, g2, w_out, w_ffn_in, w_ffn_out)


def _rope_tables(n):
    pos = np.arange(n)
    row = (pos // GRID_W).astype(np.float32)
    col = (pos % GRID_W).astype(np.float32)
    inv = (ROPE_THETA ** (-np.arange(0, ROPE_AXIS_DIM, 2, dtype=np.float32) / ROPE_AXIS_DIM)
           ).astype(np.float32)
    ang_r = row[:, None] * inv[None, :]
    ang_c = col[:, None] * inv[None, :]
    cos64 = np.concatenate([np.cos(ang_r)] * 2 + [np.cos(ang_c)] * 2, axis=1)
    sin64 = np.concatenate([-np.sin(ang_r), np.sin(ang_r), -np.sin(ang_c), np.sin(ang_c)], axis=1)
    return (np.ascontiguousarray(np.tile(cos64, (1, 2)).T, np.float32),
            np.ascontiguousarray(np.tile(sin64, (1, 2)).T, np.float32))


def kernel(x_prompt, x_sample, cache_k_ctx, cache_v_ctx, c, c_ctx, norm1_g, norm2_g, w_ada, b_ada, w_in, q_norm_g, k_norm_g, lambda_q1, lambda_k1, lambda_q2, lambda_k2, att_out_g, sgu_norm_g, sgu_w, sgu_b, mlp_out_g, w_out, w_ffn_in, w_ffn_out):
    n_ctx, ctx_len, _ = x_prompt.shape
    n_dec, dec_len, _ = x_sample.shape
    depth = norm1_g.shape[0]

    rope_tabs = _rope_tables(dec_len)
    q_scale = LOG2E / math.sqrt(HEAD_DIM)

    xp = x_prompt.reshape(n_ctx * ctx_len, D_MODEL)
    xs = x_sample.reshape(n_dec * dec_len, D_MODEL)
    cache_kt = jnp.swapaxes(cache_k_ctx, -1, -2)
    k_states, v_states = [], []
    ctx_row = lambda i: CTX_ROW
    dec_row_proj = lambda i: i // (dec_len // TM_PROJ)
    dec_row_ffn = lambda i: i // (dec_len // TM_FFN)

    for l in range(depth):
        lambda_init = 0.8 - 0.6 * math.exp(-0.3 * l)
        mods3, w_in_b, sguw = _mods(c, c_ctx[None, :], w_ada[l], b_ada[l][None, :],
                                    (w_in[l], sgu_w[l].reshape(N_GROUPS * CHUNK, CHUNK)))
        sguw = sguw.reshape(N_GROUPS, CHUNK, CHUNK)
        g1 = norm1_g[l][None, :]
        g2 = norm2_g[l][None, :]
        qg = jnp.broadcast_to(jnp.tile(q_norm_g[l] * q_scale, 2)[:, None], (LANES, LANES))
        kg = jnp.broadcast_to(jnp.tile(k_norm_g[l], 2)[:, None], (LANES, LANES))
        sgun = sgu_norm_g[l][None, :]
        sgub = jnp.broadcast_to(sgu_b[l][:, :, None], (N_GROUPS, CHUNK, GROUP_DIM))
        mlpg = mlp_out_g[l][None, :]
        att_g = att_out_g[l].reshape(N_HEADS, 1, VAL_DIM)
        lams = (lambda_q1[l][None, :], lambda_k1[l][None, :],
                lambda_q2[l][None, :], lambda_k2[l][None, :])

        q, k, vt, mlp_s = _proj(
            xs, mods3, g1, w_in_b, qg, kg, rope_tabs, sgun, sguw, sgub, mlpg,
            seq_len=dec_len, mods_row_fn=dec_row_proj)
        att_s, w_out_b, w_fi_b, w_fo_b = _attn_cache(
            lams, q, k, vt, cache_kt, cache_v_ctx, att_g, (w_out[l], w_ffn_in[l], w_ffn_out[l]),
            layer=l, n_batch=n_dec, seq_len=dec_len, lambda_init=lambda_init)

        q, k, vt, mlp, k_c, v_c = _proj(
            xp, mods3, g1, w_in_b, qg, kg, None, sgun, sguw, sgub, mlpg,
            seq_len=ctx_len, mods_row_fn=ctx_row)
        att = _attn_ctx(lams, q, k, vt, att_g, seq_len=ctx_len, lambda_init=lambda_init)
        k_states.append(k_c)
        v_states.append(v_c)

        xp, xs = _ffn((xp, att, mlp), (xs, att_s, mlp_s), mods3, g2, w_out_b, w_fi_b, w_fo_b,
                      dec_row_fn=dec_row_ffn)

    state_k = jnp.swapaxes(jnp.concatenate(k_states, axis=1), -1, -2)
    state_v = jnp.concatenate(v_states, axis=1)
    return (xp.reshape(n_ctx, ctx_len, D_MODEL), xs.reshape(n_dec, dec_len, D_MODEL),
            state_k, state_v)
```

```python
import functools
import math

import jax
import jax.numpy as jnp
import numpy as np
from jax import lax
from jax.experimental import pallas as pl
from jax.experimental.pallas import tpu as pltpu

D_MODEL = 1024
ATT_WIDTH = 512
N_HEADS = 4
HEAD_DIM = 64
VAL_DIM = 128
MLP_WIDTH = 512
N_GROUPS = 4
GROUP_DIM = 128
CHUNK = 128
D_FF = 2816
IN_WIDTH = 2560
GRID_W = 64
ROPE_THETA = 10000.0
ROPE_AXIS_DIM = 32
EPS = 1e-6
LOG2E = 1.4426950408889634
LANES = 128

F32 = jnp.float32
BF16 = jnp.bfloat16

VMEM_LIMIT_BYTES = 56 * 1024 * 1024
MODS_ROWS = 16
CTX_ROW = 8

TM_PROJ = 1024
TM_FFN = 512
TM_ATTN_CTX = 2048
CTX_AHEAD = 3
FF_CHUNKS = ((0, 1024), (1024, 1024), (2048, 768))

ONES_ROWS = 16
TQ_UNIT = 256
KEY_BLOCK = 256
SCORE_LEAD = 2
PAIRS_PER_TRIP = 5
ST_SKEW_ROWS = 8


def _const_spec(shape):
    zeros = (0,) * len(shape)
    return pl.BlockSpec(shape, lambda *_: zeros, pipeline_mode=pl.Buffered(1))


def _params(n_grid):
    return pltpu.CompilerParams(
        dimension_semantics=("arbitrary",) * n_grid,
        vmem_limit_bytes=VMEM_LIMIT_BYTES,
    )


def _rms_scale(x):
    return lax.rsqrt(jnp.mean(x * x, axis=-1, keepdims=True) + EPS)


def _mod(mods_ref, row, k):
    return mods_ref[pl.ds(row, 1), k * D_MODEL:(k + 1) * D_MODEL]


def _mods_kernel(c_ref, cctx_ref, w_ref, b_ref, *rest):
    n_weights = (len(rest) - 1) // 2
    o_ref = rest[n_weights]
    cnd = jnp.concatenate(
        [c_ref[...], jnp.broadcast_to(cctx_ref[...], (MODS_ROWS - CTX_ROW, D_MODEL))], axis=0)
    act = (cnd * jax.nn.sigmoid(cnd)).astype(BF16)
    o_ref[...] = jnp.dot(act, w_ref[...].astype(BF16), preferred_element_type=F32) + b_ref[...]
    for w32_ref, w16_ref in zip(rest[:n_weights], rest[n_weights + 1:]):
        w16_ref[...] = w32_ref[...].astype(BF16)


def _mods(c, c_ctx, w_ada, b_ada, weights):
    tn = 1536
    n_out = w_ada.shape[1]
    n_steps = n_out // tn
    assert c.shape[0] == CTX_ROW
    w_specs = []
    for w in weights:
        slab = w.shape[0] // n_steps
        assert w.shape[0] % n_steps == 0 and slab % 16 == 0
        w_specs.append(pl.BlockSpec((slab, w.shape[1]), lambda j: (j, 0)))
    return pl.pallas_call(
        _mods_kernel,
        grid=(n_steps,),
        in_specs=[
            _const_spec((CTX_ROW, D_MODEL)),
            _const_spec((1, D_MODEL)),
            pl.BlockSpec((D_MODEL, tn), lambda j: (0, j)),
            pl.BlockSpec((1, tn), lambda j: (0, j)),
        ] + w_specs,
        out_specs=[pl.BlockSpec((MODS_ROWS, tn), lambda j: (0, j))] + w_specs,
        out_shape=[jax.ShapeDtypeStruct((MODS_ROWS, n_out), F32)]
                  + [jax.ShapeDtypeStruct(w.shape, BF16) for w in weights],
        compiler_params=_params(1),
        name="mods",
    )(c, c_ctx, w_ada, b_ada, *weights)


def _proj_kernel(*refs, rope, seq_len, tm, mods_row_fn):
    it = iter(refs)
    x_ref, mods_ref, g1_ref, w_in_ref, qg_ref, kg_ref = (next(it) for _ in range(6))
    if rope:
        cos_ref, sin_ref = (next(it) for _ in range(2))
    sgun_ref, sguw_ref, sgub_ref, mlpg_ref = (next(it) for _ in range(4))
    qt_ref, k_ref, vt_ref, mlp_ref = (next(it) for _ in range(4))
    if not rope:
        kst_ref, vst_ref = (next(it) for _ in range(2))
    gate_ref = next(it)

    x = x_ref[...]
    xn = x * _rms_scale(x) * g1_ref[...]
    row = mods_row_fn(pl.program_id(0))
    xm = xn * (1.0 + _mod(mods_ref, row, 1)) + _mod(mods_ref, row, 0)
    xb = xm.astype(BF16)

    def section(lo, hi):
        return jnp.dot(xb, w_in_ref[:, lo:hi], preferred_element_type=F32)

    def head_t(sec, h, g_ref):
        t = sec[:, h * LANES:(h + 1) * LANES].T
        maps = []
        for mp in range(2):
            tmap = t[mp * HEAD_DIM:(mp + 1) * HEAD_DIM, :]
            maps.append(tmap * lax.rsqrt(jnp.mean(tmap * tmap, axis=0, keepdims=True) + EPS))
        gain = jnp.concatenate([g_ref[...]] * (tm // LANES), axis=1)
        tn = jnp.concatenate(maps, axis=0) * gain
        if rope:
            half = ROPE_AXIS_DIM // 2
            swapped = jnp.concatenate(
                [tn[r0 + off:r0 + off + half, :]
                 for r0 in range(0, LANES, ROPE_AXIS_DIM) for off in (half, 0)], axis=0)
            tn = tn * cos_ref[...] + swapped * sin_ref[...]
        return tn

    hu = section(3 * ATT_WIDTH, 3 * ATT_WIDTH + MLP_WIDTH)
    hg = section(3 * ATT_WIDTH + MLP_WIDTH, IN_WIDTH)
    for g in range(N_GROUPS):
        cols = slice(g * GROUP_DIM, (g + 1) * GROUP_DIM)
        gg = hg[:, cols]
        gn = (gg * _rms_scale(gg) * sgun_ref[:, cols]).astype(BF16)
        ug = hu[:, cols]
        wg = sguw_ref[g]
        bg = sgub_ref[g]
        for n in range(tm // CHUNK):
            rows = slice(n * CHUNK, (n + 1) * CHUNK)
            sp = jnp.dot(wg, gn[rows, :], preferred_element_type=F32) + bg
            gate_ref[rows, cols] = ug[rows, :] * sp
    o = gate_ref[...]
    mlp_ref[...] = (o * _rms_scale(o) * mlpg_ref[...]).astype(BF16)

    hq = section(0, ATT_WIDTH)
    hk = section(ATT_WIDTH, 2 * ATT_WIDTH)
    for h in range(N_HEADS):
        qt_ref[h] = head_t(hq, h, qg_ref).astype(BF16)
        kt = head_t(hk, h, kg_ref)
        k_ref[h] = kt.T.astype(BF16)
        if not rope:
            for s in range(tm // seq_len):
                for i in range(2):
                    kst_ref[s, 0, h, i, :, :] = kt[i * HEAD_DIM:(i + 1) * HEAD_DIM,
                                                   s * seq_len:(s + 1) * seq_len]

    hv = section(2 * ATT_WIDTH, 3 * ATT_WIDTH)
    hvt = hv.T.astype(BF16)
    for h in range(N_HEADS):
        vt_ref[h] = hvt[h * VAL_DIM:(h + 1) * VAL_DIM, :]
        if not rope:
            for s in range(tm // seq_len):
                vst_ref[s, 0, h, :, :] = hv[s * seq_len:(s + 1) * seq_len,
                                            h * VAL_DIM:(h + 1) * VAL_DIM]


def _proj(x2d, mods3, g1, w_in, qg, kg, rope_tabs, sgun, sguw, sgub, mlpg,
          *, seq_len, mods_row_fn):
    n_tok = x2d.shape[0]
    tm = TM_PROJ
    rope = rope_tabs is not None
    blocks_per_seq = seq_len // tm if rope else None

    in_specs = [
        pl.BlockSpec((tm, D_MODEL), lambda i: (i, 0)),
        _const_spec((MODS_ROWS, 6 * D_MODEL)),
        _const_spec((1, D_MODEL)),
        _const_spec((D_MODEL, IN_WIDTH)),
        _const_spec((LANES, LANES)),
        _const_spec((LANES, LANES)),
    ]
    args = [x2d, mods3, g1, w_in, qg, kg]
    if rope:
        tab_spec = pl.BlockSpec((LANES, tm), lambda i: (0, i % blocks_per_seq))
        in_specs += [tab_spec] * 2
        args += list(rope_tabs)
    in_specs += [
        _const_spec((1, MLP_WIDTH)),
        _const_spec((N_GROUPS, CHUNK, CHUNK)),
        _const_spec((N_GROUPS, CHUNK, GROUP_DIM)),
        _const_spec((1, MLP_WIDTH)),
    ]
    args += [sgun, sguw, sgub, mlpg]

    head_spec = pl.BlockSpec((N_HEADS, tm, LANES), lambda i: (0, i, 0))
    head_t_spec = pl.BlockSpec((N_HEADS, LANES, tm), lambda i: (0, 0, i))
    head_shape = jax.ShapeDtypeStruct((N_HEADS, n_tok, LANES), BF16)
    head_t_shape = jax.ShapeDtypeStruct((N_HEADS, LANES, n_tok), BF16)
    out_specs = [head_t_spec, head_spec, head_t_spec,
                 pl.BlockSpec((tm, MLP_WIDTH), lambda i: (i, 0))]
    out_shape = [head_t_shape, head_shape, head_t_shape,
                 jax.ShapeDtypeStruct((n_tok, MLP_WIDTH), BF16)]
    if not rope:
        n_seq = n_tok // seq_len
        spb = tm // seq_len
        out_specs += [
            pl.BlockSpec((spb, 1, N_HEADS, 2, HEAD_DIM, seq_len), lambda i: (i, 0, 0, 0, 0, 0)),
            pl.BlockSpec((spb, 1, N_HEADS, seq_len, VAL_DIM), lambda i: (i, 0, 0, 0, 0)),
        ]
        out_shape += [
            jax.ShapeDtypeStruct((n_seq, 1, N_HEADS, 2, HEAD_DIM, seq_len), F32),
            jax.ShapeDtypeStruct((n_seq, 1, N_HEADS, seq_len, VAL_DIM), F32),
        ]

    return pl.pallas_call(
        functools.partial(_proj_kernel, rope=rope, seq_len=seq_len, tm=tm, mods_row_fn=mods_row_fn),
        grid=(n_tok // tm,),
        in_specs=in_specs,
        out_specs=out_specs,
        out_shape=out_shape,
        scratch_shapes=[pltpu.VMEM((tm, MLP_WIDTH), F32)],
        compiler_params=_params(1),
        name="proj_rope" if rope else "proj_ctx",
    )(*args)


def _lambda_full(lq1, lk1, lq2, lk2, lambda_init):
    return (jnp.exp(jnp.sum(lq1[...] * lk1[...], keepdims=True))
            - jnp.exp(jnp.sum(lq2[...] * lk2[...], keepdims=True))
            + lambda_init)


def _map_queries(qt):
    row = lax.broadcasted_iota(jnp.int32, qt.shape, 0)
    zero = jnp.zeros_like(qt)
    return (jnp.where(row < HEAD_DIM, qt, zero), jnp.where(row >= HEAD_DIM, qt, zero))


def _combine_maps(o1, d1, o2, d2, lam, out_gain):
    ot = o1 * (1.0 / d1) - o2 * (lam / d2)
    ot = ot * lax.rsqrt(jnp.mean(ot * ot, axis=0, keepdims=True) + EPS)
    return (ot.T * out_gain).astype(BF16)


def _attn_ctx_kernel(lq1, lk1, lq2, lk2, qt_ref, k_ref, vt_ref, ag_ref, o_ref, st_buf,
                     *, lambda_init, seq_len):
    lam = _lambda_full(lq1, lk1, lq2, lk2, lambda_init)
    units = [(slice(s * seq_len, (s + 1) * seq_len), h)
             for s in range(k_ref.shape[1] // seq_len) for h in range(N_HEADS)]

    def scores(u):
        rows, h = units[u]
        kk = k_ref[h, rows, :]
        maxes = []
        for mp, qm in enumerate(_map_queries(qt_ref[h, :, rows])):
            st = jnp.dot(kk, qm, preferred_element_type=F32)
            st_buf[u % CTX_AHEAD, mp] = st
            maxes.append(jnp.max(st, axis=0, keepdims=True))
        return maxes

    def finish(u, maxes):
        rows, h = units[u]
        vt = vt_ref[h, :, rows]
        outs, dens = [], []
        for mp in range(2):
            e = jnp.exp2(st_buf[u % CTX_AHEAD, mp] - maxes[mp])
            dens.append(jnp.sum(e, axis=0, keepdims=True))
            outs.append(jnp.dot(vt, e.astype(BF16), preferred_element_type=F32))
        out_gain = (1.0 - lambda_init) * ag_ref[h]
        o_ref[h, rows, :] = _combine_maps(outs[0], dens[0], outs[1], dens[1], lam, out_gain)

    pending = [scores(u) for u in range(CTX_AHEAD - 1)]
    for u in range(len(units)):
        if u + CTX_AHEAD - 1 < len(units):
            pending.append(scores(u + CTX_AHEAD - 1))
        finish(u, pending.pop(0))


def _attn_ctx(lams, q, k, vt, att_g, *, seq_len, lambda_init):
    n_tok = k.shape[1]
    tm = TM_ATTN_CTX
    head_spec = pl.BlockSpec((N_HEADS, tm, LANES), lambda i: (0, i, 0))
    head_t_spec = pl.BlockSpec((N_HEADS, LANES, tm), lambda i: (0, 0, i))
    return pl.pallas_call(
        functools.partial(_attn_ctx_kernel, lambda_init=lambda_init, seq_len=seq_len),
        grid=(n_tok // tm,),
        in_specs=[_const_spec((1, HEAD_DIM))] * 4 + [
            head_t_spec,
            head_spec,
            head_t_spec,
            _const_spec((N_HEADS, 1, LANES)),
        ],
        out_specs=head_spec,
        out_shape=jax.ShapeDtypeStruct((N_HEADS, n_tok, LANES), BF16),
        scratch_shapes=[pltpu.VMEM((CTX_AHEAD, 2, seq_len, seq_len), F32)],
        compiler_params=_params(1),
        name="attn_ctx",
    )(*lams, q, k, vt, att_g)


def _attn_cache_kernel(lq1, lk1, lq2, lk2, qt_ref, k_ref, vt_ref, kct_ref, vc_ref, ag_ref,
                       *rest, lambda_init, n_new, n_weights):
    for w_ref, wb_ref in zip(rest[:n_weights], rest[n_weights + 1:2 * n_weights + 1]):
        wb_ref[...] = w_ref[...].astype(BF16)
    o_ref = rest[n_weights]
    k_all, vt_all, m_buf, acc_buf, *bufs = rest[2 * n_weights + 1:]
    st = (bufs[0:2], bufs[2:4])
    n_chunks = n_new // TQ_UNIT
    n_units = N_HEADS * n_chunks
    n_keys = k_all.shape[1]

    past = n_keys - n_new
    for h in range(N_HEADS):
        k_all[h, 0:n_new, :] = k_ref[h]
        k_all[h, n_new:, :] = kct_ref[0, 0, h].reshape(2 * HEAD_DIM, past).T.astype(BF16)
        vt_all[h, 0:VAL_DIM, 0:n_new] = vt_ref[h]
        vt_all[h, 0:VAL_DIM, n_new:] = vc_ref[0, 0, h].T.astype(BF16)
        vt_all[h, VAL_DIM:, :] = jnp.ones((ONES_ROWS, n_keys), BF16)

    lam = _lambda_full(lq1, lk1, lq2, lk2, lambda_init)

    def skew(slot, kr):
        return slice(kr.start + slot * ST_SKEW_ROWS, kr.stop + slot * ST_SKEW_ROWS)

    def head_rows(u):
        c = u % n_chunks
        return u // n_chunks, pl.ds(pl.multiple_of(c * TQ_UNIT, TQ_UNIT), TQ_UNIT)

    def stage(fin, sc, defer_out=False):
        if sc is not None:
            sc_head, sc_rows = head_rows(sc[0])
            qms = _map_queries(qt_ref[sc_head, :, sc_rows])
            mrun = [None, None]
        if fin is not None:
            fin_head = fin[0] // n_chunks
            ms = [m_buf[fin[1], mp] for mp in range(2)]
            accs = [None, None]
        n_kb = n_keys // KEY_BLOCK
        lead = SCORE_LEAD if (sc is not None and fin is not None) else 0
        for step in range(n_kb + lead):
            if sc is not None and step < n_kb:
                kr = slice(step * KEY_BLOCK, (step + 1) * KEY_BLOCK)
                kk = k_all[sc_head, kr, :]
                for mp in range(2):
                    s = jnp.dot(kk, qms[mp], preferred_element_type=F32)
                    st[sc[1]][mp][skew(sc[1], kr), :] = s
                    smax = jnp.max(s.reshape(KEY_BLOCK // 8, 8, TQ_UNIT), axis=0)
                    mrun[mp] = smax if mrun[mp] is None else jnp.maximum(mrun[mp], smax)
            if fin is not None and step >= lead:
                kr = slice((step - lead) * KEY_BLOCK, (step - lead + 1) * KEY_BLOCK)
                vt = vt_all[fin_head, :, kr]
                for mp in range(2):
                    p = jnp.exp2(st[fin[1]][mp][skew(fin[1], kr), :] - ms[mp]).astype(BF16)
                    d = jnp.dot(vt, p, preferred_element_type=F32)
                    accs[mp] = d if accs[mp] is None else accs[mp] + d
        if sc is not None:
            for mp in range(2):
                m_buf[sc[1], mp] = jnp.max(mrun[mp], axis=0, keepdims=True)
        if fin is not None:
            if defer_out:
                for mp in range(2):
                    acc_buf[mp] = accs[mp]
            else:
                write_out(fin[0], accs[0], accs[1])

    def write_out(u, o1, o2):
        head, rows = head_rows(u)
        out_gain = (1.0 - lambda_init) * ag_ref[head]
        o_ref[head, rows, :] = _combine_maps(
            o1[0:VAL_DIM, :], o1[VAL_DIM:VAL_DIM + 1, :],
            o2[0:VAL_DIM, :], o2[VAL_DIM:VAL_DIM + 1, :], lam, out_gain)

    stage(None, (0, 0))
    stage((0, 0), (1, 1), defer_out=True)

    def pair(i):
        u = 2 * i
        write_out(u - 2, acc_buf[0], acc_buf[1])
        stage((u - 1, 1), (u, 0))
        stage((u, 0), (u + 1, 1), defer_out=True)

    def trip(j, carry):
        for p in range(PAIRS_PER_TRIP):
            pair(PAIRS_PER_TRIP * j + 1 + p)
        return carry

    n_pairs = n_units // 2
    n_trips = (n_pairs - 1) // PAIRS_PER_TRIP
    lax.fori_loop(0, n_trips, trip, 0)
    for i in range(n_trips * PAIRS_PER_TRIP + 1, n_pairs):
        pair(i)
    write_out(n_units - 2, acc_buf[0], acc_buf[1])
    stage((n_units - 1, 1), None)


def _attn_cache(lams, q, k, vt, kct, vc, att_g, weights, *, layer, n_batch, seq_len, lambda_init):
    past = vc.shape[3]
    n_keys = seq_len + past
    assert (seq_len // TQ_UNIT) % 2 == 0 and seq_len // TQ_UNIT >= 4
    assert n_keys % KEY_BLOCK == 0
    head_spec = pl.BlockSpec((N_HEADS, seq_len, LANES), lambda b: (0, b, 0))
    head_t_spec = pl.BlockSpec((N_HEADS, LANES, seq_len), lambda b: (0, 0, b))
    in_specs = [_const_spec((1, HEAD_DIM))] * 4 + [
        head_t_spec,
        head_spec,
        head_t_spec,
        pl.BlockSpec((1, 1, N_HEADS, 2, HEAD_DIM, past), lambda b: (b, layer, 0, 0, 0, 0)),
        pl.BlockSpec((1, 1, N_HEADS, past, VAL_DIM), lambda b: (b, layer, 0, 0, 0)),
        _const_spec((N_HEADS, 1, LANES)),
    ]
    w_specs = []
    for w in weights:
        slab = w.shape[0] // n_batch
        assert w.shape[0] % n_batch == 0 and slab % 16 == 0
        w_specs.append(pl.BlockSpec((slab, w.shape[1]), lambda b: (b, 0)))
    return pl.pallas_call(
        functools.partial(_attn_cache_kernel, lambda_init=lambda_init, n_new=seq_len,
                          n_weights=len(weights)),
        grid=(n_batch,),
        in_specs=in_specs + w_specs,
        out_specs=[head_spec] + w_specs,
        out_shape=[jax.ShapeDtypeStruct((N_HEADS, n_batch * seq_len, LANES), BF16)]
                  + [jax.ShapeDtypeStruct(w.shape, BF16) for w in weights],
        scratch_shapes=([pltpu.VMEM((N_HEADS, n_keys, LANES), BF16),
                         pltpu.VMEM((N_HEADS, VAL_DIM + ONES_ROWS, n_keys), BF16),
                         pltpu.VMEM((2, 2, 1, TQ_UNIT), F32),
                         pltpu.VMEM((2, VAL_DIM + ONES_ROWS, TQ_UNIT), F32)]
                        + [pltpu.VMEM((n_keys, TQ_UNIT), F32)] * 2
                        + [pltpu.VMEM((n_keys + ST_SKEW_ROWS, TQ_UNIT), F32)] * 2),
        compiler_params=_params(1),
        name="attn_cache",
    )(*lams, q, k, vt, kct, vc, att_g, *weights)


def _ffn_kernel(xp_ref, attp_ref, mlpp_ref, xs_ref, atts_ref, mlps_ref, mods_ref, g2_ref,
                wo_ref, wfi_ref, wfo_ref, op_ref, os_ref, *, n_ctx_blk, dec_row_fn):
    i = pl.program_id(0)

    @pl.when(i < n_ctx_blk)
    def _():
        _ffn_block(xp_ref, attp_ref, mlpp_ref, mods_ref, g2_ref, wo_ref, wfi_ref, wfo_ref, op_ref,
                   CTX_ROW)

    @pl.when(i >= n_ctx_blk)
    def _():
        _ffn_block(xs_ref, atts_ref, mlps_ref, mods_ref, g2_ref, wo_ref, wfi_ref, wfo_ref, os_ref,
                   dec_row_fn(i - n_ctx_blk))


def _ffn_block(x_ref, att_ref, mlp_ref, mods_ref, g2_ref, wo_ref, wfi_ref, wfo_ref, o_ref, row):
    gate1, shift2, scale2, gate2 = (_mod(mods_ref, row, k) for k in (2, 3, 4, 5))
    tm = x_ref.shape[0]
    halves = [slice(i * (tm // 2), (i + 1) * (tm // 2)) for i in range(2)]
    x1s, xbs = [], []
    for r in halves:
        att = jnp.concatenate([att_ref[h, r, :] for h in range(N_HEADS)], axis=1)
        y = (jnp.dot(att, wo_ref[0:ATT_WIDTH, :], preferred_element_type=F32)
             + jnp.dot(mlp_ref[r, :], wo_ref[ATT_WIDTH:, :], preferred_element_type=F32))
        x1 = x_ref[r, :] + gate1 * y
        xn = x1 * _rms_scale(x1) * g2_ref[...]
        x1s.append(x1)
        xbs.append((xn * (1.0 + scale2) + shift2).astype(BF16))
    accs = [None, None]
    for c0, cw in FF_CHUNKS:
        pre = []
        for xb in xbs:
            gte = jnp.dot(xb, wfi_ref[:, c0:c0 + cw], preferred_element_type=F32)
            up = jnp.dot(xb, wfi_ref[:, D_FF + c0:D_FF + c0 + cw], preferred_element_type=F32)
            pre.append((gte, up))
        for i, (gte, up) in enumerate(pre):
            act = (gte * jax.nn.sigmoid(gte) * up).astype(BF16)
            part = jnp.dot(act, wfo_ref[c0:c0 + cw, :], preferred_element_type=F32)
            accs[i] = part if accs[i] is None else accs[i] + part
    for r, x1, acc in zip(halves, x1s, accs):
        o_ref[r, :] = x1 + gate2 * acc


def _ffn(ctx_inputs, dec_inputs, mods, g2, w_out, w_ffn_in, w_ffn_out, *, dec_row_fn):
    tm = TM_FFN
    n_p = ctx_inputs[0].shape[0] // tm
    n_s = dec_inputs[0].shape[0] // tm
    ctx_blk = lambda i: jnp.minimum(i, n_p - 1)
    dec_blk = lambda i: jnp.maximum(i - n_p, 0)

    def token_specs(blk):
        return [pl.BlockSpec((tm, D_MODEL), lambda i: (blk(i), 0)),
                pl.BlockSpec((N_HEADS, tm, LANES), lambda i: (0, blk(i), 0)),
                pl.BlockSpec((tm, MLP_WIDTH), lambda i: (blk(i), 0))]

    return pl.pallas_call(
        functools.partial(_ffn_kernel, n_ctx_blk=n_p, dec_row_fn=dec_row_fn),
        grid=(n_p + n_s,),
        in_specs=token_specs(ctx_blk) + token_specs(dec_blk) + [
            _const_spec((MODS_ROWS, 6 * D_MODEL)),
            _const_spec((1, D_MODEL)),
            _const_spec((D_MODEL, D_MODEL)),
            _const_spec((D_MODEL, 2 * D_FF)),
            _const_spec((D_FF, D_MODEL)),
        ],
        out_specs=[pl.BlockSpec((tm, D_MODEL), lambda i: (ctx_blk(i), 0)),
                   pl.BlockSpec((tm, D_MODEL), lambda i: (dec_blk(i), 0))],
        out_shape=[jax.ShapeDtypeStruct(ctx_inputs[0].shape, F32),
                   jax.ShapeDtypeStruct(dec_inputs[0].shape, F32)],
        compiler_params=_params(1),
        name="ffn",
    )(*ctx_inputs, *dec_inputs, mods, g2, w_out, w_ffn_in, w_ffn_out)


def _rope_tables(n):
    pos = np.arange(n)
    row = (pos // GRID_W).astype(np.float32)
    col = (pos % GRID_W).astype(np.float32)
    inv = (ROPE_THETA ** (-np.arange(0, ROPE_AXIS_DIM, 2, dtype=np.float32) / ROPE_AXIS_DIM)
           ).astype(np.float32)
    ang_r = row[:, None] * inv[None, :]
    ang_c = col[:, None] * inv[None, :]
    cos64 = np.concatenate([np.cos(ang_r)] * 2 + [np.cos(ang_c)] * 2, axis=1)
    sin64 = np.concatenate([-np.sin(ang_r), np.sin(ang_r), -np.sin(ang_c), np.sin(ang_c)], axis=1)
    return (np.ascontiguousarray(np.tile(cos64, (1, 2)).T, np.float32),
            np.ascontiguousarray(np.tile(sin64, (1, 2)).T, np.float32))


def kernel(x_prompt, x_sample, cache_k_ctx, cache_v_ctx, c, c_ctx, norm1_g, norm2_g, w_ada, b_ada, w_in, q_norm_g, k_norm_g, lambda_q1, lambda_k1, lambda_q2, lambda_k2, att_out_g, sgu_norm_g, sgu_w, sgu_b, mlp_out_g, w_out, w_ffn_in, w_ffn_out):
    n_ctx, ctx_len, _ = x_prompt.shape
    n_dec, dec_len, _ = x_sample.shape
    depth = norm1_g.shape[0]

    rope_tabs = _rope_tables(dec_len)
    q_scale = LOG2E / math.sqrt(HEAD_DIM)

    xp = x_prompt.reshape(n_ctx * ctx_len, D_MODEL)
    xs = x_sample.reshape(n_dec * dec_len, D_MODEL)
    cache_kt = jnp.swapaxes(cache_k_ctx, -1, -2)
    k_states, v_states = [], []
    ctx_row = lambda i: CTX_ROW
    dec_row_proj = lambda i: i // (dec_len // TM_PROJ)
    dec_row_ffn = lambda i: i // (dec_len // TM_FFN)

    for l in range(depth):
        lambda_init = 0.8 - 0.6 * math.exp(-0.3 * l)
        mods3, w_in_b, sguw = _mods(c, c_ctx[None, :], w_ada[l], b_ada[l][None, :],
                                    (w_in[l], sgu_w[l].reshape(N_GROUPS * CHUNK, CHUNK)))
        sguw = sguw.reshape(N_GROUPS, CHUNK, CHUNK)
        g1 = norm1_g[l][None, :]
        g2 = norm2_g[l][None, :]
        qg = jnp.broadcast_to(jnp.tile(q_norm_g[l] * q_scale, 2)[:, None], (LANES, LANES))
        kg = jnp.broadcast_to(jnp.tile(k_norm_g[l], 2)[:, None], (LANES, LANES))
        sgun = sgu_norm_g[l][None, :]
        sgub = jnp.broadcast_to(sgu_b[l][:, :, None], (N_GROUPS, CHUNK, GROUP_DIM))
        mlpg = mlp_out_g[l][None, :]
        att_g = att_out_g[l].reshape(N_HEADS, 1, VAL_DIM)
        lams = (lambda_q1[l][None, :], lambda_k1[l][None, :],
                lambda_q2[l][None, :], lambda_k2[l][None, :])

        q, k, vt, mlp_s = _proj(
            xs, mods3, g1, w_in_b, qg, kg, rope_tabs, sgun, sguw, sgub, mlpg,
            seq_len=dec_len, mods_row_fn=dec_row_proj)
        att_s, w_out_b, w_fi_b, w_fo_b = _attn_cache(
            lams, q, k, vt, cache_kt, cache_v_ctx, att_g, (w_out[l], w_ffn_in[l], w_ffn_out[l]),
            layer=l, n_batch=n_dec, seq_len=dec_len, lambda_init=lambda_init)

        q, k, vt, mlp, k_c, v_c = _proj(
            xp, mods3, g1, w_in_b, qg, kg, None, sgun, sguw, sgub, mlpg,
            seq_len=ctx_len, mods_row_fn=ctx_row)
        att = _attn_ctx(lams, q, k, vt, att_g, seq_len=ctx_len, lambda_init=lambda_init)
        k_states.append(k_c)
        v_states.append(v_c)

        xp, xs = _ffn((xp, att, mlp), (xs, att_s, mlp_s), mods3, g2, w_out_b, w_fi_b, w_fo_b,
                      dec_row_fn=dec_row_ffn)

    state_k = jnp.swapaxes(jnp.concatenate(k_states, axis=1), -1, -2)
    state_v = jnp.concatenate(v_states, axis=1)
    return (xp.reshape(n_ctx, ctx_len, D_MODEL), xs.reshape(n_dec, dec_len, D_MODEL),
            state_k, state_v)
```

```python
import functools
import math

import jax
import jax.numpy as jnp
import numpy as np
from jax import lax
from jax.experimental import pallas as pl
from jax.experimental.pallas import tpu as pltpu

D_MODEL = 1024
ATT_WIDTH = 512
N_HEADS = 4
HEAD_DIM = 64
VAL_DIM = 128
MLP_WIDTH = 512
N_GROUPS = 4
GROUP_DIM = 128
CHUNK = 128
D_FF = 2816
IN_WIDTH = 2560
GRID_W = 64
ROPE_THETA = 10000.0
ROPE_AXIS_DIM = 32
EPS = 1e-6
LOG2E = 1.4426950408889634
LANES = 128

F32 = jnp.float32
BF16 = jnp.bfloat16

VMEM_LIMIT_BYTES = 56 * 1024 * 1024
MODS_ROWS = 16
CTX_ROW = 8

TM_PROJ = 1024
TM_FFN = 512
TM_ATTN_CTX = 2048
CTX_AHEAD = 3
FF_CHUNKS = ((0, 1024), (1024, 1024), (2048, 768))

ONES_ROWS = 16
TQ_UNIT = 256
KEY_BLOCK = 256
SCORE_LEAD = 2
PAIRS_PER_TRIP = 5

def _const_spec(shape):
    zeros = (0,) * len(shape)
    return pl.BlockSpec(shape, lambda *_: zeros, pipeline_mode=pl.Buffered(1))


def _params(n_grid):
    return pltpu.CompilerParams(
        dimension_semantics=("arbitrary",) * n_grid,
        vmem_limit_bytes=VMEM_LIMIT_BYTES,
    )


def _rms_scale(x):
    return lax.rsqrt(jnp.mean(x * x, axis=-1, keepdims=True) + EPS)


def _mod(mods_ref, row, k):
    return mods_ref[pl.ds(row, 1), k * D_MODEL:(k + 1) * D_MODEL]


def _mods_kernel(c_ref, cctx_ref, w_ref, b_ref, *rest):
    n_weights = (len(rest) - 1) // 2
    o_ref = rest[n_weights]
    cnd = jnp.concatenate(
        [c_ref[...], jnp.broadcast_to(cctx_ref[...], (MODS_ROWS - CTX_ROW, D_MODEL))], axis=0)
    act = (cnd * jax.nn.sigmoid(cnd)).astype(BF16)
    o_ref[...] = jnp.dot(act, w_ref[...].astype(BF16), preferred_element_type=F32) + b_ref[...]
    for w32_ref, w16_ref in zip(rest[:n_weights], rest[n_weights + 1:]):
        w16_ref[...] = w32_ref[...].astype(BF16)


def _mods(c, c_ctx, w_ada, b_ada, weights):
    tn = 1536
    n_out = w_ada.shape[1]
    n_steps = n_out // tn
    assert c.shape[0] == CTX_ROW
    w_specs = []
    for w in weights:
        slab = w.shape[0] // n_steps
        assert w.shape[0] % n_steps == 0 and slab % 16 == 0
        w_specs.append(pl.BlockSpec((slab, w.shape[1]), lambda j: (j, 0)))
    return pl.pallas_call(
        _mods_kernel,
        grid=(n_steps,),
        in_specs=[
            _const_spec((CTX_ROW, D_MODEL)),
            _const_spec((1, D_MODEL)),
            pl.BlockSpec((D_MODEL, tn), lambda j: (0, j)),
            pl.BlockSpec((1, tn), lambda j: (0, j)),
        ] + w_specs,
        out_specs=[pl.BlockSpec((MODS_ROWS, tn), lambda j: (0, j))] + w_specs,
        out_shape=[jax.ShapeDtypeStruct((MODS_ROWS, n_out), F32)]
                  + [jax.ShapeDtypeStruct(w.shape, BF16) for w in weights],
        compiler_params=_params(1),
        name="mods",
    )(c, c_ctx, w_ada, b_ada, *weights)


def _proj_kernel(*refs, rope, seq_len, tm, mods_row_fn):
    it = iter(refs)
    x_ref, mods_ref, g1_ref, w_in_ref, qg_ref, kg_ref = (next(it) for _ in range(6))
    if rope:
        cos_ref, sin_ref = (next(it) for _ in range(2))
    sgun_ref, sguw_ref, sgub_ref, mlpg_ref = (next(it) for _ in range(4))
    qt_ref, k_ref, vt_ref, mlp_ref = (next(it) for _ in range(4))
    if not rope:
        kst_ref, vst_ref = (next(it) for _ in range(2))
    gate_ref = next(it)

    x = x_ref[...]
    xn = x * _rms_scale(x) * g1_ref[...]
    row = mods_row_fn(pl.program_id(0))
    xm = xn * (1.0 + _mod(mods_ref, row, 1)) + _mod(mods_ref, row, 0)
    xb = xm.astype(BF16)

    def section(lo, hi):
        return jnp.dot(xb, w_in_ref[:, lo:hi], preferred_element_type=F32)

    def head_t(sec, h, g_ref):
        t = sec[:, h * LANES:(h + 1) * LANES].T
        maps = []
        for mp in range(2):
            tmap = t[mp * HEAD_DIM:(mp + 1) * HEAD_DIM, :]
            maps.append(tmap * lax.rsqrt(jnp.mean(tmap * tmap, axis=0, keepdims=True) + EPS))
        gain = jnp.concatenate([g_ref[...]] * (tm // LANES), axis=1)
        tn = jnp.concatenate(maps, axis=0) * gain
        if rope:
            half = ROPE_AXIS_DIM // 2
            swapped = jnp.concatenate(
                [tn[r0 + off:r0 + off + half, :]
                 for r0 in range(0, LANES, ROPE_AXIS_DIM) for off in (half, 0)], axis=0)
            tn = tn * cos_ref[...] + swapped * sin_ref[...]
        return tn

    hu = section(3 * ATT_WIDTH, 3 * ATT_WIDTH + MLP_WIDTH)
    hg = section(3 * ATT_WIDTH + MLP_WIDTH, IN_WIDTH)
    for g in range(N_GROUPS):
        cols = slice(g * GROUP_DIM, (g + 1) * GROUP_DIM)
        gg = hg[:, cols]
        gn = (gg * _rms_scale(gg) * sgun_ref[:, cols]).astype(BF16)
        ug = hu[:, cols]
        wg = sguw_ref[g]
        bg = sgub_ref[g]
        for n in range(tm // CHUNK):
            rows = slice(n * CHUNK, (n + 1) * CHUNK)
            sp = jnp.dot(wg, gn[rows, :], preferred_element_type=F32) + bg
            gate_ref[rows, cols] = ug[rows, :] * sp
    o = gate_ref[...]
    mlp_ref[...] = (o * _rms_scale(o) * mlpg_ref[...]).astype(BF16)

    hq = section(0, ATT_WIDTH)
    hk = section(ATT_WIDTH, 2 * ATT_WIDTH)
    for h in range(N_HEADS):
        qt_ref[h] = head_t(hq, h, qg_ref).astype(BF16)
        kt = head_t(hk, h, kg_ref)
        k_ref[h] = kt.T.astype(BF16)
        if not rope:
            for s in range(tm // seq_len):
                for i in range(2):
                    kst_ref[s, 0, h, i, :, :] = kt[i * HEAD_DIM:(i + 1) * HEAD_DIM,
                                                   s * seq_len:(s + 1) * seq_len]

    hv = section(2 * ATT_WIDTH, 3 * ATT_WIDTH)
    hvt = hv.T.astype(BF16)
    for h in range(N_HEADS):
        vt_ref[h] = hvt[h * VAL_DIM:(h + 1) * VAL_DIM, :]
        if not rope:
            for s in range(tm // seq_len):
                vst_ref[s, 0, h, :, :] = hv[s * seq_len:(s + 1) * seq_len,
                                            h * VAL_DIM:(h + 1) * VAL_DIM]


def _proj(x2d, mods3, g1, w_in, qg, kg, rope_tabs, sgun, sguw, sgub, mlpg,
          *, seq_len, mods_row_fn):
    n_tok = x2d.shape[0]
    tm = TM_PROJ
    rope = rope_tabs is not None
    blocks_per_seq = seq_len // tm if rope else None

    in_specs = [
        pl.BlockSpec((tm, D_MODEL), lambda i: (i, 0)),
        _const_spec((MODS_ROWS, 6 * D_MODEL)),
        _const_spec((1, D_MODEL)),
        _const_spec((D_MODEL, IN_WIDTH)),
        _const_spec((LANES, LANES)),
        _const_spec((LANES, LANES)),
    ]
    args = [x2d, mods3, g1, w_in, qg, kg]
    if rope:
        tab_spec = pl.BlockSpec((LANES, tm), lambda i: (0, i % blocks_per_seq))
        in_specs += [tab_spec] * 2
        args += list(rope_tabs)
    in_specs += [
        _const_spec((1, MLP_WIDTH)),
        _const_spec((N_GROUPS, CHUNK, CHUNK)),
        _const_spec((N_GROUPS, CHUNK, GROUP_DIM)),
        _const_spec((1, MLP_WIDTH)),
    ]
    args += [sgun, sguw, sgub, mlpg]

    head_spec = pl.BlockSpec((N_HEADS, tm, LANES), lambda i: (0, i, 0))
    head_t_spec = pl.BlockSpec((N_HEADS, LANES, tm), lambda i: (0, 0, i))
    head_shape = jax.ShapeDtypeStruct((N_HEADS, n_tok, LANES), BF16)
    head_t_shape = jax.ShapeDtypeStruct((N_HEADS, LANES, n_tok), BF16)
    out_specs = [head_t_spec, head_spec, head_t_spec,
                 pl.BlockSpec((tm, MLP_WIDTH), lambda i: (i, 0))]
    out_shape = [head_t_shape, head_shape, head_t_shape,
                 jax.ShapeDtypeStruct((n_tok, MLP_WIDTH), BF16)]
    if not rope:
        n_seq = n_tok // seq_len
        spb = tm // seq_len
        out_specs += [
            pl.BlockSpec((spb, 1, N_HEADS, 2, HEAD_DIM, seq_len), lambda i: (i, 0, 0, 0, 0, 0)),
            pl.BlockSpec((spb, 1, N_HEADS, seq_len, VAL_DIM), lambda i: (i, 0, 0, 0, 0)),
        ]
        out_shape += [
            jax.ShapeDtypeStruct((n_seq, 1, N_HEADS, 2, HEAD_DIM, seq_len), F32),
            jax.ShapeDtypeStruct((n_seq, 1, N_HEADS, seq_len, VAL_DIM), F32),
        ]

    return pl.pallas_call(
        functools.partial(_proj_kernel, rope=rope, seq_len=seq_len, tm=tm, mods_row_fn=mods_row_fn),
        grid=(n_tok // tm,),
        in_specs=in_specs,
        out_specs=out_specs,
        out_shape=out_shape,
        scratch_shapes=[pltpu.VMEM((tm, MLP_WIDTH), F32)],
        compiler_params=_params(1),
        name="proj_rope" if rope else "proj_ctx",
    )(*args)


def _lambda_full(lq1, lk1, lq2, lk2, lambda_init):
    return (jnp.exp(jnp.sum(lq1[...] * lk1[...], keepdims=True))
            - jnp.exp(jnp.sum(lq2[...] * lk2[...], keepdims=True))
            + lambda_init)


def _map_queries(qt):
    row = lax.broadcasted_iota(jnp.int32, qt.shape, 0)
    zero = jnp.zeros_like(qt)
    return (jnp.where(row < HEAD_DIM, qt, zero), jnp.where(row >= HEAD_DIM, qt, zero))


def _combine_maps(o1, d1, o2, d2, lam, out_gain):
    ot = o1 * (1.0 / d1) - o2 * (lam / d2)
    ot = ot * lax.rsqrt(jnp.mean(ot * ot, axis=0, keepdims=True) + EPS)
    return (ot.T * out_gain).astype(BF16)


def _attn_ctx_kernel(lq1, lk1, lq2, lk2, qt_ref, k_ref, vt_ref, ag_ref, o_ref, st_buf,
                     *, lambda_init, seq_len):
    lam = _lambda_full(lq1, lk1, lq2, lk2, lambda_init)
    units = [(slice(s * seq_len, (s + 1) * seq_len), h)
             for s in range(k_ref.shape[1] // seq_len) for h in range(N_HEADS)]

    def scores(u):
        rows, h = units[u]
        kk = k_ref[h, rows, :]
        maxes = []
        for mp, qm in enumerate(_map_queries(qt_ref[h, :, rows])):
            st = jnp.dot(kk, qm, preferred_element_type=F32)
            st_buf[u % CTX_AHEAD, mp] = st
            maxes.append(jnp.max(st, axis=0, keepdims=True))
        return maxes

    def finish(u, maxes):
        rows, h = units[u]
        vt = vt_ref[h, :, rows]
        outs, dens = [], []
        for mp in range(2):
            e = jnp.exp2(st_buf[u % CTX_AHEAD, mp] - maxes[mp])
            dens.append(jnp.sum(e, axis=0, keepdims=True))
            outs.append(jnp.dot(vt, e.astype(BF16), preferred_element_type=F32))
        out_gain = (1.0 - lambda_init) * ag_ref[h]
        o_ref[h, rows, :] = _combine_maps(outs[0], dens[0], outs[1], dens[1], lam, out_gain)

    pending = [scores(u) for u in range(CTX_AHEAD - 1)]
    for u in range(len(units)):
        if u + CTX_AHEAD - 1 < len(units):
            pending.append(scores(u + CTX_AHEAD - 1))
        finish(u, pending.pop(0))


def _attn_ctx(lams, q, k, vt, att_g, *, seq_len, lambda_init):
    n_tok = k.shape[1]
    tm = TM_ATTN_CTX
    head_spec = pl.BlockSpec((N_HEADS, tm, LANES), lambda i: (0, i, 0))
    head_t_spec = pl.BlockSpec((N_HEADS, LANES, tm), lambda i: (0, 0, i))
    return pl.pallas_call(
        functools.partial(_attn_ctx_kernel, lambda_init=lambda_init, seq_len=seq_len),
        grid=(n_tok // tm,),
        in_specs=[_const_spec((1, HEAD_DIM))] * 4 + [
            head_t_spec,
            head_spec,
            head_t_spec,
            _const_spec((N_HEADS, 1, LANES)),
        ],
        out_specs=head_spec,
        out_shape=jax.ShapeDtypeStruct((N_HEADS, n_tok, LANES), BF16),
        scratch_shapes=[pltpu.VMEM((CTX_AHEAD, 2, seq_len, seq_len), F32)],
        compiler_params=_params(1),
        name="attn_ctx",
    )(*lams, q, k, vt, att_g)


def _attn_cache_kernel(lq1, lk1, lq2, lk2, qt_ref, k_ref, vt_ref, kct_ref, vc_ref, ag_ref,
                       *rest, lambda_init, n_new, n_weights):
    for w_ref, wb_ref in zip(rest[:n_weights], rest[n_weights + 1:2 * n_weights + 1]):
        wb_ref[...] = w_ref[...].astype(BF16)
    o_ref = rest[n_weights]
    k_all, vt_all, m_buf, acc_buf, *bufs = rest[2 * n_weights + 1:]
    st = (bufs[0:2], bufs[2:4])
    n_chunks = n_new // TQ_UNIT
    n_units = N_HEADS * n_chunks
    n_keys = k_all.shape[1]

    past = n_keys - n_new
    for h in range(N_HEADS):
        k_all[h, 0:n_new, :] = k_ref[h]
        k_all[h, n_new:, :] = kct_ref[0, 0, h].reshape(2 * HEAD_DIM, past).T.astype(BF16)
        vt_all[h, 0:VAL_DIM, 0:n_new] = vt_ref[h]
        vt_all[h, 0:VAL_DIM, n_new:] = vc_ref[0, 0, h].T.astype(BF16)
        vt_all[h, VAL_DIM:, :] = jnp.ones((ONES_ROWS, n_keys), BF16)

    lam = _lambda_full(lq1, lk1, lq2, lk2, lambda_init)

    def head_rows(u):
        c = u % n_chunks
        return u // n_chunks, pl.ds(pl.multiple_of(c * TQ_UNIT, TQ_UNIT), TQ_UNIT)

    def stage(fin, sc, defer_out=False):
        if sc is not None:
            sc_head, sc_rows = head_rows(sc[0])
            qms = _map_queries(qt_ref[sc_head, :, sc_rows])
            mrun = [None, None]
        if fin is not None:
            fin_head = fin[0] // n_chunks
            ms = [m_buf[fin[1], mp] for mp in range(2)]
            accs = [None, None]
        n_kb = n_keys // KEY_BLOCK
        lead = SCORE_LEAD if (sc is not None and fin is not None) else 0
        for step in range(n_kb + lead):
            if sc is not None and step < n_kb:
                kr = slice(step * KEY_BLOCK, (step + 1) * KEY_BLOCK)
                kk = k_all[sc_head, kr, :]
                for mp in range(2):
                    s = jnp.dot(kk, qms[mp], preferred_element_type=F32)
                    st[sc[1]][mp][kr, :] = s
                    smax = jnp.max(s.reshape(KEY_BLOCK // 8, 8, TQ_UNIT), axis=0)
                    mrun[mp] = smax if mrun[mp] is None else jnp.maximum(mrun[mp], smax)
            if fin is not None and step >= lead:
                kr = slice((step - lead) * KEY_BLOCK, (step - lead + 1) * KEY_BLOCK)
                vt = vt_all[fin_head, :, kr]
                for mp in range(2):
                    p = jnp.exp2(st[fin[1]][mp][kr, :] - ms[mp]).astype(BF16)
                    d = jnp.dot(vt, p, preferred_element_type=F32)
                    accs[mp] = d if accs[mp] is None else accs[mp] + d
        if sc is not None:
            for mp in range(2):
                m_buf[sc[1], mp] = jnp.max(mrun[mp], axis=0, keepdims=True)
        if fin is not None:
            if defer_out:
                for mp in range(2):
                    acc_buf[mp] = accs[mp]
            else:
                write_out(fin[0], accs[0], accs[1])

    def write_out(u, o1, o2):
        head, rows = head_rows(u)
        out_gain = (1.0 - lambda_init) * ag_ref[head]
        o_ref[head, rows, :] = _combine_maps(
            o1[0:VAL_DIM, :], o1[VAL_DIM:VAL_DIM + 1, :],
            o2[0:VAL_DIM, :], o2[VAL_DIM:VAL_DIM + 1, :], lam, out_gain)

    stage(None, (0, 0))
    stage((0, 0), (1, 1), defer_out=True)

    def pair(i):
        u = 2 * i
        write_out(u - 2, acc_buf[0], acc_buf[1])
        stage((u - 1, 1), (u, 0))
        stage((u, 0), (u + 1, 1), defer_out=True)

    def trip(j, carry):
        for p in range(PAIRS_PER_TRIP):
            pair(PAIRS_PER_TRIP * j + 1 + p)
        return carry

    n_pairs = n_units // 2
    n_trips = (n_pairs - 1) // PAIRS_PER_TRIP
    lax.fori_loop(0, n_trips, trip, 0)
    for i in range(n_trips * PAIRS_PER_TRIP + 1, n_pairs):
        pair(i)
    write_out(n_units - 2, acc_buf[0], acc_buf[1])
    stage((n_units - 1, 1), None)


def _attn_cache(lams, q, k, vt, kct, vc, att_g, weights, *, layer, n_batch, seq_len, lambda_init):
    past = vc.shape[3]
    n_keys = seq_len + past
    assert (seq_len // TQ_UNIT) % 2 == 0 and seq_len // TQ_UNIT >= 4
    assert n_keys % KEY_BLOCK == 0
    head_spec = pl.BlockSpec((N_HEADS, seq_len, LANES), lambda b: (0, b, 0))
    head_t_spec = pl.BlockSpec((N_HEADS, LANES, seq_len), lambda b: (0, 0, b))
    in_specs = [_const_spec((1, HEAD_DIM))] * 4 + [
        head_t_spec,
        head_spec,
        head_t_spec,
        pl.BlockSpec((1, 1, N_HEADS, 2, HEAD_DIM, past), lambda b: (b, layer, 0, 0, 0, 0)),
        pl.BlockSpec((1, 1, N_HEADS, past, VAL_DIM), lambda b: (b, layer, 0, 0, 0)),
        _const_spec((N_HEADS, 1, LANES)),
    ]
    w_specs = []
    for w in weights:
        slab = w.shape[0] // n_batch
        assert w.shape[0] % n_batch == 0 and slab % 16 == 0
        w_specs.append(pl.BlockSpec((slab, w.shape[1]), lambda b: (b, 0)))
    return pl.pallas_call(
        functools.partial(_attn_cache_kernel, lambda_init=lambda_init, n_new=seq_len,
                          n_weights=len(weights)),
        grid=(n_batch,),
        in_specs=in_specs + w_specs,
        out_specs=[head_spec] + w_specs,
        out_shape=[jax.ShapeDtypeStruct((N_HEADS, n_batch * seq_len, LANES), BF16)]
                  + [jax.ShapeDtypeStruct(w.shape, BF16) for w in weights],
        scratch_shapes=([pltpu.VMEM((N_HEADS, n_keys, LANES), BF16),
                         pltpu.VMEM((N_HEADS, VAL_DIM + ONES_ROWS, n_keys), BF16),
                         pltpu.VMEM((2, 2, 1, TQ_UNIT), F32),
                         pltpu.VMEM((2, VAL_DIM + ONES_ROWS, TQ_UNIT), F32)]
                        + [pltpu.VMEM((n_keys, TQ_UNIT), F32)] * 4),
        compiler_params=_params(1),
        name="attn_cache",
    )(*lams, q, k, vt, kct, vc, att_g, *weights)


def _ffn_kernel(xp_ref, attp_ref, mlpp_ref, xs_ref, atts_ref, mlps_ref, mods_ref, g2_ref,
                wo_ref, wfi_ref, wfo_ref, op_ref, os_ref, *, n_ctx_blk, dec_row_fn):
    i = pl.program_id(0)

    @pl.when(i < n_ctx_blk)
    def _():
        _ffn_block(xp_ref, attp_ref, mlpp_ref, mods_ref, g2_ref, wo_ref, wfi_ref, wfo_ref, op_ref,
                   CTX_ROW)

    @pl.when(i >= n_ctx_blk)
    def _():
        _ffn_block(xs_ref, atts_ref, mlps_ref, mods_ref, g2_ref, wo_ref, wfi_ref, wfo_ref, os_ref,
                   dec_row_fn(i - n_ctx_blk))


def _ffn_block(x_ref, att_ref, mlp_ref, mods_ref, g2_ref, wo_ref, wfi_ref, wfo_ref, o_ref, row):
    gate1, shift2, scale2, gate2 = (_mod(mods_ref, row, k) for k in (2, 3, 4, 5))
    tm = x_ref.shape[0]
    halves = [slice(i * (tm // 2), (i + 1) * (tm // 2)) for i in range(2)]
    x1s, xbs = [], []
    for r in halves:
        att = jnp.concatenate([att_ref[h, r, :] for h in range(N_HEADS)], axis=1)
        y = (jnp.dot(att, wo_ref[0:ATT_WIDTH, :], preferred_element_type=F32)
             + jnp.dot(mlp_ref[r, :], wo_ref[ATT_WIDTH:, :], preferred_element_type=F32))
        x1 = x_ref[r, :] + gate1 * y
        xn = x1 * _rms_scale(x1) * g2_ref[...]
        x1s.append(x1)
        xbs.append((xn * (1.0 + scale2) + shift2).astype(BF16))
    accs = [None, None]
    for c0, cw in FF_CHUNKS:
        pre = []
        for xb in xbs:
            gte = jnp.dot(xb, wfi_ref[:, c0:c0 + cw], preferred_element_type=F32)
            up = jnp.dot(xb, wfi_ref[:, D_FF + c0:D_FF + c0 + cw], preferred_element_type=F32)
            pre.append((gte, up))
        for i, (gte, up) in enumerate(pre):
            act = (gte * jax.nn.sigmoid(gte) * up).astype(BF16)
            part = jnp.dot(act, wfo_ref[c0:c0 + cw, :], preferred_element_type=F32)
            accs[i] = part if accs[i] is None else accs[i] + part
    for r, x1, acc in zip(halves, x1s, accs):
        o_ref[r, :] = x1 + gate2 * acc


def _ffn(ctx_inputs, dec_inputs, mods, g2, w_out, w_ffn_in, w_ffn_out, *, dec_row_fn):
    tm = TM_FFN
    n_p = ctx_inputs[0].shape[0] // tm
    n_s = dec_inputs[0].shape[0] // tm
    ctx_blk = lambda i: jnp.minimum(i, n_p - 1)
    dec_blk = lambda i: jnp.maximum(i - n_p, 0)

    def token_specs(blk):
        return [pl.BlockSpec((tm, D_MODEL), lambda i: (blk(i), 0)),
                pl.BlockSpec((N_HEADS, tm, LANES), lambda i: (0, blk(i), 0)),
                pl.BlockSpec((tm, MLP_WIDTH), lambda i: (blk(i), 0))]

    return pl.pallas_call(
        functools.partial(_ffn_kernel, n_ctx_blk=n_p, dec_row_fn=dec_row_fn),
        grid=(n_p + n_s,),
        in_specs=token_specs(ctx_blk) + token_specs(dec_blk) + [
            _const_spec((MODS_ROWS, 6 * D_MODEL)),
            _const_spec((1, D_MODEL)),
            _const_spec((D_MODEL, D_MODEL)),
            _const_spec((D_MODEL, 2 * D_FF)),
            _const_spec((D_FF, D_MODEL)),
        ],
        out_specs=[pl.BlockSpec((tm, D_MODEL), lambda i: (ctx_blk(i), 0)),
                   pl.BlockSpec((tm, D_MODEL), lambda i: (dec_blk(i), 0))],
        out_shape=[jax.ShapeDtypeStruct(ctx_inputs[0].shape, F32),
                   jax.ShapeDtypeStruct(dec_inputs[0].shape, F32)],
        compiler_params=_params(1),
        name="ffn",
    )(*ctx_inputs, *dec_inputs, mods, g2, w_out, w_ffn_in, w_ffn_out)


def _rope_tables(n):
    pos = np.arange(n)
    row = (pos // GRID_W).astype(np.float32)
    col = (pos % GRID_W).astype(np.float32)
    inv = (ROPE_THETA ** (-np.arange(0, ROPE_AXIS_DIM, 2, dtype=np.float32) / ROPE_AXIS_DIM)
           ).astype(np.float32)
    ang_r = row[:, None] * inv[None, :]
    ang_c = col[:, None] * inv[None, :]
    cos64 = np.concatenate([np.cos(ang_r)] * 2 + [np.cos(ang_c)] * 2, axis=1)
    sin64 = np.concatenate([-np.sin(ang_r), np.sin(ang_r), -np.sin(ang_c), np.sin(ang_c)], axis=1)
    return (np.ascontiguousarray(np.tile(cos64, (1, 2)).T, np.float32),
            np.ascontiguousarray(np.tile(sin64, (1, 2)).T, np.float32))


def kernel(x_prompt, x_sample, cache_k_ctx, cache_v_ctx, c, c_ctx, norm1_g, norm2_g, w_ada, b_ada, w_in, q_norm_g, k_norm_g, lambda_q1, lambda_k1, lambda_q2, lambda_k2, att_out_g, sgu_norm_g, sgu_w, sgu_b, mlp_out_g, w_out, w_ffn_in, w_ffn_out):
    n_ctx, ctx_len, _ = x_prompt.shape
    n_dec, dec_len, _ = x_sample.shape
    depth = norm1_g.shape[0]

    rope_tabs = _rope_tables(dec_len)
    q_scale = LOG2E / math.sqrt(HEAD_DIM)

    xp = x_prompt.reshape(n_ctx * ctx_len, D_MODEL)
    xs = x_sample.reshape(n_dec * dec_len, D_MODEL)
    cache_kt = jnp.swapaxes(cache_k_ctx, -1, -2)
    k_states, v_states = [], []
    ctx_row = lambda i: CTX_ROW
    dec_row_proj = lambda i: i // (dec_len // TM_PROJ)
    dec_row_ffn = lambda i: i // (dec_len // TM_FFN)

    for l in range(depth):
        lambda_init = 0.8 - 0.6 * math.exp(-0.3 * l)
        mods3, w_in_b, sguw = _mods(c, c_ctx[None, :], w_ada[l], b_ada[l][None, :],
                                    (w_in[l], sgu_w[l].reshape(N_GROUPS * CHUNK, CHUNK)))
        sguw = sguw.reshape(N_GROUPS, CHUNK, CHUNK)
        g1 = norm1_g[l][None, :]
        g2 = norm2_g[l][None, :]
        qg = jnp.broadcast_to(jnp.tile(q_norm_g[l] * q_scale, 2)[:, None], (LANES, LANES))
        kg = jnp.broadcast_to(jnp.tile(k_norm_g[l], 2)[:, None], (LANES, LANES))
        sgun = sgu_norm_g[l][None, :]
        sgub = jnp.broadcast_to(sgu_b[l][:, :, None], (N_GROUPS, CHUNK, GROUP_DIM))
        mlpg = mlp_out_g[l][None, :]
        att_g = att_out_g[l].reshape(N_HEADS, 1, VAL_DIM)
        lams = (lambda_q1[l][None, :], lambda_k1[l][None, :],
                lambda_q2[l][None, :], lambda_k2[l][None, :])

        q, k, vt, mlp_s = _proj(
            xs, mods3, g1, w_in_b, qg, kg, rope_tabs, sgun, sguw, sgub, mlpg,
            seq_len=dec_len, mods_row_fn=dec_row_proj)
        att_s, w_out_b, w_fi_b, w_fo_b = _attn_cache(
            lams, q, k, vt, cache_kt, cache_v_ctx, att_g, (w_out[l], w_ffn_in[l], w_ffn_out[l]),
            layer=l, n_batch=n_dec, seq_len=dec_len, lambda_init=lambda_init)

        q, k, vt, mlp, k_c, v_c = _proj(
            xp, mods3, g1, w_in_b, qg, kg, None, sgun, sguw, sgub, mlpg,
            seq_len=ctx_len, mods_row_fn=ctx_row)
        att = _attn_ctx(lams, q, k, vt, att_g, seq_len=ctx_len, lambda_init=lambda_init)
        k_states.append(k_c)
        v_states.append(v_c)

        xp, xs = _ffn((xp, att, mlp), (xs, att_s, mlp_s), mods3, g2, w_out_b, w_fi_b, w_fo_b,
                      dec_row_fn=dec_row_ffn)

    state_k = jnp.swapaxes(jnp.concatenate(k_states, axis=1), -1, -2)
    state_v = jnp.concatenate(v_states, axis=1)
    return (xp.reshape(n_ctx, ctx_len, D_MODEL), xs.reshape(n_dec, dec_len, D_MODEL),
            state_k, state_v)
```

```python
import functools
import math

import jax
import jax.numpy as jnp
import numpy as np
from jax import lax
from jax.experimental import pallas as pl
from jax.experimental.pallas import tpu as pltpu

D_MODEL = 1024
ATT_WIDTH = 512
N_HEADS = 4
HEAD_DIM = 64
VAL_DIM = 128
MLP_WIDTH = 512
N_GROUPS = 4
GROUP_DIM = 128
CHUNK = 128
D_FF = 2816
IN_WIDTH = 2560
GRID_W = 64
ROPE_THETA = 10000.0
ROPE_AXIS_DIM = 32
EPS = 1e-6
LOG2E = 1.4426950408889634
LANES = 128

F32 = jnp.float32
BF16 = jnp.bfloat16

VMEM_LIMIT_BYTES = 56 * 1024 * 1024
MODS_ROWS = 16
CTX_ROW = 8

TM_PROJ = 1024
TM_FFN = 512
TM_ATTN_CTX = 2048
CTX_AHEAD = 3
FF_CHUNKS = ((0, 1024), (1024, 1024), (2048, 768))

ONES_ROWS = 16
TQ_UNIT = 256
KEY_BLOCK = 256
SCORE_LEAD = 2
PAIRS_PER_TRIP = 5

def _const_spec(shape):
    zeros = (0,) * len(shape)
    return pl.BlockSpec(shape, lambda *_: zeros, pipeline_mode=pl.Buffered(1))


def _params(n_grid):
    return pltpu.CompilerParams(
        dimension_semantics=("arbitrary",) * n_grid,
        vmem_limit_bytes=VMEM_LIMIT_BYTES,
    )


def _rms_scale(x):
    return lax.rsqrt(jnp.mean(x * x, axis=-1, keepdims=True) + EPS)


def _mod(mods_ref, row, k):
    return mods_ref[pl.ds(row, 1), k * D_MODEL:(k + 1) * D_MODEL]


def _mods_kernel(c_ref, cctx_ref, w_ref, b_ref, *rest):
    n_weights = (len(rest) - 1) // 2
    o_ref = rest[n_weights]
    cnd = jnp.concatenate(
        [c_ref[...], jnp.broadcast_to(cctx_ref[...], (MODS_ROWS - CTX_ROW, D_MODEL))], axis=0)
    act = (cnd * jax.nn.sigmoid(cnd)).astype(BF16)
    o_ref[...] = jnp.dot(act, w_ref[...].astype(BF16), preferred_element_type=F32) + b_ref[...]
    for w32_ref, w16_ref in zip(rest[:n_weights], rest[n_weights + 1:]):
        w16_ref[...] = w32_ref[...].astype(BF16)


def _mods(c, c_ctx, w_ada, b_ada, weights):
    tn = 1536
    n_out = w_ada.shape[1]
    n_steps = n_out // tn
    assert c.shape[0] == CTX_ROW
    w_specs = []
    for w in weights:
        slab = w.shape[0] // n_steps
        assert w.shape[0] % n_steps == 0 and slab % 16 == 0
        w_specs.append(pl.BlockSpec((slab, w.shape[1]), lambda j: (j, 0)))
    return pl.pallas_call(
        _mods_kernel,
        grid=(n_steps,),
        in_specs=[
            _const_spec((CTX_ROW, D_MODEL)),
            _const_spec((1, D_MODEL)),
            pl.BlockSpec((D_MODEL, tn), lambda j: (0, j)),
            pl.BlockSpec((1, tn), lambda j: (0, j)),
        ] + w_specs,
        out_specs=[pl.BlockSpec((MODS_ROWS, tn), lambda j: (0, j))] + w_specs,
        out_shape=[jax.ShapeDtypeStruct((MODS_ROWS, n_out), F32)]
                  + [jax.ShapeDtypeStruct(w.shape, BF16) for w in weights],
        compiler_params=_params(1),
        name="mods",
    )(c, c_ctx, w_ada, b_ada, *weights)


def _proj_kernel(*refs, rope, seq_len, tm, mods_row_fn):
    it = iter(refs)
    x_ref, mods_ref, g1_ref, w_in_ref, qg_ref, kg_ref = (next(it) for _ in range(6))
    if rope:
        cos_ref, sin_ref = (next(it) for _ in range(2))
    sgun_ref, sguw_ref, sgub_ref, mlpg_ref = (next(it) for _ in range(4))
    qt_ref, k_ref, vt_ref, mlp_ref = (next(it) for _ in range(4))
    if not rope:
        kst_ref, vst_ref = (next(it) for _ in range(2))
    gate_ref = next(it)

    x = x_ref[...]
    xn = x * _rms_scale(x) * g1_ref[...]
    row = mods_row_fn(pl.program_id(0))
    xm = xn * (1.0 + _mod(mods_ref, row, 1)) + _mod(mods_ref, row, 0)
    xb = xm.astype(BF16)

    def section(lo, hi):
        return jnp.dot(xb, w_in_ref[:, lo:hi], preferred_element_type=F32)

    def head_t(sec, h, g_ref):
        t = sec[:, h * LANES:(h + 1) * LANES].T
        maps = []
        for mp in range(2):
            tmap = t[mp * HEAD_DIM:(mp + 1) * HEAD_DIM, :]
            maps.append(tmap * lax.rsqrt(jnp.mean(tmap * tmap, axis=0, keepdims=True) + EPS))
        gain = jnp.concatenate([g_ref[...]] * (tm // LANES), axis=1)
        tn = jnp.concatenate(maps, axis=0) * gain
        if rope:
            half = ROPE_AXIS_DIM // 2
            swapped = jnp.concatenate(
                [tn[r0 + off:r0 + off + half, :]
                 for r0 in range(0, LANES, ROPE_AXIS_DIM) for off in (half, 0)], axis=0)
            tn = tn * cos_ref[...] + swapped * sin_ref[...]
        return tn

    hu = section(3 * ATT_WIDTH, 3 * ATT_WIDTH + MLP_WIDTH)
    hg = section(3 * ATT_WIDTH + MLP_WIDTH, IN_WIDTH)
    for g in range(N_GROUPS):
        cols = slice(g * GROUP_DIM, (g + 1) * GROUP_DIM)
        gg = hg[:, cols]
        gn = (gg * _rms_scale(gg) * sgun_ref[:, cols]).astype(BF16)
        ug = hu[:, cols]
        wg = sguw_ref[g]
        bg = sgub_ref[g]
        for n in range(tm // CHUNK):
            rows = slice(n * CHUNK, (n + 1) * CHUNK)
            sp = jnp.dot(wg, gn[rows, :], preferred_element_type=F32) + bg
            gate_ref[rows, cols] = ug[rows, :] * sp
    o = gate_ref[...]
    mlp_ref[...] = (o * _rms_scale(o) * mlpg_ref[...]).astype(BF16)

    hq = section(0, ATT_WIDTH)
    hk = section(ATT_WIDTH, 2 * ATT_WIDTH)
    for h in range(N_HEADS):
        qt_ref[h] = head_t(hq, h, qg_ref).astype(BF16)
        kt = head_t(hk, h, kg_ref)
        k_ref[h] = kt.T.astype(BF16)
        if not rope:
            for s in range(tm // seq_len):
                for i in range(2):
                    kst_ref[s, 0, h, i, :, :] = kt[i * HEAD_DIM:(i + 1) * HEAD_DIM,
                                                   s * seq_len:(s + 1) * seq_len]

    hv = section(2 * ATT_WIDTH, 3 * ATT_WIDTH)
    hvt = hv.T.astype(BF16)
    for h in range(N_HEADS):
        vt_ref[h] = hvt[h * VAL_DIM:(h + 1) * VAL_DIM, :]
        if not rope:
            for s in range(tm // seq_len):
                vst_ref[s, 0, h, :, :] = hv[s * seq_len:(s + 1) * seq_len,
                                            h * VAL_DIM:(h + 1) * VAL_DIM]


def _proj(x2d, mods3, g1, w_in, qg, kg, rope_tabs, sgun, sguw, sgub, mlpg,
          *, seq_len, mods_row_fn):
    n_tok = x2d.shape[0]
    tm = TM_PROJ
    rope = rope_tabs is not None
    blocks_per_seq = seq_len // tm if rope else None

    in_specs = [
        pl.BlockSpec((tm, D_MODEL), lambda i: (i, 0)),
        _const_spec((MODS_ROWS, 6 * D_MODEL)),
        _const_spec((1, D_MODEL)),
        _const_spec((D_MODEL, IN_WIDTH)),
        _const_spec((LANES, LANES)),
        _const_spec((LANES, LANES)),
    ]
    args = [x2d, mods3, g1, w_in, qg, kg]
    if rope:
        tab_spec = pl.BlockSpec((LANES, tm), lambda i: (0, i % blocks_per_seq))
        in_specs += [tab_spec] * 2
        args += list(rope_tabs)
    in_specs += [
        _const_spec((1, MLP_WIDTH)),
        _const_spec((N_GROUPS, CHUNK, CHUNK)),
        _const_spec((N_GROUPS, CHUNK, GROUP_DIM)),
        _const_spec((1, MLP_WIDTH)),
    ]
    args += [sgun, sguw, sgub, mlpg]

    head_spec = pl.BlockSpec((N_HEADS, tm, LANES), lambda i: (0, i, 0))
    head_t_spec = pl.BlockSpec((N_HEADS, LANES, tm), lambda i: (0, 0, i))
    head_shape = jax.ShapeDtypeStruct((N_HEADS, n_tok, LANES), BF16)
    head_t_shape = jax.ShapeDtypeStruct((N_HEADS, LANES, n_tok), BF16)
    out_specs = [head_t_spec, head_spec, head_t_spec,
                 pl.BlockSpec((tm, MLP_WIDTH), lambda i: (i, 0))]
    out_shape = [head_t_shape, head_shape, head_t_shape,
                 jax.ShapeDtypeStruct((n_tok, MLP_WIDTH), BF16)]
    if not rope:
        n_seq = n_tok // seq_len
        spb = tm // seq_len
        out_specs += [
            pl.BlockSpec((spb, 1, N_HEADS, 2, HEAD_DIM, seq_len), lambda i: (i, 0, 0, 0, 0, 0)),
            pl.BlockSpec((spb, 1, N_HEADS, seq_len, VAL_DIM), lambda i: (i, 0, 0, 0, 0)),
        ]
        out_shape += [
            jax.ShapeDtypeStruct((n_seq, 1, N_HEADS, 2, HEAD_DIM, seq_len), F32),
            jax.ShapeDtypeStruct((n_seq, 1, N_HEADS, seq_len, VAL_DIM), F32),
        ]

    return pl.pallas_call(
        functools.partial(_proj_kernel, rope=rope, seq_len=seq_len, tm=tm, mods_row_fn=mods_row_fn),
        grid=(n_tok // tm,),
        in_specs=in_specs,
        out_specs=out_specs,
        out_shape=out_shape,
        scratch_shapes=[pltpu.VMEM((tm, MLP_WIDTH), F32)],
        compiler_params=_params(1),
        name="proj_rope" if rope else "proj_ctx",
    )(*args)


def _lambda_full(lq1, lk1, lq2, lk2, lambda_init):
    return (jnp.exp(jnp.sum(lq1[...] * lk1[...], keepdims=True))
            - jnp.exp(jnp.sum(lq2[...] * lk2[...], keepdims=True))
            + lambda_init)


def _map_queries(qt):
    row = lax.broadcasted_iota(jnp.int32, qt.shape, 0)
    zero = jnp.zeros_like(qt)
    return (jnp.where(row < HEAD_DIM, qt, zero), jnp.where(row >= HEAD_DIM, qt, zero))


def _combine_maps(o1, d1, o2, d2, lam, out_gain):
    ot = o1 * (1.0 / d1) - o2 * (lam / d2)
    ot = ot * lax.rsqrt(jnp.mean(ot * ot, axis=0, keepdims=True) + EPS)
    return (ot.T * out_gain).astype(BF16)


def _attn_ctx_kernel(lq1, lk1, lq2, lk2, qt_ref, k_ref, vt_ref, ag_ref, o_ref, st_buf,
                     *, lambda_init, seq_len):
    lam = _lambda_full(lq1, lk1, lq2, lk2, lambda_init)
    units = [(slice(s * seq_len, (s + 1) * seq_len), h)
             for s in range(k_ref.shape[1] // seq_len) for h in range(N_HEADS)]

    def scores(u):
        rows, h = units[u]
        kk = k_ref[h, rows, :]
        maxes = []
        for mp, qm in enumerate(_map_queries(qt_ref[h, :, rows])):
            st = jnp.dot(kk, qm, preferred_element_type=F32)
            st_buf[u % CTX_AHEAD, mp] = st
            maxes.append(jnp.max(st, axis=0, keepdims=True))
        return maxes

    def finish(u, maxes):
        rows, h = units[u]
        vt = jnp.concatenate([vt_ref[h, :, rows], jnp.ones((ONES_ROWS, seq_len), BF16)], axis=0)
        outs, dens = [], []
        for mp in range(2):
            e = jnp.exp2(st_buf[u % CTX_AHEAD, mp] - maxes[mp]).astype(BF16)
            o = jnp.dot(vt, e, preferred_element_type=F32)
            outs.append(o[0:VAL_DIM, :])
            dens.append(o[VAL_DIM:VAL_DIM + 1, :])
        out_gain = (1.0 - lambda_init) * ag_ref[h]
        o_ref[h, rows, :] = _combine_maps(outs[0], dens[0], outs[1], dens[1], lam, out_gain)

    pending = [scores(u) for u in range(CTX_AHEAD - 1)]
    for u in range(len(units)):
        if u + CTX_AHEAD - 1 < len(units):
            pending.append(scores(u + CTX_AHEAD - 1))
        finish(u, pending.pop(0))


def _attn_ctx(lams, q, k, vt, att_g, *, seq_len, lambda_init):
    n_tok = k.shape[1]
    tm = TM_ATTN_CTX
    head_spec = pl.BlockSpec((N_HEADS, tm, LANES), lambda i: (0, i, 0))
    head_t_spec = pl.BlockSpec((N_HEADS, LANES, tm), lambda i: (0, 0, i))
    return pl.pallas_call(
        functools.partial(_attn_ctx_kernel, lambda_init=lambda_init, seq_len=seq_len),
        grid=(n_tok // tm,),
        in_specs=[_const_spec((1, HEAD_DIM))] * 4 + [
            head_t_spec,
            head_spec,
            head_t_spec,
            _const_spec((N_HEADS, 1, LANES)),
        ],
        out_specs=head_spec,
        out_shape=jax.ShapeDtypeStruct((N_HEADS, n_tok, LANES), BF16),
        scratch_shapes=[pltpu.VMEM((CTX_AHEAD, 2, seq_len, seq_len), F32)],
        compiler_params=_params(1),
        name="attn_ctx",
    )(*lams, q, k, vt, att_g)


def _attn_cache_kernel(lq1, lk1, lq2, lk2, qt_ref, k_ref, vt_ref, kct_ref, vc_ref, ag_ref,
                       *rest, lambda_init, n_new, n_weights):
    for w_ref, wb_ref in zip(rest[:n_weights], rest[n_weights + 1:2 * n_weights + 1]):
        wb_ref[...] = w_ref[...].astype(BF16)
    o_ref = rest[n_weights]
    k_all, vt_all, m_buf, acc_buf, *bufs = rest[2 * n_weights + 1:]
    st = (bufs[0:2], bufs[2:4])
    n_chunks = n_new // TQ_UNIT
    n_units = N_HEADS * n_chunks
    n_keys = k_all.shape[1]

    past = n_keys - n_new
    for h in range(N_HEADS):
        k_all[h, 0:n_new, :] = k_ref[h]
        k_all[h, n_new:, :] = kct_ref[0, 0, h].reshape(2 * HEAD_DIM, past).T.astype(BF16)
        vt_all[h, 0:VAL_DIM, 0:n_new] = vt_ref[h]
        vt_all[h, 0:VAL_DIM, n_new:] = vc_ref[0, 0, h].T.astype(BF16)

    @pl.when(pl.program_id(0) == 0)
    def _():
        for h in range(N_HEADS):
            vt_all[h, VAL_DIM:, :] = jnp.ones((ONES_ROWS, n_keys), BF16)

    lam = _lambda_full(lq1, lk1, lq2, lk2, lambda_init)

    def head_rows(u):
        c = u % n_chunks
        return u // n_chunks, pl.ds(pl.multiple_of(c * TQ_UNIT, TQ_UNIT), TQ_UNIT)

    def stage(fin, sc, defer_out=False):
        if sc is not None:
            sc_head, sc_rows = head_rows(sc[0])
            qms = _map_queries(qt_ref[sc_head, :, sc_rows])
            mrun = [None, None]
        if fin is not None:
            fin_head = fin[0] // n_chunks
            ms = [m_buf[fin[1], mp] for mp in range(2)]
            accs = [None, None]
        n_kb = n_keys // KEY_BLOCK
        lead = SCORE_LEAD if (sc is not None and fin is not None) else 0
        for step in range(n_kb + lead):
            if sc is not None and step < n_kb:
                kr = slice(step * KEY_BLOCK, (step + 1) * KEY_BLOCK)
                kk = k_all[sc_head, kr, :]
                for mp in range(2):
                    s = jnp.dot(kk, qms[mp], preferred_element_type=F32)
                    st[sc[1]][mp][kr, :] = s
                    smax = jnp.max(s.reshape(KEY_BLOCK // 8, 8, TQ_UNIT), axis=0)
                    mrun[mp] = smax if mrun[mp] is None else jnp.maximum(mrun[mp], smax)
            if fin is not None and step >= lead:
                kr = slice((step - lead) * KEY_BLOCK, (step - lead + 1) * KEY_BLOCK)
                vt = vt_all[fin_head, :, kr]
                for mp in range(2):
                    p = jnp.exp2(st[fin[1]][mp][kr, :] - ms[mp]).astype(BF16)
                    d = jnp.dot(vt, p, preferred_element_type=F32)
                    accs[mp] = d if accs[mp] is None else accs[mp] + d
        if sc is not None:
            for mp in range(2):
                m_buf[sc[1], mp] = jnp.max(mrun[mp], axis=0, keepdims=True)
        if fin is not None:
            if defer_out:
                for mp in range(2):
                    acc_buf[mp] = accs[mp]
            else:
                write_out(fin[0], accs[0], accs[1])

    def write_out(u, o1, o2):
        head, rows = head_rows(u)
        out_gain = (1.0 - lambda_init) * ag_ref[head]
        o_ref[head, rows, :] = _combine_maps(
            o1[0:VAL_DIM, :], o1[VAL_DIM:VAL_DIM + 1, :],
            o2[0:VAL_DIM, :], o2[VAL_DIM:VAL_DIM + 1, :], lam, out_gain)

    stage(None, (0, 0))
    stage((0, 0), (1, 1), defer_out=True)

    def pair(i):
        u = 2 * i
        write_out(u - 2, acc_buf[0], acc_buf[1])
        stage((u - 1, 1), (u, 0))
        stage((u, 0), (u + 1, 1), defer_out=True)

    def trip(j, carry):
        for p in range(PAIRS_PER_TRIP):
            pair(PAIRS_PER_TRIP * j + 1 + p)
        return carry

    n_pairs = n_units // 2
    n_trips = (n_pairs - 1) // PAIRS_PER_TRIP
    lax.fori_loop(0, n_trips, trip, 0)
    for i in range(n_trips * PAIRS_PER_TRIP + 1, n_pairs):
        pair(i)
    write_out(n_units - 2, acc_buf[0], acc_buf[1])
    stage((n_units - 1, 1), None)


def _attn_cache(lams, q, k, vt, kct, vc, att_g, weights, *, layer, n_batch, seq_len, lambda_init):
    past = vc.shape[3]
    n_keys = seq_len + past
    assert (seq_len // TQ_UNIT) % 2 == 0 and seq_len // TQ_UNIT >= 4
    assert n_keys % KEY_BLOCK == 0
    head_spec = pl.BlockSpec((N_HEADS, seq_len, LANES), lambda b: (0, b, 0))
    head_t_spec = pl.BlockSpec((N_HEADS, LANES, seq_len), lambda b: (0, 0, b))
    in_specs = [_const_spec((1, HEAD_DIM))] * 4 + [
        head_t_spec,
        head_spec,
        head_t_spec,
        pl.BlockSpec((1, 1, N_HEADS, 2, HEAD_DIM, past), lambda b: (b, layer, 0, 0, 0, 0)),
        pl.BlockSpec((1, 1, N_HEADS, past, VAL_DIM), lambda b: (b, layer, 0, 0, 0)),
        _const_spec((N_HEADS, 1, LANES)),
    ]
    w_specs = []
    for w in weights:
        slab = w.shape[0] // n_batch
        assert w.shape[0] % n_batch == 0 and slab % 16 == 0
        w_specs.append(pl.BlockSpec((slab, w.shape[1]), lambda b: (b, 0)))
    return pl.pallas_call(
        functools.partial(_attn_cache_kernel, lambda_init=lambda_init, n_new=seq_len,
                          n_weights=len(weights)),
        grid=(n_batch,),
        in_specs=in_specs + w_specs,
        out_specs=[head_spec] + w_specs,
        out_shape=[jax.ShapeDtypeStruct((N_HEADS, n_batch * seq_len, LANES), BF16)]
                  + [jax.ShapeDtypeStruct(w.shape, BF16) for w in weights],
        scratch_shapes=([pltpu.VMEM((N_HEADS, n_keys, LANES), BF16),
                         pltpu.VMEM((N_HEADS, VAL_DIM + ONES_ROWS, n_keys), BF16),
                         pltpu.VMEM((2, 2, 1, TQ_UNIT), F32),
                         pltpu.VMEM((2, VAL_DIM + ONES_ROWS, TQ_UNIT), F32)]
                        + [pltpu.VMEM((n_keys, TQ_UNIT), F32)] * 4),
        compiler_params=_params(1),
        name="attn_cache",
    )(*lams, q, k, vt, kct, vc, att_g, *weights)


def _ffn_kernel(xp_ref, attp_ref, mlpp_ref, xs_ref, atts_ref, mlps_ref, mods_ref, g2_ref,
                wo_ref, wfi_ref, wfo_ref, op_ref, os_ref, *, n_ctx_blk, dec_row_fn):
    i = pl.program_id(0)

    @pl.when(i < n_ctx_blk)
    def _():
        _ffn_block(xp_ref, attp_ref, mlpp_ref, mods_ref, g2_ref, wo_ref, wfi_ref, wfo_ref, op_ref,
                   CTX_ROW)

    @pl.when(i >= n_ctx_blk)
    def _():
        _ffn_block(xs_ref, atts_ref, mlps_ref, mods_ref, g2_ref, wo_ref, wfi_ref, wfo_ref, os_ref,
                   dec_row_fn(i - n_ctx_blk))


def _ffn_block(x_ref, att_ref, mlp_ref, mods_ref, g2_ref, wo_ref, wfi_ref, wfo_ref, o_ref, row):
    gate1, shift2, scale2, gate2 = (_mod(mods_ref, row, k) for k in (2, 3, 4, 5))
    tm = x_ref.shape[0]
    halves = [slice(i * (tm // 2), (i + 1) * (tm // 2)) for i in range(2)]
    x1s, xbs = [], []
    for r in halves:
        att = jnp.concatenate([att_ref[h, r, :] for h in range(N_HEADS)], axis=1)
        y = (jnp.dot(att, wo_ref[0:ATT_WIDTH, :], preferred_element_type=F32)
             + jnp.dot(mlp_ref[r, :], wo_ref[ATT_WIDTH:, :], preferred_element_type=F32))
        x1 = x_ref[r, :] + gate1 * y
        xn = x1 * _rms_scale(x1) * g2_ref[...]
        x1s.append(x1)
        xbs.append((xn * (1.0 + scale2) + shift2).astype(BF16))
    accs = [None, None]
    for c0, cw in FF_CHUNKS:
        pre = []
        for xb in xbs:
            gte = jnp.dot(xb, wfi_ref[:, c0:c0 + cw], preferred_element_type=F32)
            up = jnp.dot(xb, wfi_ref[:, D_FF + c0:D_FF + c0 + cw], preferred_element_type=F32)
            pre.append((gte, up))
        for i, (gte, up) in enumerate(pre):
            act = (gte * jax.nn.sigmoid(gte) * up).astype(BF16)
            part = jnp.dot(act, wfo_ref[c0:c0 + cw, :], preferred_element_type=F32)
            accs[i] = part if accs[i] is None else accs[i] + part
    for r, x1, acc in zip(halves, x1s, accs):
        o_ref[r, :] = x1 + gate2 * acc


def _ffn(ctx_inputs, dec_inputs, mods, g2, w_out, w_ffn_in, w_ffn_out, *, dec_row_fn):
    tm = TM_FFN
    n_p = ctx_inputs[0].shape[0] // tm
    n_s = dec_inputs[0].shape[0] // tm
    ctx_blk = lambda i: jnp.minimum(i, n_p - 1)
    dec_blk = lambda i: jnp.maximum(i - n_p, 0)

    def token_specs(blk):
        return [pl.BlockSpec((tm, D_MODEL), lambda i: (blk(i), 0)),
                pl.BlockSpec((N_HEADS, tm, LANES), lambda i: (0, blk(i), 0)),
                pl.BlockSpec((tm, MLP_WIDTH), lambda i: (blk(i), 0))]

    return pl.pallas_call(
        functools.partial(_ffn_kernel, n_ctx_blk=n_p, dec_row_fn=dec_row_fn),
        grid=(n_p + n_s,),
        in_specs=token_specs(ctx_blk) + token_specs(dec_blk) + [
            _const_spec((MODS_ROWS, 6 * D_MODEL)),
            _const_spec((1, D_MODEL)),
            _const_spec((D_MODEL, D_MODEL)),
            _const_spec((D_MODEL, 2 * D_FF)),
            _const_spec((D_FF, D_MODEL)),
        ],
        out_specs=[pl.BlockSpec((tm, D_MODEL), lambda i: (ctx_blk(i), 0)),
                   pl.BlockSpec((tm, D_MODEL), lambda i: (dec_blk(i), 0))],
        out_shape=[jax.ShapeDtypeStruct(ctx_inputs[0].shape, F32),
                   jax.ShapeDtypeStruct(dec_inputs[0].shape, F32)],
        compiler_params=_params(1),
        name="ffn",
    )(*ctx_inputs, *dec_inputs, mods, g2, w_out, w_ffn_in, w_ffn_out)


def _rope_tables(n):
    pos = np.arange(n)
    row = (pos // GRID_W).astype(np.float32)
    col = (pos % GRID_W).astype(np.float32)
    inv = (ROPE_THETA ** (-np.arange(0, ROPE_AXIS_DIM, 2, dtype=np.float32) / ROPE_AXIS_DIM)
           ).astype(np.float32)
    ang_r = row[:, None] * inv[None, :]
    ang_c = col[:, None] * inv[None, :]
    cos64 = np.concatenate([np.cos(ang_r)] * 2 + [np.cos(ang_c)] * 2, axis=1)
    sin64 = np.concatenate([-np.sin(ang_r), np.sin(ang_r), -np.sin(ang_c), np.sin(ang_c)], axis=1)
    return (np.ascontiguousarray(np.tile(cos64, (1, 2)).T, np.float32),
            np.ascontiguousarray(np.tile(sin64, (1, 2)).T, np.float32))


def kernel(x_prompt, x_sample, cache_k_ctx, cache_v_ctx, c, c_ctx, norm1_g, norm2_g, w_ada, b_ada, w_in, q_norm_g, k_norm_g, lambda_q1, lambda_k1, lambda_q2, lambda_k2, att_out_g, sgu_norm_g, sgu_w, sgu_b, mlp_out_g, w_out, w_ffn_in, w_ffn_out):
    n_ctx, ctx_len, _ = x_prompt.shape
    n_dec, dec_len, _ = x_sample.shape
    depth = norm1_g.shape[0]

    rope_tabs = _rope_tables(dec_len)
    q_scale = LOG2E / math.sqrt(HEAD_DIM)

    xp = x_prompt.reshape(n_ctx * ctx_len, D_MODEL)
    xs = x_sample.reshape(n_dec * dec_len, D_MODEL)
    cache_kt = jnp.swapaxes(cache_k_ctx, -1, -2)
    k_states, v_states = [], []
    ctx_row = lambda i: CTX_ROW
    dec_row_proj = lambda i: i // (dec_len // TM_PROJ)
    dec_row_ffn = lambda i: i // (dec_len // TM_FFN)

    for l in range(depth):
        lambda_init = 0.8 - 0.6 * math.exp(-0.3 * l)
        mods3, w_in_b, sguw = _mods(c, c_ctx[None, :], w_ada[l], b_ada[l][None, :],
                                    (w_in[l], sgu_w[l].reshape(N_GROUPS * CHUNK, CHUNK)))
        sguw = sguw.reshape(N_GROUPS, CHUNK, CHUNK)
        g1 = norm1_g[l][None, :]
        g2 = norm2_g[l][None, :]
        qg = jnp.broadcast_to(jnp.tile(q_norm_g[l] * q_scale, 2)[:, None], (LANES, LANES))
        kg = jnp.broadcast_to(jnp.tile(k_norm_g[l], 2)[:, None], (LANES, LANES))
        sgun = sgu_norm_g[l][None, :]
        sgub = jnp.broadcast_to(sgu_b[l][:, :, None], (N_GROUPS, CHUNK, GROUP_DIM))
        mlpg = mlp_out_g[l][None, :]
        att_g = att_out_g[l].reshape(N_HEADS, 1, VAL_DIM)
        lams = (lambda_q1[l][None, :], lambda_k1[l][None, :],
                lambda_q2[l][None, :], lambda_k2[l][None, :])

        q, k, vt, mlp_s = _proj(
            xs, mods3, g1, w_in_b, qg, kg, rope_tabs, sgun, sguw, sgub, mlpg,
            seq_len=dec_len, mods_row_fn=dec_row_proj)
        att_s, w_out_b, w_fi_b, w_fo_b = _attn_cache(
            lams, q, k, vt, cache_kt, cache_v_ctx, att_g, (w_out[l], w_ffn_in[l], w_ffn_out[l]),
            layer=l, n_batch=n_dec, seq_len=dec_len, lambda_init=lambda_init)

        q, k, vt, mlp, k_c, v_c = _proj(
            xp, mods3, g1, w_in_b, qg, kg, None, sgun, sguw, sgub, mlpg,
            seq_len=ctx_len, mods_row_fn=ctx_row)
        att = _attn_ctx(lams, q, k, vt, att_g, seq_len=ctx_len, lambda_init=lambda_init)
        k_states.append(k_c)
        v_states.append(v_c)

        xp, xs = _ffn((xp, att, mlp), (xs, att_s, mlp_s), mods3, g2, w_out_b, w_fi_b, w_fo_b,
                      dec_row_fn=dec_row_ffn)

    state_k = jnp.swapaxes(jnp.concatenate(k_states, axis=1), -1, -2)
    state_v = jnp.concatenate(v_states, axis=1)
    return (xp.reshape(n_ctx, ctx_len, D_MODEL), xs.reshape(n_dec, dec_len, D_MODEL),
            state_k, state_v)
```
